```python
import jax, jax.numpy as jnp
from jax import lax
import numpy as np

D_MODEL = 1024
BATCH = 16
SEQ = 2048
DEPTH = 4

N_MIXERS = 2
N_HEADS = 16
HEAD_DIM = D_MODEL // N_HEADS
RMS_EPS = 1e-6
NEG = -1e30
BIG = 1e30
NSA_KV_GROUPS = 4
NSA_HPG = N_HEADS // NSA_KV_GROUPS
KV_W = NSA_KV_GROUPS * HEAD_DIM
CMP_BLOCK = 32
CMP_STRIDE = 16
CMP_HIDDEN = 4 * HEAD_DIM
SLC_BLOCK = 64
N_SLC_BLOCKS = 8
WINDOW = 512
NSA_QBLOCK = 32
N_BRANCH = 3
NSA_IN_W = D_MODEL + 6 * KV_W + N_BRANCH * N_HEADS
SB_QBLOCK = 128
D_FF = ((8 * D_MODEL // 3 + 255) // 256) * 256
N_NSA_LAYERS = (DEPTH + 1) // 2
N_SB_LAYERS = DEPTH // 2

kernel_name = "nsa_stickbreaking_interleaved_trunk"


def _alibi_slopes(n):
    return jnp.asarray(2.0 ** (-8.0 * (np.arange(n) + 1) / n), dtype=jnp.float32)


def _rmsnorm(x, g):
    xf = x.astype(jnp.float32)
    y = xf * lax.rsqrt(jnp.mean(xf * xf, axis=-1, keepdims=True) + RMS_EPS)
    return (y * g.astype(jnp.float32)).astype(x.dtype)


def _swiglu(h, w1, w3, w2):
    return (jax.nn.silu(h @ w1) * (h @ w3)) @ w2


def _compress(kv, pe, w1, w2):
    b, s, g, dh = kv.shape
    r = CMP_BLOCK // CMP_STRIDE
    nch = s // CMP_STRIDE
    nc = nch - r + 1
    ch = kv.reshape(b, nch, CMP_STRIDE, g, dh)
    blk = jnp.concatenate([ch[:, i:i + nc] for i in range(r)], axis=2)
    blk = blk + pe[None, None, :, None, :]
    flat = jnp.moveaxis(blk, 3, 2).reshape(b, nc, g, CMP_BLOCK * dh)
    return jax.nn.gelu(flat @ w1) @ w2


def _cmp_to_slc_matrix(n_cmp, n_slc):
    cs = np.arange(n_cmp) * CMP_STRIDE
    ce = cs + CMP_BLOCK
    ss = np.arange(n_slc) * SLC_BLOCK
    se = ss + SLC_BLOCK
    ov = np.clip(np.minimum(ce[:, None], se[None]) - np.maximum(cs[:, None], ss[None]), 0, None)
    return jnp.asarray(ov / CMP_STRIDE, dtype=jnp.float32)


def _gather_blocks(blk, idx):
    return jax.vmap(jax.vmap(lambda kb, ib: kb[ib]))(blk, idx)


def _nsa_mixer(h, w_in, gate_b, cmp_pe, cmp_w1, cmp_w2, w_out, slopes):
    b, s, d = h.shape
    g_, hpg, dh = NSA_KV_GROUPS, NSA_HPG, HEAD_DIM
    proj = h @ w_in
    cuts = np.cumsum([d, KV_W, KV_W, KV_W, KV_W, KV_W, KV_W]).tolist()
    q, kc, vc, ksl, vsl, kw, vw, gt = jnp.split(proj, cuts, axis=-1)
    q = q.reshape(b, s, g_, hpg, dh) * (dh ** -0.5)
    gates = jax.nn.sigmoid(gt + gate_b).reshape(b, s, N_BRANCH, g_, hpg)
    kc = _compress(kc.reshape(b, s, g_, dh), cmp_pe[0], cmp_w1[0], cmp_w2[0])
    vc = _compress(vc.reshape(b, s, g_, dh), cmp_pe[1], cmp_w1[1], cmp_w2[1])
    n_cmp = kc.shape[1]
    n_slcb = s // SLC_BLOCK
    n_sel = min(N_SLC_BLOCKS, n_slcb)
    cmap = _cmp_to_slc_matrix(n_cmp, n_slcb)
    c_end = jnp.arange(n_cmp) * CMP_STRIDE + CMP_BLOCK - 1
    ks_blk = jnp.transpose(ksl.reshape(b, n_slcb, SLC_BLOCK, g_, dh), (0, 3, 1, 2, 4))
    vs_blk = jnp.transpose(vsl.reshape(b, n_slcb, SLC_BLOCK, g_, dh), (0, 3, 1, 2, 4))
    pad = ((0, 0), (WINDOW, 0), (0, 0), (0, 0))
    kw_pad = jnp.pad(kw.reshape(b, s, g_, dh), pad)
    vw_pad = jnp.pad(vw.reshape(b, s, g_, dh), pad)
    sl = slopes.reshape(g_, hpg)
    qb = NSA_QBLOCK

    def step(c):
        qs = c * qb
        qc = lax.dynamic_slice_in_dim(q, qs, qb, axis=1)
        gc = lax.dynamic_slice_in_dim(gates, qs, qb, axis=1)
        t = qs + jnp.arange(qb)
        dist_c = t[:, None] - c_end[None]
        valid_c = dist_c >= 0
        s_c = jnp.einsum('bqghd,bngd->bghqn', qc, kc, preferred_element_type=jnp.float32)
        s_c = s_c - sl[None, :, :, None, None] * dist_c.astype(jnp.float32)
        p_c = jax.nn.softmax(jnp.where(valid_c, s_c, NEG), axis=-1) * valid_c
        o_c = jnp.einsum('bghqn,bngd->bqghd', p_c.astype(vc.dtype), vc)
        imp = jnp.einsum('bghqn,nj->bgqj', p_c, cmap)
        cur = t // SLC_BLOCK
        jb = jnp.arange(n_slcb)
        causal_j = jb[None] <= cur[:, None]
        forced = (jb[None] == 0) | (jb[None] == cur[:, None]) | (jb[None] == cur[:, None] - 1)
        imp = jnp.where(causal_j & forced, BIG, jnp.where(causal_j, imp, -BIG))
        _, idx = lax.top_k(imp, n_sel)
        blk_ok = idx <= cur[None, None, :, None]
        k_sel = _gather_blocks(ks_blk, idx)
        v_sel = _gather_blocks(vs_blk, idx)
        pos_s = idx[..., None] * SLC_BLOCK + jnp.arange(SLC_BLOCK)
        dist_s = t[None, None, :, None, None] - pos_s
        valid_s = ((dist_s >= 0) & blk_ok[..., None])[:, :, None]
        s_s = jnp.einsum('bqghd,bgqnkd->bghqnk', qc, k_sel, preferred_element_type=jnp.float32)
        s_s = s_s - sl[None, :, :, None, None, None] * dist_s[:, :, None].astype(jnp.float32)
        s_s = jnp.where(valid_s, s_s, NEG)
        p_s = jax.nn.softmax(s_s.reshape(b, g_, hpg, qb, n_sel * SLC_BLOCK), axis=-1)
        p_s = p_s.reshape(s_s.shape) * valid_s
        o_s = jnp.einsum('bghqnk,bgqnkd->bqghd', p_s.astype(v_sel.dtype), v_sel)
        kwc = lax.dynamic_slice_in_dim(kw_pad, qs, WINDOW + qb, axis=1)
        vwc = lax.dynamic_slice_in_dim(vw_pad, qs, WINDOW + qb, axis=1)
        pos_w = qs - WINDOW + jnp.arange(WINDOW + qb)
        dist_w = t[:, None] - pos_w[None]
        valid_w = (dist_w >= 0) & (dist_w < WINDOW) & (pos_w[None] >= 0)
        s_w = jnp.einsum('bqghd,bkgd->bghqk', qc, kwc, preferred_element_type=jnp.float32)
        s_w = s_w - sl[None, :, :, None, None] * dist_w.astype(jnp.float32)
        p_w = jax.nn.softmax(jnp.where(valid_w, s_w, NEG), axis=-1)
        o_w = jnp.einsum('bghqk,bkgd->bqghd', p_w.astype(vwc.dtype), vwc)
        return (gc[:, :, 0, :, :, None] * o_c + gc[:, :, 1, :, :, None] * o_s
                + gc[:, :, 2, :, :, None] * o_w)

    outs = lax.map(step, jnp.arange(s // qb))
    o = jnp.moveaxis(outs, 0, 1).reshape(b, s, d)
    return o @ w_out


def _stick_breaking_mixer(h, w_in, w_out):
    b, s, d = h.shape
    q, k, v = jnp.split(h @ w_in, 3, axis=-1)
    q = jnp.transpose(q.reshape(b, s, N_HEADS, HEAD_DIM), (0, 2, 1, 3)) * (HEAD_DIM ** -0.5)
    k = jnp.transpose(k.reshape(b, s, N_HEADS, HEAD_DIM), (0, 2, 1, 3))
    v = jnp.transpose(v.reshape(b, s, N_HEADS, HEAD_DIM), (0, 2, 1, 3))
    qb = SB_QBLOCK
    kpos = jnp.arange(s)

    def step(c):
        qs = c * qb
        qc = lax.dynamic_slice_in_dim(q, qs, qb, axis=2)
        t = qs + jnp.arange(qb)
        before = kpos[None] < t[:, None]
        z = jnp.einsum('bhqd,bhkd->bhqk', qc, k, preferred_element_type=jnp.float32)
        log_1m_beta = jnp.where(before, -jax.nn.softplus(z), 0.0)
        suffix = lax.cumsum(log_1m_beta, axis=3, reverse=True) - log_1m_beta
        log_a = -jax.nn.softplus(-z) + suffix
        a = jnp.where(before, jnp.exp(log_a), 0.0)
        return jnp.einsum('bhqk,bhkd->bqhd', a.astype(v.dtype), v)

    outs = lax.map(step, jnp.arange(s // qb))
    o = jnp.moveaxis(outs, 0, 1).reshape(b, s, d)
    return o @ w_out


def setup_inputs(seed: int = 0) -> dict:
    key = jax.random.key(seed)
    ks = jax.random.split(key, 16)

    def nrm(k, shape, scale):
        return jax.random.normal(k, shape, jnp.float32) * scale

    na, nb = N_NSA_LAYERS, N_SB_LAYERS
    return {
        "x": nrm(ks[0], (BATCH, SEQ, D_MODEL), 1.0),
        "mix_norm_g": 1.0 + nrm(ks[1], (DEPTH, D_MODEL), 0.05),
        "ffn_norm_g": 1.0 + nrm(ks[2], (DEPTH, D_MODEL), 0.05),
        "final_norm_g": 1.0 + nrm(ks[3], (D_MODEL,), 0.05),
        "nsa_w_in": nrm(ks[4], (na, D_MODEL, NSA_IN_W), D_MODEL ** -0.5),
        "nsa_gate_b": nrm(ks[5], (na, N_BRANCH * N_HEADS), 0.1),
        "nsa_cmp_pe": nrm(ks[6], (na, 2, CMP_BLOCK, HEAD_DIM), 0.1),
        "nsa_cmp_w1": nrm(ks[7], (na, 2, CMP_BLOCK * HEAD_DIM, CMP_HIDDEN), (CMP_BLOCK * HEAD_DIM) ** -0.5),
        "nsa_cmp_w2": nrm(ks[8], (na, 2, CMP_HIDDEN, HEAD_DIM), CMP_HIDDEN ** -0.5),
        "nsa_w_out": nrm(ks[9], (na, D_MODEL, D_MODEL), D_MODEL ** -0.5),
        "sb_w_in": nrm(ks[10], (nb, D_MODEL, 3 * D_MODEL), D_MODEL ** -0.5),
        "sb_w_out": nrm(ks[11], (nb, D_MODEL, D_MODEL), D_MODEL ** -0.5),
        "ffn_w1": nrm(ks[12], (DEPTH, D_MODEL, D_FF), D_MODEL ** -0.5),
        "ffn_w3": nrm(ks[13], (DEPTH, D_MODEL, D_FF), D_MODEL ** -0.5),
        "ffn_w2": nrm(ks[14], (DEPTH, D_FF, D_MODEL), D_FF ** -0.5),
    }


def reference(x, mix_norm_g, ffn_norm_g, final_norm_g, nsa_w_in, nsa_gate_b, nsa_cmp_pe,
              nsa_cmp_w1, nsa_cmp_w2, nsa_w_out, sb_w_in, sb_w_out, ffn_w1, ffn_w3, ffn_w2):
    slopes = _alibi_slopes(N_HEADS)
    h = x
    for i in range(DEPTH):
        a = _rmsnorm(h, mix_norm_g[i])
        j = i // N_MIXERS
        if i % N_MIXERS == 0:
            mix = _nsa_mixer(a, nsa_w_in[j], nsa_gate_b[j], nsa_cmp_pe[j], nsa_cmp_w1[j],
                             nsa_cmp_w2[j], nsa_w_out[j], slopes)
        else:
            mix = _stick_breaking_mixer(a, sb_w_in[j], sb_w_out[j])
        h = h + mix
        h = h + _swiglu(_rmsnorm(h, ffn_norm_g[i]), ffn_w1[i], ffn_w3[i], ffn_w2[i])
    return _rmsnorm(h, final_norm_g)
```

```python
import functools

import numpy as np
import jax
import jax.numpy as jnp
from jax import lax
from jax.experimental import pallas as pl
from jax.experimental.pallas import tpu as pltpu

F32 = jnp.float32
BF16 = jnp.bfloat16

RMS_EPS = 1e-6
NEG = -1e30
BIG = 1e30

N_HEADS = 16
HEAD_DIM = 64
N_GROUPS = 4
HPG = N_HEADS // N_GROUPS
N_BRANCH = 3
CMP_BLOCK = 32
CMP_STRIDE = 16
SLC_BLOCK = 64
N_SLC_BLOCKS = 8
WINDOW = 512

LANES = 128
POS_LANE = 32
VMEM_LIMIT = 56 * 1024 * 1024

_NT = (((1,), (1,)), ((), ()))


def _dot(a, b):
    return jnp.dot(a, b, preferred_element_type=F32)


def _dot_nt(a, b):
    return lax.dot_general(a, b, _NT, preferred_element_type=F32)


def _rmsnorm(x, g):
    ms = jnp.mean(x * x, axis=-1, keepdims=True)
    return (x * lax.rsqrt(ms + RMS_EPS)) * g


def _split_bf16(x):
    hi = x.astype(BF16)
    lo = (x - hi.astype(F32)).astype(BF16)
    return hi, lo


def _params(*sem):
    return pltpu.CompilerParams(dimension_semantics=sem, vmem_limit_bytes=VMEM_LIMIT)


def _const_spec(shape):
    nd = len(shape)
    return pl.BlockSpec(shape, lambda *_: (0,) * nd)


def _nsa_proj_kernel(x_ref, g_ref, wm_ref, wc_ref, wg_ref, gb_ref, main_ref, cv_ref, gate_ref):
    a = _rmsnorm(x_ref[...], g_ref[...]).astype(BF16)
    main_ref[...] = _dot(a, wm_ref[...]).astype(main_ref.dtype)
    cv = _dot(a, wc_ref[...])
    for kv in range(2):
        for g in range(N_GROUPS):
            c0 = (kv * N_GROUPS + g) * HEAD_DIM
            cv_ref[kv, g, :, :] = cv[:, c0:c0 + HEAD_DIM]
    gate_ref[...] = jax.nn.sigmoid(_dot(a, wg_ref[...]) + gb_ref[...])


def _nsa_proj(h, g, w_main, w_cv, w_gate, gate_b, tm):
    b, s, d = h.shape
    nm, ng = w_main.shape[1], w_gate.shape[1]
    return pl.pallas_call(
        _nsa_proj_kernel,
        grid=(b, s // tm),
        in_specs=[
            pl.BlockSpec((None, tm, d), lambda bi, r: (bi, r, 0)),
            _const_spec((1, d)),
            _const_spec(w_main.shape),
            _const_spec(w_cv.shape),
            _const_spec(w_gate.shape),
            _const_spec((1, ng)),
        ],
        out_specs=[
            pl.BlockSpec((None, tm, nm), lambda bi, r: (bi, r, 0)),
            pl.BlockSpec((None, 2, N_GROUPS, tm, HEAD_DIM), lambda bi, r: (bi, 0, 0, r, 0)),
            pl.BlockSpec((None, tm, ng), lambda bi, r: (bi, r, 0)),
        ],
        out_shape=[
            jax.ShapeDtypeStruct((b, s, nm), BF16),
            jax.ShapeDtypeStruct((b, 2, N_GROUPS, s, HEAD_DIM), F32),
            jax.ShapeDtypeStruct((b, s, ng), F32),
        ],
        compiler_params=_params("parallel", "parallel"),
        name="nsa_proj",
    )(h, g, w_main, w_cv, w_gate, gate_b)


def _sb_proj_kernel(x_ref, g_ref, w_ref, o_ref):
    a = _rmsnorm(x_ref[...], g_ref[...]).astype(BF16)
    o_ref[...] = _dot(a, w_ref[...]).astype(o_ref.dtype)


def _sb_proj(h, g, w, tm):
    b, s, d = h.shape
    n = w.shape[1]
    return pl.pallas_call(
        _sb_proj_kernel,
        grid=(b, s // tm),
        in_specs=[
            pl.BlockSpec((None, tm, d), lambda bi, r: (bi, r, 0)),
            _const_spec((1, d)),
            _const_spec(w.shape),
        ],
        out_specs=pl.BlockSpec((None, tm, n), lambda bi, r: (bi, r, 0)),
        out_shape=jax.ShapeDtypeStruct((b, s, n), BF16),
        compiler_params=_params("parallel", "parallel"),
        name="sb_proj",
    )(h, g, w)


def _nsa_compress_kernel(c_ref, pe_ref, w1_ref, w2_ref, o_ref):
    rows = c_ref.shape[1]
    half = c_ref.shape[2]
    out = jnp.zeros(o_ref.shape, F32)
    for kv in range(2):
        c = c_ref[kv]
        top = (c + pe_ref[kv, 0:1, :]).astype(BF16)
        bot = (c + pe_ref[kv, 1:2, :]).astype(BF16)
        a_top = _dot(top, w1_ref[kv, 0:half, :])
        a_bot = _dot(bot, w1_ref[kv, half:2 * half, :])
        hid = jax.nn.gelu(a_top + pltpu.roll(a_bot, rows - 1, 0), approximate=True)
        out = out + _dot(hid.astype(BF16), w2_ref[kv])
    o_ref[...] = out


def _nsa_compress(c, pe, w1, w2pad):
    b, _, rows, width = c.shape
    return pl.pallas_call(
        _nsa_compress_kernel,
        grid=(b,),
        in_specs=[
            pl.BlockSpec((None, 2, rows, width), lambda bi: (bi, 0, 0, 0)),
            _const_spec(pe.shape),
            _const_spec(w1.shape),
            _const_spec(w2pad.shape),
        ],
        out_specs=pl.BlockSpec((None, rows, LANES), lambda bi: (bi, 0, 0)),
        out_shape=jax.ShapeDtypeStruct((b, rows, LANES), F32),
        compiler_params=_params("parallel"),
        name="nsa_compress",
    )(c, pe, w1, w2pad)


def _nsa_attn_kernel(q_ref, kk_ref, vv_ref, cmp_ref, gate_ref, kpos_ref, cmapt_ref, hconst_ref,
                     o_ref, kaug_ref, qsel_ref, qwin_ref, *, tq, n_slc, n_sel):
    i = pl.program_id(2)
    t0 = i * tq
    tk = tq

    @pl.when(i == 0)
    def _():
        kaug_ref[:, 0:LANES] = kk_ref[...]
        kaug_ref[:, LANES:2 * LANES] = kpos_ref[...]

    lane = lax.broadcasted_iota(jnp.int32, (tq, LANES), 1)
    low = lane < HEAD_DIM

    q4 = q_ref[...].astype(F32) * (HEAD_DIM ** -0.5)
    qa, qw = [], []
    for h in range(HPG):
        blk = q4[:, LANES * (h // 2):LANES * (h // 2 + 1)]
        rolled = pltpu.roll(blk, HEAD_DIM, 1)
        if h % 2 == 0:
            qa.append(jnp.where(low, blk, 0.0))
            qw.append(jnp.where(low, 0.0, rolled))
        else:
            qa.append(jnp.where(low, rolled, 0.0))
            qw.append(jnp.where(low, 0.0, blk))

    ck = cmp_ref[...].astype(BF16)
    nc = ck.shape[0]
    t_abs = t0 + lax.broadcasted_iota(jnp.int32, (tq, nc), 0)
    n_idx = lax.broadcasted_iota(jnp.int32, (tq, nc), 1)
    dist_c = t_abs - (n_idx * CMP_STRIDE + (CMP_BLOCK - 1))
    valid_c = dist_c >= 0
    dist_cf = dist_c.astype(F32)
    psum = jnp.zeros((tq, nc), F32)
    o_cmp = []
    for h in range(HPG):
        slope = hconst_ref[HPG + h:HPG + h + 1, 0:1]
        s_c = _dot_nt(qa[h].astype(BF16), ck) - slope * dist_cf
        s_c = jnp.where(valid_c, s_c, NEG)
        m_c = jnp.max(s_c, axis=1, keepdims=True)
        e_c = jnp.where(valid_c, jnp.exp(s_c - m_c), 0.0)
        den = jnp.sum(e_c, axis=1, keepdims=True)
        p_c = e_c / jnp.maximum(den, 1e-30)
        psum = psum + p_c
        o_cmp.append(_dot(p_c.astype(BF16), ck))

    cmt = cmapt_ref[...]
    jp = cmt.shape[0]
    p_hi, p_lo = _split_bf16(psum)
    imp_t = _dot_nt(cmt, p_hi) + _dot_nt(cmt, p_lo)
    jb = lax.broadcasted_iota(jnp.int32, (jp, tq), 0)
    cur = jnp.right_shift(t0 + lax.broadcasted_iota(jnp.int32, (jp, tq), 1), SLC_BLOCK.bit_length() - 1)
    causal_j = jb <= cur
    forced = (jb == 0) | (jb == cur) | (jb == cur - 1)
    val = jnp.where(causal_j & forced, BIG, jnp.where(causal_j, imp_t, -BIG))
    rank = jnp.zeros((jp, tq), jnp.int32)
    for j in range(n_slc):
        row = val[j:j + 1, :]
        beats = (row > val) | ((row == val) & (jb > j))
        rank = rank + beats.astype(jnp.int32)
    sel = causal_j & (rank < n_sel)
    mb_t = jnp.where(sel, 0.0, NEG)
    if jp < LANES:
        mb_t = jnp.concatenate([mb_t, jnp.zeros((LANES - jp, tq), F32)], axis=0)
    mb = mb_t.T

    for h in range(HPG):
        pieces = hconst_ref[h:h + 1, :]
        r0 = h * tq
        qsel_ref[r0:r0 + tq, 0:LANES] = qa[h].astype(BF16)
        qsel_ref[r0:r0 + tq, LANES:2 * LANES] = (mb + pieces).astype(BF16)
        qwin_ref[r0:r0 + tq, 0:LANES] = qw[h].astype(BF16)
        qwin_ref[r0:r0 + tq, LANES:2 * LANES] = jnp.broadcast_to(pieces, (tq, LANES)).astype(BF16)

    m_rows = HPG * tq
    t_loc = lax.broadcasted_iota(jnp.int32, (m_rows, tk), 0) & (tq - 1)
    k_loc = lax.broadcasted_iota(jnp.int32, (m_rows, tk), 1)
    rel = t_loc - k_loc
    causal_bias = jnp.where(rel >= 0, 0.0, NEG)

    def scores(qref, j):
        kt = kaug_ref[pl.ds(pl.multiple_of(j * tk, tk), tk), :]
        return _dot_nt(qref[...], kt)

    def pv(p, j):
        return _dot(p.astype(BF16), vv_ref[pl.ds(pl.multiple_of(j * tk, tk), tk), :])

    def first_tile(qref):
        s = scores(qref, i) + causal_bias
        m = jnp.max(s, axis=1, keepdims=True)
        p = jnp.exp(s - m)
        return m, jnp.sum(p, axis=1, keepdims=True), pv(p, i)

    def update(carry, s, j):
        m, l, acc = carry
        m_new = jnp.maximum(m, jnp.max(s, axis=1, keepdims=True))
        alpha = jnp.exp(m - m_new)
        p = jnp.exp(s - m_new)
        return m_new, alpha * l + jnp.sum(p, axis=1, keepdims=True), alpha * acc + pv(p, j)

    def sel_body(jj, carry):
        j = i - 1 - jj
        return update(carry, scores(qsel_ref, j), j)

    _, l_s, acc_s = lax.fori_loop(0, i, sel_body, first_tile(qsel_ref))
    o_sel = acc_s / l_s

    def win_body(jj, carry):
        j = i - jj
        dist = rel + jj * tk
        bias = jnp.where((dist >= 0) & (dist < WINDOW), 0.0, NEG)
        return update(carry, scores(qwin_ref, j) + bias, j)

    n_win = jnp.minimum(i, (WINDOW + tk - 1) // tk) + 1
    _, l_w, acc_w = lax.fori_loop(1, n_win, win_body, first_tile(qwin_ref))
    o_win = acc_w / l_w

    gt = gate_ref[...]
    for pair in range(HPG // 2):
        res = []
        for h in (2 * pair, 2 * pair + 1):
            g_c = gt[:, h:h + 1]
            g_s = gt[:, HPG + h:HPG + h + 1]
            g_w = gt[:, 2 * HPG + h:2 * HPG + h + 1]
            oc, osel, ow = o_cmp[h], o_sel[h * tq:(h + 1) * tq], o_win[h * tq:(h + 1) * tq]
            if h % 2 == 0:
                oc, ow = pltpu.roll(oc, HEAD_DIM, 1), pltpu.roll(ow, HEAD_DIM, 1)
            else:
                osel = pltpu.roll(osel, HEAD_DIM, 1)
            res.append(g_c * oc + g_s * osel + g_w * ow)
        o_ref[:, LANES * pair:LANES * (pair + 1)] = jnp.where(low, res[0], res[1]).astype(o_ref.dtype)


def _nsa_attn(main, cmp, gates, kpos, cmapt, hconst, tq):
    b, s, _ = main.shape
    nc = cmp.shape[1] // N_GROUPS
    n_slc = s // SLC_BLOCK
    gw = HPG * HEAD_DIM
    per_g = (gw + 2 * LANES) // LANES
    kern = functools.partial(_nsa_attn_kernel, tq=tq, n_slc=n_slc, n_sel=min(N_SLC_BLOCKS, n_slc))
    return pl.pallas_call(
        kern,
        grid=(b, N_GROUPS, s // tq),
        in_specs=[
            pl.BlockSpec((None, tq, gw), lambda bi, g, i: (bi, i, g * per_g // 2)),
            pl.BlockSpec((None, s, LANES), lambda bi, g, i: (bi, 0, g * per_g + 2)),
            pl.BlockSpec((None, s, LANES), lambda bi, g, i: (bi, 0, g * per_g + 3)),
            pl.BlockSpec((None, nc, LANES), lambda bi, g, i: (bi, g, 0)),
            pl.BlockSpec((None, tq, LANES), lambda bi, g, i: (bi, i, g)),
            _const_spec(kpos.shape),
            _const_spec(cmapt.shape),
            pl.BlockSpec((None, 2 * HPG, LANES), lambda bi, g, i: (g, 0, 0)),
        ],
        out_specs=pl.BlockSpec((None, tq, gw), lambda bi, g, i: (bi, i, g)),
        out_shape=jax.ShapeDtypeStruct((b, s, N_GROUPS * gw), BF16),
        scratch_shapes=[
            pltpu.VMEM((s, 2 * LANES), BF16),
            pltpu.VMEM((HPG * tq, 2 * LANES), BF16),
            pltpu.VMEM((HPG * tq, 2 * LANES), BF16),
        ],
        compiler_params=_params("parallel", "parallel", "arbitrary"),
        name="nsa_attn",
    )(main, main, main, cmp, gates, kpos, cmapt, hconst)


def _sb_attn_kernel(q_ref, k_ref, v_ref, tri_ref, o_ref, *, tq):
    i = pl.program_id(2)
    tk = tq
    lane = lax.broadcasted_iota(jnp.int32, (tq, LANES), 1)
    low = lane < HEAD_DIM
    before = (lax.broadcasted_iota(jnp.int32, (tq, tk), 1)
              < lax.broadcasted_iota(jnp.int32, (tq, tk), 0))
    q2 = q_ref[...].astype(F32) * (HEAD_DIM ** -0.5)
    tri = tri_ref[...]

    def tile(qh, j, diag):
        off = pl.multiple_of(j * tk, tk)
        z = _dot_nt(qh, k_ref[pl.ds(off, tk), :])
        sp = jnp.maximum(z, 0.0) + jnp.log1p(jnp.exp(-jnp.abs(z)))
        if diag:
            sp = jnp.where(before, sp, 0.0)
        hi, lo = _split_bf16(sp)
        suffix = _dot(hi, tri) + _dot(lo, tri)
        return z, sp, suffix, v_ref[pl.ds(off, tk), :]

    outs = []
    for hd in range(2):
        qh = jnp.where(low == (hd == 0), q2, 0.0).astype(BF16)
        z, sp, suffix, vj = tile(qh, i, True)
        a = jnp.where(before, jnp.exp(z - sp + suffix), 0.0)
        acc0 = _dot(a.astype(BF16), vj)
        c0 = -jnp.sum(sp, axis=1, keepdims=True)

        def body(jj, carry, qh=qh):
            acc, c = carry
            z, sp, suffix, vj = tile(qh, i - 1 - jj, False)
            a = jnp.exp(z - sp + suffix + c)
            return acc + _dot(a.astype(BF16), vj), c - jnp.sum(sp, axis=1, keepdims=True)

        acc, _ = lax.fori_loop(0, i, body, (acc0, c0))
        outs.append(acc)
    o_ref[...] = jnp.where(low, outs[0], outs[1]).astype(o_ref.dtype)


def _sb_attn(qkv, tri, tq):
    b, s, n3 = qkv.shape
    npair = n3 // 3 // LANES
    return pl.pallas_call(
        functools.partial(_sb_attn_kernel, tq=tq),
        grid=(b, npair, s // tq),
        in_specs=[
            pl.BlockSpec((None, tq, LANES), lambda bi, p, i: (bi, i, p)),
            pl.BlockSpec((None, s, LANES), lambda bi, p, i: (bi, 0, npair + p)),
            pl.BlockSpec((None, s, LANES), lambda bi, p, i: (bi, 0, 2 * npair + p)),
            _const_spec(tri.shape),
        ],
        out_specs=pl.BlockSpec((None, tq, LANES), lambda bi, p, i: (bi, i, p)),
        out_shape=jax.ShapeDtypeStruct((b, s, npair * LANES), BF16),
        compiler_params=_params("parallel", "parallel", "arbitrary"),
        name="sb_attn",
    )(qkv, qkv, qkv, tri)


def _mix_ffn_kernel(h_ref, o_ref, wo_ref, g_ref, w1_ref, w3_ref, w2_ref, fg_ref, out_ref, a_ref, acc_ref,
                    *, final):
    h1 = h_ref[...] + _dot(o_ref[...], wo_ref[...])
    a_ref[...] = _rmsnorm(h1, g_ref[...]).astype(BF16)
    acc_ref[...] = h1

    def body(c, carry):
        a = a_ref[...]
        u = _dot(a, w1_ref[c])
        v = _dot(a, w3_ref[c])
        t = (u * jax.nn.sigmoid(u)) * v
        acc_ref[...] += _dot(t.astype(BF16), w2_ref[c])
        return carry

    lax.fori_loop(0, w1_ref.shape[0], body, 0)
    h2 = acc_ref[...]
    out_ref[...] = _rmsnorm(h2, fg_ref[...]) if final else h2


def _mix_ffn(h, o, wo, g, w1, w3, w2, fg, tm, final):
    r, d = h.shape
    single = pl.Buffered(1)

    def wspec(shape):
        nd = len(shape)
        return pl.BlockSpec(shape, lambda *_: (0,) * nd, pipeline_mode=single)

    return pl.pallas_call(
        functools.partial(_mix_ffn_kernel, final=final),
        grid=(r // tm,),
        in_specs=[
            pl.BlockSpec((tm, d), lambda i: (i, 0)),
            pl.BlockSpec((tm, d), lambda i: (i, 0)),
            wspec(wo.shape),
            _const_spec((1, d)),
            wspec(w1.shape),
            wspec(w3.shape),
            wspec(w2.shape),
            _const_spec((1, d)),
        ],
        out_specs=pl.BlockSpec((tm, d), lambda i: (i, 0)),
        out_shape=jax.ShapeDtypeStruct((r, d), F32),
        scratch_shapes=[pltpu.VMEM((tm, d), BF16), pltpu.VMEM((tm, d), F32)],
        compiler_params=_params("parallel"),
        name="mix_ffn",
    )(h, o, wo, g, w1, w3, w2, fg)


def _nsa_layout(d):
    kvw = N_GROUPS * HEAD_DIM
    off = {"q": 0, "kc": d, "vc": d + kvw, "ksl": d + 2 * kvw, "vsl": d + 3 * kvw, "kw": d + 4 * kvw,
           "vw": d + 5 * kvw, "gt": d + 6 * kvw}
    main = []
    for g in range(N_GROUPS):
        main.append(np.arange(HPG * HEAD_DIM) + g * HPG * HEAD_DIM)
        for name in ("ksl", "kw", "vsl", "vw"):
            main.append(off[name] + g * HEAD_DIM + np.arange(HEAD_DIM))
    main = np.concatenate(main)
    cv = off["kc"] + np.arange(2 * kvw)
    gate = np.full((N_GROUPS * LANES,), -1, np.int64)
    for g in range(N_GROUPS):
        for br in range(N_BRANCH):
            for hh in range(HPG):
                gate[g * LANES + br * HPG + hh] = br * N_HEADS + g * HPG + hh
    return main, cv, gate, off["gt"]


def _bf16_pieces(x):
    x = np.asarray(x, np.float32)
    out = []
    for _ in range(3):
        p = x.astype(BF16).astype(np.float32)
        out.append(p)
        x = (x - p).astype(np.float32)
    return out


def _nsa_tables(s, tk):
    n_slc = s // SLC_BLOCK
    n_cmp = s // CMP_STRIDE - CMP_BLOCK // CMP_STRIDE + 1
    nc = s // CMP_STRIDE
    pos = np.arange(s)
    kpos = np.zeros((s, LANES), np.float32)
    kpos[pos, pos // SLC_BLOCK] = 1.0
    for c in range(3):
        kpos[:, POS_LANE + c] = (pos // tk) * tk
        kpos[:, POS_LANE + 3 + c] = pos % tk
    cs = np.arange(n_cmp) * CMP_STRIDE
    ss = np.arange(n_slc) * SLC_BLOCK
    ov = np.clip(np.minimum(cs[:, None] + CMP_BLOCK, ss[None] + SLC_BLOCK) - np.maximum(cs[:, None], ss[None]), 0, None)
    jp = -(-n_slc // 16) * 16
    cmapt = np.zeros((jp, nc), np.float32)
    cmapt[:n_slc, :n_cmp] = (ov / CMP_STRIDE).T
    slopes = np.asarray(2.0 ** (-8.0 * (np.arange(N_HEADS) + 1) / N_HEADS), np.float32)
    hconst = np.zeros((N_GROUPS, 2 * HPG, LANES), np.float32)
    pieces = _bf16_pieces(slopes)
    for g in range(N_GROUPS):
        for hh in range(HPG):
            for c in range(3):
                hconst[g, hh, POS_LANE + c] = pieces[c][g * HPG + hh]
                hconst[g, hh, POS_LANE + 3 + c] = pieces[c][g * HPG + hh]
            hconst[g, HPG + hh, :] = slopes[g * HPG + hh]
    return jnp.asarray(kpos, BF16), jnp.asarray(cmapt, BF16), jnp.asarray(hconst, F32)


def _ffn_chunk(d_ff):
    for tf in (512, 256, 128):
        if d_ff % tf == 0:
            return tf
    return d_ff


def kernel(x, mix_norm_g, ffn_norm_g, final_norm_g, nsa_w_in, nsa_gate_b, nsa_cmp_pe, nsa_cmp_w1, nsa_cmp_w2,
           nsa_w_out, sb_w_in, sb_w_out, ffn_w1, ffn_w3, ffn_w2):
    b, s, d = x.shape
    depth = mix_norm_g.shape[0]
    d_ff = ffn_w1.shape[2]
    tm = min(512, s)
    tq = min(256, s)
    tf = _ffn_chunk(d_ff)
    nch = d_ff // tf

    main_idx, cv_idx, gate_idx, gt_off = _nsa_layout(d)
    gate_valid = jnp.asarray(gate_idx >= 0)
    gate_cols = jnp.asarray(np.maximum(gate_idx, 0))
    kpos, cmapt, hconst = _nsa_tables(s, tq)
    tri = jnp.asarray(-np.tril(np.ones((tq, tq), np.float32), -1), BF16)
    chunk = CMP_STRIDE * HEAD_DIM
    nc = s // CMP_STRIDE

    h = x
    for i in range(depth):
        j = i // 2
        gm = mix_norm_g[i].reshape(1, d)
        if i % 2 == 0:
            w_in = nsa_w_in[j]
            w_main = w_in[:, main_idx].astype(BF16)
            w_cv = w_in[:, cv_idx].astype(BF16)
            w_gate = jnp.where(gate_valid[None, :], w_in[:, gt_off + gate_cols], 0.0).astype(BF16)
            gate_b = jnp.where(gate_valid, nsa_gate_b[j][gate_cols], 0.0).reshape(1, -1)
            main, cv, gates = _nsa_proj(h, gm, w_main, w_cv, w_gate, gate_b, tm)
            c = cv.reshape(b, 2, N_GROUPS * nc, chunk)
            pe = nsa_cmp_pe[j].reshape(2, 2, chunk)
            w2pad = jnp.zeros((2, nsa_cmp_w2.shape[2], LANES), F32)
            w2pad = w2pad.at[0, :, :HEAD_DIM].set(nsa_cmp_w2[j, 0]).at[1, :, HEAD_DIM:].set(nsa_cmp_w2[j, 1])
            cmp = _nsa_compress(c, pe, nsa_cmp_w1[j].astype(BF16), w2pad.astype(BF16))
            o = _nsa_attn(main, cmp, gates, kpos, cmapt, hconst, tq)
            w_out = nsa_w_out[j]
        else:
            qkv = _sb_proj(h, gm, sb_w_in[j].astype(BF16), tm)
            o = _sb_attn(qkv, tri, tq)
            w_out = sb_w_out[j]
        w1 = ffn_w1[i].reshape(d, nch, tf).transpose(1, 0, 2).astype(BF16)
        w3 = ffn_w3[i].reshape(d, nch, tf).transpose(1, 0, 2).astype(BF16)
        w2 = ffn_w2[i].reshape(nch, tf, d).astype(BF16)
        final = i == depth - 1
        h = _mix_ffn(h.reshape(b * s, d), o.reshape(b * s, d), w_out.astype(BF16), ffn_norm_g[i].reshape(1, d),
                     w1, w3, w2, final_norm_g.reshape(1, d), tm, final).reshape(b, s, d)
    return h
```

```python
import functools

import numpy as np
import jax
import jax.numpy as jnp
from jax import lax
from jax.experimental import pallas as pl
from jax.experimental.pallas import tpu as pltpu

F32 = jnp.float32
BF16 = jnp.bfloat16

RMS_EPS = 1e-6
NEG = -1e30
BIG = 1e30

N_HEADS = 16
HEAD_DIM = 64
N_GROUPS = 4
HPG = N_HEADS // N_GROUPS
N_BRANCH = 3
CMP_BLOCK = 32
CMP_STRIDE = 16
SLC_BLOCK = 64
N_SLC_BLOCKS = 8
WINDOW = 512

LANES = 128
POS_LANE = 32
VMEM_LIMIT = 56 * 1024 * 1024
SB_CHAINS = 4

_NT = (((1,), (1,)), ((), ()))


def _dot(a, b):
    return jnp.dot(a, b, preferred_element_type=F32)


def _dot_nt(a, b):
    return lax.dot_general(a, b, _NT, preferred_element_type=F32)


def _rmsnorm(x, g):
    ms = jnp.mean(x * x, axis=-1, keepdims=True)
    return (x * lax.rsqrt(ms + RMS_EPS)) * g


def _split_bf16(x):
    hi = x.astype(BF16)
    lo = (x - hi.astype(F32)).astype(BF16)
    return hi, lo


def _split_bf16_trunc(x):
    hi = lax.bitcast_convert_type(lax.bitcast_convert_type(x, jnp.uint32) & jnp.uint32(0xFFFF0000), F32)
    return hi.astype(BF16), (x - hi).astype(BF16)


def _params(*sem):
    return pltpu.CompilerParams(dimension_semantics=sem, vmem_limit_bytes=VMEM_LIMIT)


def _const_spec(shape):
    nd = len(shape)
    return pl.BlockSpec(shape, lambda *_: (0,) * nd)


def _nsa_proj_kernel(x_ref, g_ref, wm_ref, wc_ref, wg_ref, gb_ref, main_ref, cv_ref, gate_ref):
    a = _rmsnorm(x_ref[...], g_ref[...]).astype(BF16)
    main_ref[...] = _dot(a, wm_ref[...]).astype(main_ref.dtype)
    cv = _dot(a, wc_ref[...])
    for kv in range(2):
        for g in range(N_GROUPS):
            c0 = (kv * N_GROUPS + g) * HEAD_DIM
            cv_ref[kv, g, :, :] = cv[:, c0:c0 + HEAD_DIM]
    gate_ref[...] = jax.nn.sigmoid(_dot(a, wg_ref[...]) + gb_ref[...])


def _nsa_proj(h, g, w_main, w_cv, w_gate, gate_b, tm):
    b, s, d = h.shape
    nm, ng = w_main.shape[1], w_gate.shape[1]
    return pl.pallas_call(
        _nsa_proj_kernel,
        grid=(b, s // tm),
        in_specs=[
            pl.BlockSpec((None, tm, d), lambda bi, r: (bi, r, 0)),
            _const_spec((1, d)),
            _const_spec(w_main.shape),
            _const_spec(w_cv.shape),
            _const_spec(w_gate.shape),
            _const_spec((1, ng)),
        ],
        out_specs=[
            pl.BlockSpec((None, tm, nm), lambda bi, r: (bi, r, 0)),
            pl.BlockSpec((None, 2, N_GROUPS, tm, HEAD_DIM), lambda bi, r: (bi, 0, 0, r, 0)),
            pl.BlockSpec((None, tm, ng), lambda bi, r: (bi, r, 0)),
        ],
        out_shape=[
            jax.ShapeDtypeStruct((b, s, nm), BF16),
            jax.ShapeDtypeStruct((b, 2, N_GROUPS, s, HEAD_DIM), F32),
            jax.ShapeDtypeStruct((b, s, ng), F32),
        ],
        compiler_params=_params("parallel", "parallel"),
        name="nsa_proj",
    )(h, g, w_main, w_cv, w_gate, gate_b)


def _sb_proj_kernel(x_ref, g_ref, w_ref, o_ref):
    a = _rmsnorm(x_ref[...], g_ref[...]).astype(BF16)
    o_ref[...] = _dot(a, w_ref[...]).astype(o_ref.dtype)


def _sb_proj(h, g, w, tm):
    b, s, d = h.shape
    n = w.shape[1]
    return pl.pallas_call(
        _sb_proj_kernel,
        grid=(b, s // tm),
        in_specs=[
            pl.BlockSpec((None, tm, d), lambda bi, r: (bi, r, 0)),
            _const_spec((1, d)),
            _const_spec(w.shape),
        ],
        out_specs=pl.BlockSpec((None, tm, n), lambda bi, r: (bi, r, 0)),
        out_shape=jax.ShapeDtypeStruct((b, s, n), BF16),
        compiler_params=_params("parallel", "parallel"),
        name="sb_proj",
    )(h, g, w)


def _nsa_compress_kernel(c_ref, pe_ref, w1_ref, w2_ref, o_ref):
    rows = c_ref.shape[1]
    half = c_ref.shape[2]
    out = jnp.zeros(o_ref.shape, F32)
    for kv in range(2):
        c = c_ref[kv]
        top = (c + pe_ref[kv, 0:1, :]).astype(BF16)
        bot = (c + pe_ref[kv, 1:2, :]).astype(BF16)
        a_top = _dot(top, w1_ref[kv, 0:half, :])
        a_bot = _dot(bot, w1_ref[kv, half:2 * half, :])
        hid = jax.nn.gelu(a_top + pltpu.roll(a_bot, rows - 1, 0), approximate=True)
        out = out + _dot(hid.astype(BF16), w2_ref[kv])
    o_ref[...] = out


def _nsa_compress(c, pe, w1, w2pad):
    b, _, rows, width = c.shape
    return pl.pallas_call(
        _nsa_compress_kernel,
        grid=(b,),
        in_specs=[
            pl.BlockSpec((None, 2, rows, width), lambda bi: (bi, 0, 0, 0)),
            _const_spec(pe.shape),
            _const_spec(w1.shape),
            _const_spec(w2pad.shape),
        ],
        out_specs=pl.BlockSpec((None, rows, LANES), lambda bi: (bi, 0, 0)),
        out_shape=jax.ShapeDtypeStruct((b, rows, LANES), F32),
        compiler_params=_params("parallel"),
        name="nsa_compress",
    )(c, pe, w1, w2pad)


def _nsa_attn_kernel(q_ref, kk_ref, vv_ref, cmp_ref, gate_ref, kpos_ref, cmapt_ref, hconst_ref,
                     o_ref, kaug_ref, vts_ref, vtw_ref, qsel_ref, qwin_ref, *, tq, n_slc, n_sel):
    i = pl.program_id(2)
    t0 = i * tq
    tk = tq
    n_kt = kaug_ref.shape[0] // tk

    @pl.when(i == 0)
    def _():
        kaug_ref[:, 0:LANES] = kk_ref[...]
        kaug_ref[:, LANES:2 * LANES] = kpos_ref[...]
        top = lax.broadcasted_iota(jnp.int32, (LANES, tk), 0) < HEAD_DIM
        for j in range(n_kt):
            vt = vv_ref[j * tk:(j + 1) * tk, :].astype(F32).T
            vts_ref[j] = jnp.where(top, vt, 1.0).astype(BF16)
            vtw_ref[j] = jnp.where(top, 1.0, vt).astype(BF16)

    lane = lax.broadcasted_iota(jnp.int32, (tq, LANES), 1)
    low = lane < HEAD_DIM

    q4 = q_ref[...].astype(F32) * (HEAD_DIM ** -0.5)
    qa, qw = [], []
    for h in range(HPG):
        blk = q4[:, LANES * (h // 2):LANES * (h // 2 + 1)]
        rolled = pltpu.roll(blk, HEAD_DIM, 1)
        if h % 2 == 0:
            qa.append(jnp.where(low, blk, 0.0).astype(BF16))
            qw.append(jnp.where(low, 0.0, rolled).astype(BF16))
        else:
            qa.append(jnp.where(low, rolled, 0.0).astype(BF16))
            qw.append(jnp.where(low, 0.0, blk).astype(BF16))

    ck = cmp_ref[...]
    nc = ck.shape[0]
    sc_all = _dot_nt(ck.astype(BF16), jnp.concatenate(qa, axis=0))
    dist_c = ((t0 + lax.broadcasted_iota(jnp.int32, (nc, tq), 1))
              - (lax.broadcasted_iota(jnp.int32, (nc, tq), 0) * CMP_STRIDE + (CMP_BLOCK - 1)))
    valid_c = dist_c >= 0
    dist_cf = dist_c.astype(F32)
    psum = jnp.zeros((nc, tq), F32)
    p_heads = []
    for h in range(HPG):
        slope = hconst_ref[HPG + h:HPG + h + 1, 0:1]
        s_c = jnp.where(valid_c, sc_all[:, h * tq:(h + 1) * tq] - slope * dist_cf, NEG)
        m_c = jnp.max(s_c, axis=0, keepdims=True)
        e_c = jnp.where(valid_c, jnp.exp(s_c - m_c), 0.0)
        den = jnp.sum(e_c, axis=0, keepdims=True)
        p_c = e_c / jnp.maximum(den, 1e-30)
        psum = psum + p_c
        p_heads.append(p_c.astype(BF16))
    o_cmp_t = _dot(ck.T.astype(BF16), jnp.concatenate(p_heads, axis=1))[HEAD_DIM:]

    cmt = cmapt_ref[...]
    jp = cmt.shape[0]
    p_hi, p_lo = _split_bf16(psum)
    imp_t = _dot(cmt, p_hi) + _dot(cmt, p_lo)
    jb = lax.broadcasted_iota(jnp.int32, (jp, tq), 0)
    cur = jnp.right_shift(t0 + lax.broadcasted_iota(jnp.int32, (jp, tq), 1), SLC_BLOCK.bit_length() - 1)
    causal_j = jb <= cur
    forced = (jb == 0) | (jb == cur) | (jb == cur - 1)
    val = jnp.where(causal_j & forced, BIG, jnp.where(causal_j, imp_t, -BIG))
    rank = jnp.zeros((jp, tq), jnp.int32)
    for j in range(n_slc):
        row = val[j:j + 1, :]
        beats = (row > val) | ((row == val) & (jb > j))
        rank = rank + beats.astype(jnp.int32)
    sel = causal_j & (rank < n_sel)
    mb_t = jnp.where(sel, 0.0, NEG)
    if jp < LANES:
        mb_t = jnp.concatenate([mb_t, jnp.zeros((LANES - jp, tq), F32)], axis=0)
    mb = mb_t.T

    for h in range(HPG):
        pieces = hconst_ref[h:h + 1, :]
        r0 = h * tq
        qsel_ref[r0:r0 + tq, 0:LANES] = qa[h]
        qsel_ref[r0:r0 + tq, LANES:2 * LANES] = (mb + pieces).astype(BF16)
        qwin_ref[r0:r0 + tq, 0:LANES] = qw[h]
        qwin_ref[r0:r0 + tq, LANES:2 * LANES] = jnp.broadcast_to(pieces, (tq, LANES)).astype(BF16)

    def scores(qref, j0, nt):
        kt = kaug_ref[pl.ds(pl.multiple_of(j0 * tk, tk), nt * tk), :]
        return _dot_nt(kt, qref[...])

    def pv(vt_ref, j0, nt, p):
        acc = _dot(vt_ref[j0], p[0:tk])
        for u in range(1, nt):
            acc = acc + _dot(vt_ref[j0 + u], p[u * tk:(u + 1) * tk])
        return acc

    def heads(bias):
        return jnp.concatenate([bias] * HPG, axis=1)

    def dist_to(j0, nt):
        key = j0 * tk + lax.broadcasted_iota(jnp.int32, (nt * tk, tq), 0)
        return (t0 + lax.broadcasted_iota(jnp.int32, (nt * tk, tq), 1)) - key

    nwt = min(WINDOW // tk + 1, n_kt)
    jw = jnp.clip(i - (nwt - 1), 0, n_kt - nwt)
    dist_w = dist_to(jw, nwt)
    s_w = scores(qwin_ref, jw, nwt) + heads(jnp.where((dist_w >= 0) & (dist_w < WINDOW), 0.0, NEG))
    p_w = jnp.exp(s_w - jnp.max(s_w, axis=0, keepdims=True)).astype(BF16)
    acc_w = pv(vtw_ref, jw, nwt, p_w)
    o_win_t = acc_w[HEAD_DIM:] * (1.0 / acc_w[0:1])

    pair = i // 2
    s_d = scores(qsel_ref, 2 * pair, 2) + heads(jnp.where(dist_to(2 * pair, 2) >= 0, 0.0, NEG))
    m_d = jnp.max(s_d, axis=0, keepdims=True)
    acc_d = pv(vts_ref, 2 * pair, 2, jnp.exp(s_d - m_d).astype(BF16))

    def sel_body(jj, carry):
        m, acc = carry
        j0 = 2 * (pair - 1 - jj)
        s = scores(qsel_ref, j0, 2)
        m_new = jnp.maximum(m, jnp.max(s, axis=0, keepdims=True))
        p = jnp.exp(s - m_new).astype(BF16)
        return m_new, jnp.exp(m - m_new) * acc + pv(vts_ref, j0, 2, p)

    _, acc_s = lax.fori_loop(0, pair, sel_body, (m_d, acc_d))
    o_sel_t = acc_s[:HEAD_DIM] * (1.0 / acc_s[HEAD_DIM:HEAD_DIM + 1])

    gt_t = gate_ref[...].T
    rows = []
    for h in range(HPG):
        sl = slice(h * tq, (h + 1) * tq)
        rows.append(gt_t[h:h + 1] * o_cmp_t[:, sl] + gt_t[HPG + h:HPG + h + 1] * o_sel_t[:, sl]
                    + gt_t[2 * HPG + h:2 * HPG + h + 1] * o_win_t[:, sl])
    o_ref[...] = jnp.concatenate(rows, axis=0).T.astype(o_ref.dtype)


def _nsa_attn(main, cmp, gates, kpos, cmapt, hconst, tq):
    b, s, _ = main.shape
    assert (s // tq) % 2 == 0
    nc = cmp.shape[1] // N_GROUPS
    n_slc = s // SLC_BLOCK
    gw = HPG * HEAD_DIM
    per_g = (gw + 2 * LANES) // LANES
    kern = functools.partial(_nsa_attn_kernel, tq=tq, n_slc=n_slc, n_sel=min(N_SLC_BLOCKS, n_slc))
    return pl.pallas_call(
        kern,
        grid=(b, N_GROUPS, s // tq),
        in_specs=[
            pl.BlockSpec((None, tq, gw), lambda bi, g, i: (bi, i, g * per_g // 2)),
            pl.BlockSpec((None, s, LANES), lambda bi, g, i: (bi, 0, g * per_g + 2)),
            pl.BlockSpec((None, s, LANES), lambda bi, g, i: (bi, 0, g * per_g + 3)),
            pl.BlockSpec((None, nc, LANES), lambda bi, g, i: (bi, g, 0)),
            pl.BlockSpec((None, tq, LANES), lambda bi, g, i: (bi, i, g)),
            _const_spec(kpos.shape),
            _const_spec(cmapt.shape),
            pl.BlockSpec((None, 2 * HPG, LANES), lambda bi, g, i: (g, 0, 0)),
        ],
        out_specs=pl.BlockSpec((None, tq, gw), lambda bi, g, i: (bi, i, g)),
        out_shape=jax.ShapeDtypeStruct((b, s, N_GROUPS * gw), BF16),
        scratch_shapes=[
            pltpu.VMEM((s, 2 * LANES), BF16),
            pltpu.VMEM((s // tq, LANES, tq), BF16),
            pltpu.VMEM((s // tq, LANES, tq), BF16),
            pltpu.VMEM((HPG * tq, 2 * LANES), BF16),
            pltpu.VMEM((HPG * tq, 2 * LANES), BF16),
        ],
        compiler_params=_params("parallel", "parallel", "arbitrary"),
        name="nsa_attn",
    )(main, main, main, cmp, gates, kpos, cmapt, hconst)


def _sb_attn_kernel(q_ref, k_ref, v_ref, tri_ref, o_ref, vt_ref, *, tq):
    i = pl.program_id(2)
    tk = tq
    n_kt = k_ref.shape[0] // tk
    n_chain = q_ref.shape[1] // LANES

    @pl.when(i == 0)
    def _():
        for c in range(n_chain):
            for j in range(n_kt):
                vt_ref[c, j] = v_ref[j * tk:(j + 1) * tk, c * LANES:(c + 1) * LANES].astype(F32).T.astype(BF16)

    low = lax.broadcasted_iota(jnp.int32, (tq, LANES), 1) < HEAD_DIM
    before = (lax.broadcasted_iota(jnp.int32, (tk, 2 * tq), 0)
              < (lax.broadcasted_iota(jnp.int32, (tk, 2 * tq), 1) & (tq - 1)))
    tri = tri_ref[...]

    qst = []
    for c in range(n_chain):
        q2 = q_ref[:, c * LANES:(c + 1) * LANES].astype(F32) * (HEAD_DIM ** -0.5)
        qst.append(jnp.concatenate([jnp.where(low, q2, 0.0), jnp.where(low, 0.0, q2)], axis=0).astype(BF16))

    chains = range(n_chain)

    def tiles(j, diag, carry):
        off = pl.multiple_of(j * tk, tk)
        z = [_dot_nt(k_ref[pl.ds(off, tk), c * LANES:(c + 1) * LANES], qst[c]) for c in chains]
        sp = [jnp.maximum(z[c], 0.0) + jnp.log(1.0 + jnp.exp(-jnp.abs(z[c]))) for c in chains]
        if diag:
            sp = [jnp.where(before, sp[c], 0.0) for c in chains]
        parts = [_split_bf16_trunc(sp[c]) for c in chains]
        suffix = [_dot(tri, parts[c][0]) + _dot(tri, parts[c][1]) for c in chains]
        if diag:
            a = [jnp.where(before, jnp.exp(z[c] + suffix[c]), 0.0) for c in chains]
        else:
            a = [jnp.exp(z[c] + suffix[c] + carry[c][1]) for c in chains]
        pv = [_dot(vt_ref[c, j], a[c].astype(BF16)) for c in chains]
        tot = [jnp.sum(sp[c], axis=0, keepdims=True) for c in chains]
        if diag:
            return tuple((pv[c], -tot[c]) for c in chains)
        return tuple((carry[c][0] + pv[c], carry[c][1] - tot[c]) for c in chains)

    res = lax.fori_loop(0, i, lambda jj, carry: tiles(i - 1 - jj, False, carry), tiles(i, True, None))
    for c in range(n_chain):
        acc = res[c][0]
        out_t = jnp.concatenate([acc[:HEAD_DIM, :tq], acc[HEAD_DIM:, tq:]], axis=0)
        o_ref[:, c * LANES:(c + 1) * LANES] = out_t.T.astype(o_ref.dtype)


def _sb_attn(qkv, tri, tq):
    b, s, n3 = qkv.shape
    w = LANES * SB_CHAINS
    nblk = n3 // 3 // w
    return pl.pallas_call(
        functools.partial(_sb_attn_kernel, tq=tq),
        grid=(b, nblk, s // tq),
        in_specs=[
            pl.BlockSpec((None, tq, w), lambda bi, p, i: (bi, i, p)),
            pl.BlockSpec((None, s, w), lambda bi, p, i: (bi, 0, nblk + p)),
            pl.BlockSpec((None, s, w), lambda bi, p, i: (bi, 0, 2 * nblk + p)),
            _const_spec(tri.shape),
        ],
        out_specs=pl.BlockSpec((None, tq, w), lambda bi, p, i: (bi, i, p)),
        out_shape=jax.ShapeDtypeStruct((b, s, nblk * w), BF16),
        scratch_shapes=[pltpu.VMEM((SB_CHAINS, s // tq, LANES, tq), BF16)],
        compiler_params=_params("parallel", "parallel", "arbitrary"),
        name="sb_attn",
    )(qkv, qkv, qkv, tri)


def _mix_ffn_kernel(h_ref, o_ref, wo_ref, g_ref, w1_ref, w3_ref, w2_ref, fg_ref, out_ref, a_ref, acc_ref,
                    *, final):
    h1 = h_ref[...] + _dot(o_ref[...], wo_ref[...])
    a_ref[...] = _rmsnorm(h1, g_ref[...]).astype(BF16)
    acc_ref[...] = h1

    def body(c, carry):
        a = a_ref[...]
        u = _dot(a, w1_ref[c])
        v = _dot(a, w3_ref[c])
        t = (u * jax.nn.sigmoid(u)) * v
        acc_ref[...] += _dot(t.astype(BF16), w2_ref[c])
        return carry

    lax.fori_loop(0, w1_ref.shape[0], body, 0)
    h2 = acc_ref[...]
    out_ref[...] = _rmsnorm(h2, fg_ref[...]) if final else h2


def _mix_ffn(h, o, wo, g, w1, w3, w2, fg, tm, final):
    r, d = h.shape
    single = pl.Buffered(1)

    def wspec(shape):
        nd = len(shape)
        return pl.BlockSpec(shape, lambda *_: (0,) * nd, pipeline_mode=single)

    return pl.pallas_call(
        functools.partial(_mix_ffn_kernel, final=final),
        grid=(r // tm,),
        in_specs=[
            pl.BlockSpec((tm, d), lambda i: (i, 0)),
            pl.BlockSpec((tm, d), lambda i: (i, 0)),
            wspec(wo.shape),
            _const_spec((1, d)),
            wspec(w1.shape),
            wspec(w3.shape),
            wspec(w2.shape),
            _const_spec((1, d)),
        ],
        out_specs=pl.BlockSpec((tm, d), lambda i: (i, 0)),
        out_shape=jax.ShapeDtypeStruct((r, d), F32),
        scratch_shapes=[pltpu.VMEM((tm, d), BF16), pltpu.VMEM((tm, d), F32)],
        compiler_params=_params("parallel"),
        name="mix_ffn",
    )(h, o, wo, g, w1, w3, w2, fg)


def _nsa_layout(d):
    kvw = N_GROUPS * HEAD_DIM
    off = {"q": 0, "kc": d, "vc": d + kvw, "ksl": d + 2 * kvw, "vsl": d + 3 * kvw, "kw": d + 4 * kvw,
           "vw": d + 5 * kvw, "gt": d + 6 * kvw}
    main = []
    for g in range(N_GROUPS):
        main.append(np.arange(HPG * HEAD_DIM) + g * HPG * HEAD_DIM)
        for name in ("ksl", "kw", "vsl", "vw"):
            main.append(off[name] + g * HEAD_DIM + np.arange(HEAD_DIM))
    main = np.concatenate(main)
    cv = off["kc"] + np.arange(2 * kvw)
    gate = np.full((N_GROUPS * LANES,), -1, np.int64)
    for g in range(N_GROUPS):
        for br in range(N_BRANCH):
            for hh in range(HPG):
                gate[g * LANES + br * HPG + hh] = br * N_HEADS + g * HPG + hh
    return main, cv, gate, off["gt"]


def _bf16_pieces(x):
    x = np.asarray(x, np.float32)
    out = []
    for _ in range(3):
        p = x.astype(BF16).astype(np.float32)
        out.append(p)
        x = (x - p).astype(np.float32)
    return out


def _nsa_tables(s, tk):
    n_slc = s // SLC_BLOCK
    n_cmp = s // CMP_STRIDE - CMP_BLOCK // CMP_STRIDE + 1
    nc = s // CMP_STRIDE
    pos = np.arange(s)
    kpos = np.zeros((s, LANES), np.float32)
    kpos[pos, pos // SLC_BLOCK] = 1.0
    for c in range(3):
        kpos[:, POS_LANE + c] = (pos // tk) * tk
        kpos[:, POS_LANE + 3 + c] = pos % tk
    cs = np.arange(n_cmp) * CMP_STRIDE
    ss = np.arange(n_slc) * SLC_BLOCK
    ov = np.clip(np.minimum(cs[:, None] + CMP_BLOCK, ss[None] + SLC_BLOCK) - np.maximum(cs[:, None], ss[None]), 0, None)
    jp = -(-n_slc // 16) * 16
    cmapt = np.zeros((jp, nc), np.float32)
    cmapt[:n_slc, :n_cmp] = (ov / CMP_STRIDE).T
    slopes = np.asarray(2.0 ** (-8.0 * (np.arange(N_HEADS) + 1) / N_HEADS), np.float32)
    hconst = np.zeros((N_GROUPS, 2 * HPG, LANES), np.float32)
    pieces = _bf16_pieces(slopes)
    for g in range(N_GROUPS):
        for hh in range(HPG):
            for c in range(3):
                hconst[g, hh, POS_LANE + c] = pieces[c][g * HPG + hh]
                hconst[g, hh, POS_LANE + 3 + c] = pieces[c][g * HPG + hh]
            hconst[g, HPG + hh, :] = slopes[g * HPG + hh]
    return jnp.asarray(kpos, BF16), jnp.asarray(cmapt, BF16), jnp.asarray(hconst, F32)


def _ffn_chunk(d_ff):
    for tf in (512, 256, 128):
        if d_ff % tf == 0:
            return tf
    return d_ff


def kernel(x, mix_norm_g, ffn_norm_g, final_norm_g, nsa_w_in, nsa_gate_b, nsa_cmp_pe, nsa_cmp_w1, nsa_cmp_w2,
           nsa_w_out, sb_w_in, sb_w_out, ffn_w1, ffn_w3, ffn_w2):
    b, s, d = x.shape
    depth = mix_norm_g.shape[0]
    d_ff = ffn_w1.shape[2]
    tm = min(512, s)
    tq = min(256, s)
    tf = _ffn_chunk(d_ff)
    nch = d_ff // tf

    main_idx, cv_idx, gate_idx, gt_off = _nsa_layout(d)
    gate_valid = jnp.asarray(gate_idx >= 0)
    gate_cols = jnp.asarray(np.maximum(gate_idx, 0))
    kpos, cmapt, hconst = _nsa_tables(s, tq)
    tri = jnp.asarray(-np.triu(np.ones((tq, tq), np.float32)), BF16)
    chunk = CMP_STRIDE * HEAD_DIM
    nc = s // CMP_STRIDE

    h = x
    for i in range(depth):
        j = i // 2
        gm = mix_norm_g[i].reshape(1, d)
        if i % 2 == 0:
            w_in = nsa_w_in[j]
            w_main = w_in[:, main_idx].astype(BF16)
            w_cv = w_in[:, cv_idx].astype(BF16)
            w_gate = jnp.where(gate_valid[None, :], w_in[:, gt_off + gate_cols], 0.0).astype(BF16)
            gate_b = jnp.where(gate_valid, nsa_gate_b[j][gate_cols], 0.0).reshape(1, -1)
            main, cv, gates = _nsa_proj(h, gm, w_main, w_cv, w_gate, gate_b, tm)
            c = cv.reshape(b, 2, N_GROUPS * nc, chunk)
            pe = nsa_cmp_pe[j].reshape(2, 2, chunk)
            w2pad = jnp.zeros((2, nsa_cmp_w2.shape[2], LANES), F32)
            w2pad = w2pad.at[0, :, :HEAD_DIM].set(nsa_cmp_w2[j, 0]).at[1, :, HEAD_DIM:].set(nsa_cmp_w2[j, 1])
            cmp = _nsa_compress(c, pe, nsa_cmp_w1[j].astype(BF16), w2pad.astype(BF16))
            o = _nsa_attn(main, cmp, gates, kpos, cmapt, hconst, tq)
            w_out = nsa_w_out[j]
        else:
            qkv = _sb_proj(h, gm, sb_w_in[j].astype(BF16), tm)
            o = _sb_attn(qkv, tri, tq)
            w_out = sb_w_out[j]
        w1 = ffn_w1[i].reshape(d, nch, tf).transpose(1, 0, 2).astype(BF16)
        w3 = ffn_w3[i].reshape(d, nch, tf).transpose(1, 0, 2).astype(BF16)
        w2 = ffn_w2[i].reshape(nch, tf, d).astype(BF16)
        final = i == depth - 1
        h = _mix_ffn(h.reshape(b * s, d), o.reshape(b * s, d), w_out.astype(BF16), ffn_norm_g[i].reshape(1, d),
                     w1, w3, w2, final_norm_g.reshape(1, d), tm, final).reshape(b, s, d)
    return h
```

```python
import functools

import numpy as np
import jax
import jax.numpy as jnp
from jax import lax
from jax.experimental import pallas as pl
from jax.experimental.pallas import tpu as pltpu

F32 = jnp.float32
BF16 = jnp.bfloat16

RMS_EPS = 1e-6
NEG = -1e30
BIG = 1e30

N_HEADS = 16
HEAD_DIM = 64
N_GROUPS = 4
HPG = N_HEADS // N_GROUPS
N_BRANCH = 3
CMP_BLOCK = 32
CMP_STRIDE = 16
SLC_BLOCK = 64
N_SLC_BLOCKS = 8
WINDOW = 512

LANES = 128
POS_LANE = 32
LOG2E = 1.4426950408889634
VMEM_LIMIT = 56 * 1024 * 1024
SB_CHAINS = 4

_NT = (((1,), (1,)), ((), ()))


def _dot(a, b):
    return jnp.dot(a, b, preferred_element_type=F32)


def _dot_nt(a, b):
    return lax.dot_general(a, b, _NT, preferred_element_type=F32)


def _rmsnorm(x, g):
    ms = jnp.mean(x * x, axis=-1, keepdims=True)
    return (x * lax.rsqrt(ms + RMS_EPS)) * g


def _split_bf16(x):
    hi = x.astype(BF16)
    lo = (x - hi.astype(F32)).astype(BF16)
    return hi, lo


def _split_bf16_trunc(x):
    hi = lax.bitcast_convert_type(lax.bitcast_convert_type(x, jnp.uint32) & jnp.uint32(0xFFFF0000), F32)
    return hi.astype(BF16), (x - hi).astype(BF16)


def _neg_abs(x):
    return lax.bitcast_convert_type(lax.bitcast_convert_type(x, jnp.uint32) | jnp.uint32(0x80000000), F32)


def _params(*sem):
    return pltpu.CompilerParams(dimension_semantics=sem, vmem_limit_bytes=VMEM_LIMIT)


def _const_spec(shape):
    nd = len(shape)
    return pl.BlockSpec(shape, lambda *_: (0,) * nd)


def _nsa_proj_kernel(x_ref, g_ref, wm_ref, wc_ref, wg_ref, gb_ref, main_ref, cv_ref, gate_ref):
    a = _rmsnorm(x_ref[...], g_ref[...]).astype(BF16)
    main_ref[...] = _dot(a, wm_ref[...]).astype(main_ref.dtype)
    cv = _dot(a, wc_ref[...])
    for kv in range(2):
        for g in range(N_GROUPS):
            c0 = (kv * N_GROUPS + g) * HEAD_DIM
            cv_ref[kv, g, :, :] = cv[:, c0:c0 + HEAD_DIM]
    gate_ref[...] = jax.nn.sigmoid(_dot(a, wg_ref[...]) + gb_ref[...])


def _nsa_proj(h, g, w_main, w_cv, w_gate, gate_b, tm):
    b, s, d = h.shape
    nm, ng = w_main.shape[1], w_gate.shape[1]
    return pl.pallas_call(
        _nsa_proj_kernel,
        grid=(b, s // tm),
        in_specs=[
            pl.BlockSpec((None, tm, d), lambda bi, r: (bi, r, 0)),
            _const_spec((1, d)),
            _const_spec(w_main.shape),
            _const_spec(w_cv.shape),
            _const_spec(w_gate.shape),
            _const_spec((1, ng)),
        ],
        out_specs=[
            pl.BlockSpec((None, tm, nm), lambda bi, r: (bi, r, 0)),
            pl.BlockSpec((None, 2, N_GROUPS, tm, HEAD_DIM), lambda bi, r: (bi, 0, 0, r, 0)),
            pl.BlockSpec((None, tm, ng), lambda bi, r: (bi, r, 0)),
        ],
        out_shape=[
            jax.ShapeDtypeStruct((b, s, nm), BF16),
            jax.ShapeDtypeStruct((b, 2, N_GROUPS, s, HEAD_DIM), F32),
            jax.ShapeDtypeStruct((b, s, ng), F32),
        ],
        compiler_params=_params("parallel", "parallel"),
        name="nsa_proj",
    )(h, g, w_main, w_cv, w_gate, gate_b)


def _sb_proj_kernel(x_ref, g_ref, w_ref, o_ref):
    a = _rmsnorm(x_ref[...], g_ref[...]).astype(BF16)
    o_ref[...] = _dot(a, w_ref[...]).astype(o_ref.dtype)


def _sb_proj(h, g, w, tm):
    b, s, d = h.shape
    n = w.shape[1]
    return pl.pallas_call(
        _sb_proj_kernel,
        grid=(b, s // tm),
        in_specs=[
            pl.BlockSpec((None, tm, d), lambda bi, r: (bi, r, 0)),
            _const_spec((1, d)),
            _const_spec(w.shape),
        ],
        out_specs=pl.BlockSpec((None, tm, n), lambda bi, r: (bi, r, 0)),
        out_shape=jax.ShapeDtypeStruct((b, s, n), BF16),
        compiler_params=_params("parallel", "parallel"),
        name="sb_proj",
    )(h, g, w)


def _nsa_compress_kernel(c_ref, pe_ref, w1_ref, w2_ref, o_ref):
    rows = c_ref.shape[1]
    half = c_ref.shape[2]
    out = jnp.zeros(o_ref.shape, F32)
    for kv in range(2):
        c = c_ref[kv]
        top = (c + pe_ref[kv, 0:1, :]).astype(BF16)
        bot = (c + pe_ref[kv, 1:2, :]).astype(BF16)
        a_top = _dot(top, w1_ref[kv, 0:half, :])
        a_bot = _dot(bot, w1_ref[kv, half:2 * half, :])
        hid = jax.nn.gelu(a_top + pltpu.roll(a_bot, rows - 1, 0), approximate=True)
        out = out + _dot(hid.astype(BF16), w2_ref[kv])
    o_ref[...] = out


def _nsa_compress(c, pe, w1, w2pad):
    b, _, rows, width = c.shape
    return pl.pallas_call(
        _nsa_compress_kernel,
        grid=(b,),
        in_specs=[
            pl.BlockSpec((None, 2, rows, width), lambda bi: (bi, 0, 0, 0)),
            _const_spec(pe.shape),
            _const_spec(w1.shape),
            _const_spec(w2pad.shape),
        ],
        out_specs=pl.BlockSpec((None, rows, LANES), lambda bi: (bi, 0, 0)),
        out_shape=jax.ShapeDtypeStruct((b, rows, LANES), F32),
        compiler_params=_params("parallel"),
        name="nsa_compress",
    )(c, pe, w1, w2pad)


def _nsa_attn_kernel(q_ref, kk_ref, vv_ref, cmp_ref, gate_ref, kpos_ref, cmapt_ref, hconst_ref, wb_ref, sb_ref,
                     o_ref, kaug_s_ref, kaug_w_ref, vts_ref, vtw_ref, qsel_ref, qwin_ref, *, tq, n_slc, n_sel):
    i = pl.program_id(2)
    t0 = i * tq
    tk = tq
    n_kt = kaug_s_ref.shape[0] // tk

    @pl.when(i == 0)
    def _():
        kk = kk_ref[...].astype(F32)
        kk_sw = pltpu.roll(kk, HEAD_DIM, 1)
        first = lax.broadcasted_iota(jnp.int32, kk.shape, 1) < HEAD_DIM
        kaug_s_ref[:, 0:LANES] = jnp.where(first, kk, kk_sw).astype(BF16)
        kaug_w_ref[:, 0:LANES] = jnp.where(first, kk_sw, kk).astype(BF16)
        kaug_s_ref[:, LANES:2 * LANES] = kpos_ref[...]
        kaug_w_ref[:, LANES:2 * LANES] = kpos_ref[...]
        top = lax.broadcasted_iota(jnp.int32, (LANES, tk), 0) < HEAD_DIM
        for j in range(n_kt):
            vt = vv_ref[j * tk:(j + 1) * tk, :].astype(F32).T
            vts_ref[j] = jnp.where(top, vt, 1.0).astype(BF16)
            vtw_ref[j] = jnp.where(top, 1.0, vt).astype(BF16)

    lane = lax.broadcasted_iota(jnp.int32, (tq, LANES), 1)
    low = lane < HEAD_DIM

    q4 = q_ref[...].astype(F32) * (HEAD_DIM ** -0.5 * LOG2E)
    q4_hi = q4.astype(BF16).astype(F32)
    q4_lo = q4 - q4_hi
    qa = []
    for h in range(HPG):
        cols = slice(LANES * (h // 2), LANES * (h // 2 + 1))
        if h % 2 == 0:
            qa.append(jnp.where(low, q4_hi[:, cols], pltpu.roll(q4_lo[:, cols], HEAD_DIM, 1)).astype(BF16))
        else:
            qa.append(jnp.where(low, pltpu.roll(q4_hi[:, cols], HEAD_DIM, 1), q4_lo[:, cols]).astype(BF16))

    def scores(kref, qref, j0, nt):
        kt = kref[pl.ds(pl.multiple_of(j0 * tk, tk), nt * tk), :]
        return _dot_nt(kt, qref[...])

    def pv(vt_ref, j0, nt, p):
        acc = _dot(vt_ref[j0], p[0:tk])
        for u in range(1, nt):
            acc = acc + _dot(vt_ref[j0 + u], p[u * tk:(u + 1) * tk])
        return acc

    def heads(bias):
        return jnp.concatenate([bias] * HPG, axis=1)

    for h in range(HPG):
        qwin_ref[h * tq:(h + 1) * tq, 0:LANES] = qa[h]
        qwin_ref[h * tq:(h + 1) * tq, LANES:2 * LANES] = jnp.broadcast_to(hconst_ref[h:h + 1, :],
                                                                          (tq, LANES)).astype(BF16)
    nwt = min(WINDOW // tk + 1, n_kt)
    jw = jnp.clip(i - (nwt - 1), 0, n_kt - nwt)
    s_w = scores(kaug_w_ref, qwin_ref, jw, nwt)
    wbias = wb_ref[i - jw]

    def window_head(h):
        s_h = s_w[:, h * tq:(h + 1) * tq] + wbias
        p_h = jnp.exp2(s_h - jnp.max(s_h, axis=0, keepdims=True)).astype(BF16)
        acc = pv(vtw_ref, jw, nwt, p_h)
        return acc[HEAD_DIM:] * (1.0 / acc[0:1])

    ck = cmp_ref[...]
    nc = ck.shape[0]
    ckk = jnp.where(lax.broadcasted_iota(jnp.int32, ck.shape, 1) < HEAD_DIM, ck, pltpu.roll(ck, HEAD_DIM, 1))
    sc_all = _dot_nt(ckk.astype(BF16), jnp.concatenate(qa, axis=0))
    dist_c = ((t0 + lax.broadcasted_iota(jnp.int32, (nc, tq), 1))
              - (lax.broadcasted_iota(jnp.int32, (nc, tq), 0) * CMP_STRIDE + (CMP_BLOCK - 1)))
    valid_c = dist_c >= 0
    dist_cf = dist_c.astype(F32)
    psum = jnp.zeros((nc, tq), F32)
    p_heads = []
    for h in range(HPG):
        slope = hconst_ref[HPG + h:HPG + h + 1, 0:1]
        s_c = jnp.where(valid_c, sc_all[:, h * tq:(h + 1) * tq] - slope * dist_cf, NEG)
        m_c = jnp.max(s_c, axis=0, keepdims=True)
        e_c = jnp.where(valid_c, jnp.exp2(s_c - m_c), 0.0)
        den = jnp.sum(e_c, axis=0, keepdims=True)
        p_c = e_c / jnp.maximum(den, 1e-30)
        psum = psum + p_c
        p_heads.append(p_c.astype(BF16))
    o_cmp_t = _dot(ck.T.astype(BF16), jnp.concatenate(p_heads, axis=1))[HEAD_DIM:]

    cmt = cmapt_ref[...]
    jp = cmt.shape[0]
    p_hi, p_lo = _split_bf16(psum)
    imp_t = _dot(cmt, p_hi) + _dot(cmt, p_lo)
    jb = lax.broadcasted_iota(jnp.int32, (jp, tq), 0)
    cur = jnp.right_shift(t0 + lax.broadcasted_iota(jnp.int32, (jp, tq), 1), SLC_BLOCK.bit_length() - 1)
    causal_j = jb <= cur
    forced = (jb == 0) | (jb == cur) | (jb == cur - 1)
    val = jnp.where(causal_j & forced, BIG, jnp.where(causal_j, imp_t, -BIG))
    rank = jnp.zeros((jp, tq), jnp.int32)
    o_win_heads = []
    for j in range(n_slc):
        if j % (n_slc // HPG) == 0:
            o_win_heads.append(window_head(j // (n_slc // HPG)))
        row = val[j:j + 1, :]
        beats = (row > val) | ((row == val) & (jb > j))
        rank = rank + beats.astype(jnp.int32)
    o_win_t = jnp.concatenate(o_win_heads, axis=1)
    sel = causal_j & (rank < n_sel)
    mb_t = jnp.where(sel, 0.0, NEG)
    if jp < LANES:
        mb_t = jnp.concatenate([mb_t, jnp.zeros((LANES - jp, tq), F32)], axis=0)
    mb = mb_t.T

    for h in range(HPG):
        qsel_ref[h * tq:(h + 1) * tq, 0:LANES] = qa[h]
        qsel_ref[h * tq:(h + 1) * tq, LANES:2 * LANES] = (mb + hconst_ref[h:h + 1, :]).astype(BF16)
    pair = i // 2
    s_d = scores(kaug_s_ref, qsel_ref, 2 * pair, 2) + heads(sb_ref[i - 2 * pair])
    m_d = jnp.max(s_d, axis=0, keepdims=True)
    acc_d = pv(vts_ref, 2 * pair, 2, jnp.exp2(s_d - m_d).astype(BF16))

    def sel_body(jj, carry):
        m, acc = carry
        j0 = 2 * (pair - 1 - jj)
        s = scores(kaug_s_ref, qsel_ref, j0, 2)
        m_new = jnp.maximum(m, jnp.max(s, axis=0, keepdims=True))
        p = jnp.exp2(s - m_new).astype(BF16)
        return m_new, jnp.exp2(m - m_new) * acc + pv(vts_ref, j0, 2, p)

    _, acc_s = lax.fori_loop(0, pair, sel_body, (m_d, acc_d))
    o_sel_t = acc_s[:HEAD_DIM] * (1.0 / acc_s[HEAD_DIM:HEAD_DIM + 1])

    gt_t = gate_ref[...].T
    rows = []
    for h in range(HPG):
        sl = slice(h * tq, (h + 1) * tq)
        rows.append(gt_t[h:h + 1] * o_cmp_t[:, sl] + gt_t[HPG + h:HPG + h + 1] * o_sel_t[:, sl]
                    + gt_t[2 * HPG + h:2 * HPG + h + 1] * o_win_t[:, sl])
    o_ref[...] = jnp.concatenate(rows, axis=0).T.astype(o_ref.dtype)


def _nsa_attn(main, cmp, gates, kpos, cmapt, hconst, wbias, sbias, tq):
    b, s, _ = main.shape
    assert (s // tq) % 2 == 0
    nc = cmp.shape[1] // N_GROUPS
    n_slc = s // SLC_BLOCK
    gw = HPG * HEAD_DIM
    per_g = (gw + 2 * LANES) // LANES
    kern = functools.partial(_nsa_attn_kernel, tq=tq, n_slc=n_slc, n_sel=min(N_SLC_BLOCKS, n_slc))
    return pl.pallas_call(
        kern,
        grid=(b, N_GROUPS, s // tq),
        in_specs=[
            pl.BlockSpec((None, tq, gw), lambda bi, g, i: (bi, i, g * per_g // 2)),
            pl.BlockSpec((None, s, LANES), lambda bi, g, i: (bi, 0, g * per_g + 2)),
            pl.BlockSpec((None, s, LANES), lambda bi, g, i: (bi, 0, g * per_g + 3)),
            pl.BlockSpec((None, nc, LANES), lambda bi, g, i: (bi, g, 0)),
            pl.BlockSpec((None, tq, LANES), lambda bi, g, i: (bi, i, g)),
            _const_spec(kpos.shape),
            _const_spec(cmapt.shape),
            pl.BlockSpec((None, 2 * HPG, LANES), lambda bi, g, i: (g, 0, 0)),
            _const_spec(wbias.shape),
            _const_spec(sbias.shape),
        ],
        out_specs=pl.BlockSpec((None, tq, gw), lambda bi, g, i: (bi, i, g)),
        out_shape=jax.ShapeDtypeStruct((b, s, N_GROUPS * gw), BF16),
        scratch_shapes=[
            pltpu.VMEM((s, 2 * LANES), BF16),
            pltpu.VMEM((s, 2 * LANES), BF16),
            pltpu.VMEM((s // tq, LANES, tq), BF16),
            pltpu.VMEM((s // tq, LANES, tq), BF16),
            pltpu.VMEM((HPG * tq, 2 * LANES), BF16),
            pltpu.VMEM((HPG * tq, 2 * LANES), BF16),
        ],
        compiler_params=_params("parallel", "parallel", "arbitrary"),
        name="nsa_attn",
    )(main, main, main, cmp, gates, kpos, cmapt, hconst, wbias, sbias)


def _sb_attn_kernel(q_ref, k_ref, v_ref, tri_ref, o_ref, vt_ref, *, tq):
    i = pl.program_id(2)
    tk = tq
    n_kt = k_ref.shape[0] // tk
    n_chain = q_ref.shape[1] // LANES

    @pl.when(i == 0)
    def _():
        for c in range(n_chain):
            for j in range(n_kt):
                vt_ref[c, j] = v_ref[j * tk:(j + 1) * tk, c * LANES:(c + 1) * LANES].astype(F32).T.astype(BF16)

    low = lax.broadcasted_iota(jnp.int32, (tq, LANES), 1) < HEAD_DIM
    before = (lax.broadcasted_iota(jnp.int32, (tk, 2 * tq), 0)
              < (lax.broadcasted_iota(jnp.int32, (tk, 2 * tq), 1) & (tq - 1)))
    tri = tri_ref[...]

    qst = []
    for c in range(n_chain):
        q2 = q_ref[:, c * LANES:(c + 1) * LANES].astype(F32) * (HEAD_DIM ** -0.5)
        qst.append(jnp.concatenate([jnp.where(low, q2, 0.0), jnp.where(low, 0.0, q2)], axis=0).astype(BF16))

    chains = range(n_chain)

    def tiles(j, diag, carry):
        off = pl.multiple_of(j * tk, tk)
        z = [_dot_nt(k_ref[pl.ds(off, tk), c * LANES:(c + 1) * LANES], qst[c]) for c in chains]
        sp = [jnp.maximum(z[c], 0.0) + jnp.log(1.0 + jnp.exp(_neg_abs(z[c]))) for c in chains]
        if diag:
            sp = [jnp.where(before, sp[c], 0.0) for c in chains]
        sums = [_dot(tri, jnp.concatenate(_split_bf16_trunc(sp[c]), axis=0)) for c in chains]
        if diag:
            a = [jnp.where(before, jnp.exp(z[c] + sums[c][:tk]), 0.0) for c in chains]
        else:
            a = [jnp.exp(z[c] + sums[c][:tk] + carry[c][1]) for c in chains]
        pv = [_dot(vt_ref[c, j], a[c].astype(BF16)) for c in chains]
        if diag:
            return tuple((pv[c], sums[c][tk:tk + 1]) for c in chains)
        return tuple((carry[c][0] + pv[c], carry[c][1] + sums[c][tk:tk + 1]) for c in chains)

    res = lax.fori_loop(0, i, lambda jj, carry: tiles(i - 1 - jj, False, carry), tiles(i, True, None))
    for c in range(n_chain):
        acc = res[c][0]
        out_t = jnp.concatenate([acc[:HEAD_DIM, :tq], acc[HEAD_DIM:, tq:]], axis=0)
        o_ref[:, c * LANES:(c + 1) * LANES] = out_t.T.astype(o_ref.dtype)


def _sb_attn(qkv, tri, tq):
    b, s, n3 = qkv.shape
    w = LANES * SB_CHAINS
    nblk = n3 // 3 // w
    return pl.pallas_call(
        functools.partial(_sb_attn_kernel, tq=tq),
        grid=(b, nblk, s // tq),
        in_specs=[
            pl.BlockSpec((None, tq, w), lambda bi, p, i: (bi, i, p)),
            pl.BlockSpec((None, s, w), lambda bi, p, i: (bi, 0, nblk + p)),
            pl.BlockSpec((None, s, w), lambda bi, p, i: (bi, 0, 2 * nblk + p)),
            _const_spec(tri.shape),
        ],
        out_specs=pl.BlockSpec((None, tq, w), lambda bi, p, i: (bi, i, p)),
        out_shape=jax.ShapeDtypeStruct((b, s, nblk * w), BF16),
        scratch_shapes=[pltpu.VMEM((SB_CHAINS, s // tq, LANES, tq), BF16)],
        compiler_params=_params("parallel", "parallel", "arbitrary"),
        name="sb_attn",
    )(qkv, qkv, qkv, tri)


def _mix_ffn_kernel(h_ref, o_ref, wo_ref, g_ref, w1_ref, w3_ref, w2_ref, fg_ref, out_ref, a_ref, acc_ref,
                    *, final):
    h1 = h_ref[...] + _dot(o_ref[...], wo_ref[...])
    a_ref[...] = _rmsnorm(h1, g_ref[...]).astype(BF16)
    acc_ref[...] = h1

    for c in range(w1_ref.shape[0]):
        a = a_ref[...]
        u = _dot(a, w1_ref[c])
        v = _dot(a, w3_ref[c])
        t = (u * jax.nn.sigmoid(u)) * v
        acc_ref[...] += _dot(t.astype(BF16), w2_ref[c])
    h2 = acc_ref[...]
    out_ref[...] = _rmsnorm(h2, fg_ref[...]) if final else h2


def _mix_ffn(h, o, wo, g, w1, w3, w2, fg, tm, final):
    r, d = h.shape
    single = pl.Buffered(1)

    def wspec(shape):
        nd = len(shape)
        return pl.BlockSpec(shape, lambda *_: (0,) * nd, pipeline_mode=single)

    return pl.pallas_call(
        functools.partial(_mix_ffn_kernel, final=final),
        grid=(r // tm,),
        in_specs=[
            pl.BlockSpec((tm, d), lambda i: (i, 0)),
            pl.BlockSpec((tm, d), lambda i: (i, 0)),
            wspec(wo.shape),
            _const_spec((1, d)),
            wspec(w1.shape),
            wspec(w3.shape),
            wspec(w2.shape),
            _const_spec((1, d)),
        ],
        out_specs=pl.BlockSpec((tm, d), lambda i: (i, 0)),
        out_shape=jax.ShapeDtypeStruct((r, d), F32),
        scratch_shapes=[pltpu.VMEM((tm, d), BF16), pltpu.VMEM((tm, d), F32)],
        compiler_params=_params("parallel"),
        name="mix_ffn",
    )(h, o, wo, g, w1, w3, w2, fg)


def _nsa_layout(d):
    kvw = N_GROUPS * HEAD_DIM
    off = {"q": 0, "kc": d, "vc": d + kvw, "ksl": d + 2 * kvw, "vsl": d + 3 * kvw, "kw": d + 4 * kvw,
           "vw": d + 5 * kvw, "gt": d + 6 * kvw}
    main = []
    for g in range(N_GROUPS):
        main.append(np.arange(HPG * HEAD_DIM) + g * HPG * HEAD_DIM)
        for name in ("ksl", "kw", "vsl", "vw"):
            main.append(off[name] + g * HEAD_DIM + np.arange(HEAD_DIM))
    main = np.concatenate(main)
    cv = off["kc"] + np.arange(2 * kvw)
    gate = np.full((N_GROUPS * LANES,), -1, np.int64)
    for g in range(N_GROUPS):
        for br in range(N_BRANCH):
            for hh in range(HPG):
                gate[g * LANES + br * HPG + hh] = br * N_HEADS + g * HPG + hh
    return main, cv, gate, off["gt"]


def _bf16_pieces(x):
    x = np.asarray(x, np.float32)
    out = []
    for _ in range(3):
        p = x.astype(BF16).astype(np.float32)
        out.append(p)
        x = (x - p).astype(np.float32)
    return out


def _nsa_tables(s, tk):
    n_slc = s // SLC_BLOCK
    n_cmp = s // CMP_STRIDE - CMP_BLOCK // CMP_STRIDE + 1
    nc = s // CMP_STRIDE
    pos = np.arange(s)
    kpos = np.zeros((s, LANES), np.float32)
    kpos[pos, pos // SLC_BLOCK] = 1.0
    for c in range(3):
        kpos[:, POS_LANE + c] = (pos // tk) * tk
        kpos[:, POS_LANE + 3 + c] = pos % tk
    cs = np.arange(n_cmp) * CMP_STRIDE
    ss = np.arange(n_slc) * SLC_BLOCK
    ov = np.clip(np.minimum(cs[:, None] + CMP_BLOCK, ss[None] + SLC_BLOCK) - np.maximum(cs[:, None], ss[None]), 0, None)
    jp = -(-n_slc // 16) * 16
    cmapt = np.zeros((jp, nc), np.float32)
    cmapt[:n_slc, :n_cmp] = (ov / CMP_STRIDE).T
    slopes = np.asarray(2.0 ** (-8.0 * (np.arange(N_HEADS) + 1) / N_HEADS), np.float32)
    hconst = np.zeros((N_GROUPS, 2 * HPG, LANES), np.float32)
    slopes = (slopes.astype(np.float64) * LOG2E).astype(np.float32)
    pieces = _bf16_pieces(slopes)
    for g in range(N_GROUPS):
        for hh in range(HPG):
            for c in range(3):
                hconst[g, hh, POS_LANE + c] = pieces[c][g * HPG + hh]
                hconst[g, hh, POS_LANE + 3 + c] = pieces[c][g * HPG + hh]
            hconst[g, HPG + hh, :] = slopes[g * HPG + hh]
    n_kt = s // tk
    nwt = min(WINDOW // tk + 1, n_kt)
    tt = np.arange(tk)[None, :]

    def dist(r, nt):
        return r * tk + tt - np.arange(nt * tk)[:, None]

    wbias = np.stack([np.where((dist(r, nwt) >= 0) & (dist(r, nwt) < WINDOW), 0.0, NEG) for r in range(nwt)])
    sbias = np.stack([np.where(dist(r, 2) >= 0, 0.0, NEG) for r in range(2)])
    return (jnp.asarray(kpos, BF16), jnp.asarray(cmapt, BF16), jnp.asarray(hconst, F32),
            jnp.asarray(wbias, F32), jnp.asarray(sbias, F32))


def _ffn_chunk(d_ff):
    for tf in (512, 256, 128):
        if d_ff % tf == 0:
            return tf
    return d_ff


def kernel(x, mix_norm_g, ffn_norm_g, final_norm_g, nsa_w_in, nsa_gate_b, nsa_cmp_pe, nsa_cmp_w1, nsa_cmp_w2,
           nsa_w_out, sb_w_in, sb_w_out, ffn_w1, ffn_w3, ffn_w2):
    b, s, d = x.shape
    depth = mix_norm_g.shape[0]
    d_ff = ffn_w1.shape[2]
    tm = min(512, s)
    tq = min(256, s)
    tf = _ffn_chunk(d_ff)
    nch = d_ff // tf

    main_idx, cv_idx, gate_idx, gt_off = _nsa_layout(d)
    gate_valid = jnp.asarray(gate_idx >= 0)
    gate_cols = jnp.asarray(np.maximum(gate_idx, 0))
    kpos, cmapt, hconst, wbias, sbias = _nsa_tables(s, tq)
    upper = -np.triu(np.ones((tq, tq), np.float32))
    tri = jnp.asarray(np.concatenate([np.concatenate([upper, upper], axis=1),
                                      -np.ones((16, 2 * tq), np.float32)], axis=0), BF16)
    chunk = CMP_STRIDE * HEAD_DIM
    nc = s // CMP_STRIDE

    h = x
    for i in range(depth):
        j = i // 2
        gm = mix_norm_g[i].reshape(1, d)
        if i % 2 == 0:
            w_in = nsa_w_in[j]
            w_main = w_in[:, main_idx].astype(BF16)
            w_cv = w_in[:, cv_idx].astype(BF16)
            w_gate = jnp.where(gate_valid[None, :], w_in[:, gt_off + gate_cols], 0.0).astype(BF16)
            gate_b = jnp.where(gate_valid, nsa_gate_b[j][gate_cols], 0.0).reshape(1, -1)
            main, cv, gates = _nsa_proj(h, gm, w_main, w_cv, w_gate, gate_b, tm)
            c = cv.reshape(b, 2, N_GROUPS * nc, chunk)
            pe = nsa_cmp_pe[j].reshape(2, 2, chunk)
            w2pad = jnp.zeros((2, nsa_cmp_w2.shape[2], LANES), F32)
            w2pad = w2pad.at[0, :, :HEAD_DIM].set(nsa_cmp_w2[j, 0]).at[1, :, HEAD_DIM:].set(nsa_cmp_w2[j, 1])
            cmp = _nsa_compress(c, pe, nsa_cmp_w1[j].astype(BF16), w2pad.astype(BF16))
            o = _nsa_attn(main, cmp, gates, kpos, cmapt, hconst, wbias, sbias, tq)
            w_out = nsa_w_out[j]
        else:
            qkv = _sb_proj(h, gm, sb_w_in[j].astype(BF16), tm)
            o = _sb_attn(qkv, tri, tq)
            w_out = sb_w_out[j]
        w1 = ffn_w1[i].reshape(d, nch, tf).transpose(1, 0, 2).astype(BF16)
        w3 = ffn_w3[i].reshape(d, nch, tf).transpose(1, 0, 2).astype(BF16)
        w2 = ffn_w2[i].reshape(nch, tf, d).astype(BF16)
        final = i == depth - 1
        h = _mix_ffn(h.reshape(b * s, d), o.reshape(b * s, d), w_out.astype(BF16), ffn_norm_g[i].reshape(1, d),
                     w1, w3, w2, final_norm_g.reshape(1, d), tm, final).reshape(b, s, d)
    return h
```

```python
import functools

import numpy as np
import jax
import jax.numpy as jnp
from jax import lax
from jax.experimental import pallas as pl
from jax.experimental.pallas import tpu as pltpu

F32 = jnp.float32
BF16 = jnp.bfloat16

RMS_EPS = 1e-6
NEG = -1e30
BIG = 1e30

N_HEADS = 16
HEAD_DIM = 64
N_GROUPS = 4
HPG = N_HEADS // N_GROUPS
N_BRANCH = 3
CMP_BLOCK = 32
CMP_STRIDE = 16
SLC_BLOCK = 64
N_SLC_BLOCKS = 8
WINDOW = 512

LANES = 128
POS_LANE = 32
LOG2E = 1.4426950408889634
VMEM_LIMIT = 56 * 1024 * 1024
SB_CHAINS = 4
NSA_CHAINS = 2

_NT = (((1,), (1,)), ((), ()))


def _dot(a, b):
    return jnp.dot(a, b, preferred_element_type=F32)


def _dot_nt(a, b):
    return lax.dot_general(a, b, _NT, preferred_element_type=F32)


def _rmsnorm(x, g):
    ms = jnp.mean(x * x, axis=-1, keepdims=True)
    return (x * lax.rsqrt(ms + RMS_EPS)) * g


def _split_bf16(x):
    hi = x.astype(BF16)
    lo = (x - hi.astype(F32)).astype(BF16)
    return hi, lo


def _neg_abs(x):
    return lax.bitcast_convert_type(lax.bitcast_convert_type(x, jnp.uint32) | jnp.uint32(0x80000000), F32)


def _params(*sem):
    return pltpu.CompilerParams(dimension_semantics=sem, vmem_limit_bytes=VMEM_LIMIT)


def _const_spec(shape):
    nd = len(shape)
    return pl.BlockSpec(shape, lambda *_: (0,) * nd)


def _nsa_proj_kernel(x_ref, g_ref, wm_ref, wc_ref, wg_ref, gb_ref, main_ref, cv_ref, gate_ref):
    a = _rmsnorm(x_ref[...], g_ref[...]).astype(BF16)
    main_ref[...] = _dot(a, wm_ref[...]).astype(main_ref.dtype)
    cv = _dot(a, wc_ref[...])
    for kv in range(2):
        for g in range(N_GROUPS):
            c0 = (kv * N_GROUPS + g) * HEAD_DIM
            cv_ref[kv, g, :, :] = cv[:, c0:c0 + HEAD_DIM]
    gate_ref[...] = jax.nn.sigmoid(_dot(a, wg_ref[...]) + gb_ref[...])


def _nsa_proj(h, g, w_main, w_cv, w_gate, gate_b, tm):
    b, s, d = h.shape
    nm, ng = w_main.shape[1], w_gate.shape[1]
    return pl.pallas_call(
        _nsa_proj_kernel,
        grid=(b, s // tm),
        in_specs=[
            pl.BlockSpec((None, tm, d), lambda bi, r: (bi, r, 0)),
            _const_spec((1, d)),
            _const_spec(w_main.shape),
            _const_spec(w_cv.shape),
            _const_spec(w_gate.shape),
            _const_spec((1, ng)),
        ],
        out_specs=[
            pl.BlockSpec((None, tm, nm), lambda bi, r: (bi, r, 0)),
            pl.BlockSpec((None, 2, N_GROUPS, tm, HEAD_DIM), lambda bi, r: (bi, 0, 0, r, 0)),
            pl.BlockSpec((None, tm, ng), lambda bi, r: (bi, r, 0)),
        ],
        out_shape=[
            jax.ShapeDtypeStruct((b, s, nm), BF16),
            jax.ShapeDtypeStruct((b, 2, N_GROUPS, s, HEAD_DIM), F32),
            jax.ShapeDtypeStruct((b, s, ng), F32),
        ],
        compiler_params=_params("parallel", "parallel"),
        name="nsa_proj",
    )(h, g, w_main, w_cv, w_gate, gate_b)


def _sb_proj_kernel(x_ref, g_ref, w_ref, o_ref):
    a = _rmsnorm(x_ref[...], g_ref[...]).astype(BF16)
    o_ref[...] = _dot(a, w_ref[...]).astype(o_ref.dtype)


def _sb_proj(h, g, w, tm):
    b, s, d = h.shape
    n = w.shape[1]
    return pl.pallas_call(
        _sb_proj_kernel,
        grid=(b, s // tm),
        in_specs=[
            pl.BlockSpec((None, tm, d), lambda bi, r: (bi, r, 0)),
            _const_spec((1, d)),
            _const_spec(w.shape),
        ],
        out_specs=pl.BlockSpec((None, tm, n), lambda bi, r: (bi, r, 0)),
        out_shape=jax.ShapeDtypeStruct((b, s, n), BF16),
        compiler_params=_params("parallel", "parallel"),
        name="sb_proj",
    )(h, g, w)


def _nsa_compress_kernel(c_ref, pe_ref, w1_ref, w2_ref, o_ref):
    rows = c_ref.shape[1]
    half = c_ref.shape[2]
    out = jnp.zeros(o_ref.shape, F32)
    for kv in range(2):
        c = c_ref[kv]
        top = (c + pe_ref[kv, 0:1, :]).astype(BF16)
        bot = (c + pe_ref[kv, 1:2, :]).astype(BF16)
        a_top = _dot(top, w1_ref[kv, 0:half, :])
        a_bot = _dot(bot, w1_ref[kv, half:2 * half, :])
        hid = jax.nn.gelu(a_top + pltpu.roll(a_bot, rows - 1, 0), approximate=True)
        out = out + _dot(hid.astype(BF16), w2_ref[kv])
    o_ref[...] = out


def _nsa_compress(c, pe, w1, w2pad):
    b, _, rows, width = c.shape
    return pl.pallas_call(
        _nsa_compress_kernel,
        grid=(b,),
        in_specs=[
            pl.BlockSpec((None, 2, rows, width), lambda bi: (bi, 0, 0, 0)),
            _const_spec(pe.shape),
            _const_spec(w1.shape),
            _const_spec(w2pad.shape),
        ],
        out_specs=pl.BlockSpec((None, rows, LANES), lambda bi: (bi, 0, 0)),
        out_shape=jax.ShapeDtypeStruct((b, rows, LANES), F32),
        compiler_params=_params("parallel"),
        name="nsa_compress",
    )(c, pe, w1, w2pad)


def _nsa_attn_kernel(*refs, tq, n_slc, n_sel, n_chain):
    q_refs, kk_refs, vv_refs = refs[0:n_chain], refs[n_chain:2 * n_chain], refs[2 * n_chain:3 * n_chain]
    cmp_ref, gate_ref, kpos_ref, cmapt_ref, hconst_ref, wb_ref, sb_ref, o_ref = refs[3 * n_chain:3 * n_chain + 8]
    kaug_s_ref, kaug_w_ref, vts_ref, vtw_ref, qsel_ref, qwin_ref = refs[3 * n_chain + 8:]
    chains = range(n_chain)
    i = pl.program_id(2)
    t0 = i * tq
    tk = tq
    n_kt = kaug_s_ref.shape[1] // tk
    gw = HPG * HEAD_DIM

    @pl.when(i == 0)
    def _():
        top = lax.broadcasted_iota(jnp.int32, (LANES, tk), 0) < HEAD_DIM
        for c in chains:
            kk = kk_refs[c][...].astype(F32)
            kk_sw = pltpu.roll(kk, HEAD_DIM, 1)
            first = lax.broadcasted_iota(jnp.int32, kk.shape, 1) < HEAD_DIM
            kaug_s_ref[c, :, 0:LANES] = jnp.where(first, kk, kk_sw).astype(BF16)
            kaug_w_ref[c, :, 0:LANES] = jnp.where(first, kk_sw, kk).astype(BF16)
            kaug_s_ref[c, :, LANES:2 * LANES] = kpos_ref[...]
            kaug_w_ref[c, :, LANES:2 * LANES] = kpos_ref[...]
            for j in range(n_kt):
                vt = vv_refs[c][j * tk:(j + 1) * tk, :].astype(F32).T
                vts_ref[c, j] = jnp.where(top, vt, 1.0).astype(BF16)
                vtw_ref[c, j] = jnp.where(top, 1.0, vt).astype(BF16)

    low = lax.broadcasted_iota(jnp.int32, (tq, LANES), 1) < HEAD_DIM

    qa = []
    for c in chains:
        q4 = q_refs[c][...].astype(F32) * (HEAD_DIM ** -0.5 * LOG2E)
        q4_hi = q4.astype(BF16).astype(F32)
        q4_lo = q4 - q4_hi
        per_head = []
        for h in range(HPG):
            cols = slice(LANES * (h // 2), LANES * (h // 2 + 1))
            if h % 2 == 0:
                per_head.append(jnp.where(low, q4_hi[:, cols], pltpu.roll(q4_lo[:, cols], HEAD_DIM, 1)).astype(BF16))
            else:
                per_head.append(jnp.where(low, pltpu.roll(q4_hi[:, cols], HEAD_DIM, 1), q4_lo[:, cols]).astype(BF16))
        qa.append(per_head)

    def scores(kref, qref, c, j0, nt):
        kt = kref[c, pl.ds(pl.multiple_of(j0 * tk, tk), nt * tk), :]
        return _dot_nt(kt, qref[c])

    def pv(vt_ref, c, j0, nt, p):
        acc = _dot(vt_ref[c, j0], p[0:tk])
        for u in range(1, nt):
            acc = acc + _dot(vt_ref[c, j0 + u], p[u * tk:(u + 1) * tk])
        return acc

    def heads(bias):
        return jnp.concatenate([bias] * HPG, axis=1)

    for c in chains:
        for h in range(HPG):
            qwin_ref[c, h * tq:(h + 1) * tq, 0:LANES] = qa[c][h]
            qwin_ref[c, h * tq:(h + 1) * tq, LANES:2 * LANES] = jnp.broadcast_to(hconst_ref[c, h:h + 1, :],
                                                                                 (tq, LANES)).astype(BF16)
    nwt = min(WINDOW // tk + 1, n_kt)
    jw = jnp.clip(i - (nwt - 1), 0, n_kt - nwt)
    s_w = [scores(kaug_w_ref, qwin_ref, c, jw, nwt) for c in chains]
    wbias = wb_ref[i - jw]

    def window_head(c, h):
        s_h = s_w[c][:, h * tq:(h + 1) * tq] + wbias
        p_h = jnp.exp2(s_h - jnp.max(s_h, axis=0, keepdims=True)).astype(BF16)
        acc = pv(vtw_ref, c, jw, nwt, p_h)
        return acc[HEAD_DIM:] * (1.0 / acc[0:1])

    nc = cmp_ref.shape[0] // n_chain
    ck = [cmp_ref[c * nc:(c + 1) * nc, :] for c in chains]
    first_c = lax.broadcasted_iota(jnp.int32, (nc, LANES), 1) < HEAD_DIM
    sc_all = [_dot_nt(jnp.where(first_c, ck[c], pltpu.roll(ck[c], HEAD_DIM, 1)).astype(BF16),
                      jnp.concatenate(qa[c], axis=0)) for c in chains]
    dist_c = ((t0 + lax.broadcasted_iota(jnp.int32, (nc, tq), 1))
              - (lax.broadcasted_iota(jnp.int32, (nc, tq), 0) * CMP_STRIDE + (CMP_BLOCK - 1)))
    valid_c = dist_c >= 0
    dist_cf = dist_c.astype(F32)
    psum, p_heads = [], []
    for c in chains:
        ps = jnp.zeros((nc, tq), F32)
        ph = []
        for h in range(HPG):
            slope = hconst_ref[c, HPG + h:HPG + h + 1, 0:1]
            s_c = jnp.where(valid_c, sc_all[c][:, h * tq:(h + 1) * tq] - slope * dist_cf, NEG)
            m_c = jnp.max(s_c, axis=0, keepdims=True)
            e_c = jnp.where(valid_c, jnp.exp2(s_c - m_c), 0.0)
            den = jnp.sum(e_c, axis=0, keepdims=True)
            p_c = e_c / jnp.maximum(den, 1e-30)
            ps = ps + p_c
            ph.append(p_c.astype(BF16))
        psum.append(ps)
        p_heads.append(ph)
    o_cmp_t = [_dot(ck[c].T.astype(BF16), jnp.concatenate(p_heads[c], axis=1))[HEAD_DIM:] for c in chains]

    cmt = cmapt_ref[...]
    jp = cmt.shape[0]
    imp_t = []
    for c in chains:
        p_hi, p_lo = _split_bf16(psum[c])
        imp_t.append(_dot(cmt, p_hi) + _dot(cmt, p_lo))
    jb = lax.broadcasted_iota(jnp.int32, (jp, tq), 0)
    cur = jnp.right_shift(t0 + lax.broadcasted_iota(jnp.int32, (jp, tq), 1), SLC_BLOCK.bit_length() - 1)
    causal_j = jb <= cur
    forced = (jb == 0) | (jb == cur) | (jb == cur - 1)
    val = [jnp.where(causal_j & forced, BIG, jnp.where(causal_j, imp_t[c], -BIG)) for c in chains]
    rank = [jnp.zeros((jp, tq), jnp.int32) for c in chains]
    o_win_heads = [[] for c in chains]
    for j in range(n_slc):
        if j % (n_slc // HPG) == 0:
            for c in chains:
                o_win_heads[c].append(window_head(c, j // (n_slc // HPG)))
        for c in chains:
            row = val[c][j:j + 1, :]
            beats = (row > val[c]) | ((row == val[c]) & (jb > j))
            rank[c] = rank[c] + beats.astype(jnp.int32)
    o_win_t = [jnp.concatenate(o_win_heads[c], axis=1) for c in chains]

    for c in chains:
        mb_t = jnp.where(causal_j & (rank[c] < n_sel), 0.0, NEG)
        if jp < LANES:
            mb_t = jnp.concatenate([mb_t, jnp.zeros((LANES - jp, tq), F32)], axis=0)
        mb = mb_t.T
        for h in range(HPG):
            qsel_ref[c, h * tq:(h + 1) * tq, 0:LANES] = qa[c][h]
            qsel_ref[c, h * tq:(h + 1) * tq, LANES:2 * LANES] = (mb + hconst_ref[c, h:h + 1, :]).astype(BF16)
    pair = i // 2
    sbias = heads(sb_ref[i - 2 * pair])
    s_d = [scores(kaug_s_ref, qsel_ref, c, 2 * pair, 2) + sbias for c in chains]
    m_d = [jnp.max(s_d[c], axis=0, keepdims=True) for c in chains]
    p_d = [jnp.exp2(s_d[c] - m_d[c]).astype(BF16) for c in chains]
    acc_d = [pv(vts_ref, c, 2 * pair, 2, p_d[c]) for c in chains]

    def sel_body(jj, carry):
        j0 = 2 * (pair - 1 - jj)
        s = [scores(kaug_s_ref, qsel_ref, c, j0, 2) for c in chains]
        m_new = [jnp.maximum(carry[c][0], jnp.max(s[c], axis=0, keepdims=True)) for c in chains]
        p = [jnp.exp2(s[c] - m_new[c]).astype(BF16) for c in chains]
        acc = [jnp.exp2(carry[c][0] - m_new[c]) * carry[c][1] + pv(vts_ref, c, j0, 2, p[c]) for c in chains]
        return tuple((m_new[c], acc[c]) for c in chains)

    res = lax.fori_loop(0, pair, sel_body, tuple((m_d[c], acc_d[c]) for c in chains))

    for c in chains:
        acc_s = res[c][1]
        o_sel_t = acc_s[:HEAD_DIM] * (1.0 / acc_s[HEAD_DIM:HEAD_DIM + 1])
        gt_t = gate_ref[:, c * LANES:(c + 1) * LANES].T
        rows = []
        for h in range(HPG):
            sl = slice(h * tq, (h + 1) * tq)
            rows.append(gt_t[h:h + 1] * o_cmp_t[c][:, sl] + gt_t[HPG + h:HPG + h + 1] * o_sel_t[:, sl]
                        + gt_t[2 * HPG + h:2 * HPG + h + 1] * o_win_t[c][:, sl])
        o_ref[:, c * gw:(c + 1) * gw] = jnp.concatenate(rows, axis=0).T.astype(o_ref.dtype)


def _nsa_attn(main, cmp, gates, kpos, cmapt, hconst, wbias, sbias, tq):
    b, s, _ = main.shape
    assert (s // tq) % 2 == 0 and N_GROUPS % NSA_CHAINS == 0
    nch = NSA_CHAINS
    nc = cmp.shape[1] // N_GROUPS
    n_slc = s // SLC_BLOCK
    gw = HPG * HEAD_DIM
    per_g = (gw + 2 * LANES) // LANES

    def q_spec(c):
        return pl.BlockSpec((None, tq, gw), lambda bi, gp, i: (bi, i, (gp * nch + c) * per_g // 2))

    def kv_spec(c, off):
        return pl.BlockSpec((None, s, LANES), lambda bi, gp, i: (bi, 0, (gp * nch + c) * per_g + off))

    kern = functools.partial(_nsa_attn_kernel, tq=tq, n_slc=n_slc, n_sel=min(N_SLC_BLOCKS, n_slc), n_chain=nch)
    return pl.pallas_call(
        kern,
        grid=(b, N_GROUPS // nch, s // tq),
        in_specs=(
            [q_spec(c) for c in range(nch)]
            + [kv_spec(c, 2) for c in range(nch)]
            + [kv_spec(c, 3) for c in range(nch)]
            + [
                pl.BlockSpec((None, nch * nc, LANES), lambda bi, gp, i: (bi, gp, 0)),
                pl.BlockSpec((None, tq, nch * LANES), lambda bi, gp, i: (bi, i, gp)),
                _const_spec(kpos.shape),
                _const_spec(cmapt.shape),
                pl.BlockSpec((nch, 2 * HPG, LANES), lambda bi, gp, i: (gp, 0, 0)),
                _const_spec(wbias.shape),
                _const_spec(sbias.shape),
            ]),
        out_specs=pl.BlockSpec((None, tq, nch * gw), lambda bi, gp, i: (bi, i, gp)),
        out_shape=jax.ShapeDtypeStruct((b, s, N_GROUPS * gw), BF16),
        scratch_shapes=[
            pltpu.VMEM((nch, s, 2 * LANES), BF16),
            pltpu.VMEM((nch, s, 2 * LANES), BF16),
            pltpu.VMEM((nch, s // tq, LANES, tq), BF16),
            pltpu.VMEM((nch, s // tq, LANES, tq), BF16),
            pltpu.VMEM((nch, HPG * tq, 2 * LANES), BF16),
            pltpu.VMEM((nch, HPG * tq, 2 * LANES), BF16),
        ],
        compiler_params=_params("parallel", "parallel", "arbitrary"),
        name="nsa_attn",
    )(*([main] * (3 * nch)), cmp, gates, kpos, cmapt, hconst, wbias, sbias)


def _sb_attn_kernel(q_ref, k_ref, v_ref, tri_ref, o_ref, vt_ref, *, tq):
    i = pl.program_id(2)
    tk = tq
    n_kt = k_ref.shape[0] // tk
    n_chain = q_ref.shape[1] // LANES

    @pl.when(i == 0)
    def _():
        for c in range(n_chain):
            for j in range(n_kt):
                vt_ref[c, j] = v_ref[j * tk:(j + 1) * tk, c * LANES:(c + 1) * LANES].astype(F32).T.astype(BF16)

    low = lax.broadcasted_iota(jnp.int32, (tq, LANES), 1) < HEAD_DIM
    before = (lax.broadcasted_iota(jnp.int32, (tk, 2 * tq), 0)
              < (lax.broadcasted_iota(jnp.int32, (tk, 2 * tq), 1) & (tq - 1)))
    tri = tri_ref[...]

    qst = []
    for c in range(n_chain):
        q2 = q_ref[:, c * LANES:(c + 1) * LANES].astype(F32) * (HEAD_DIM ** -0.5)
        qst.append(jnp.concatenate([jnp.where(low, q2, 0.0), jnp.where(low, 0.0, q2)], axis=0).astype(BF16))

    chains = range(n_chain)

    def tiles(j, diag, carry):
        off = pl.multiple_of(j * tk, tk)
        z = [_dot_nt(k_ref[pl.ds(off, tk), c * LANES:(c + 1) * LANES], qst[c]) for c in chains]
        sp = [jnp.maximum(z[c], 0.0) + jnp.log(1.0 + jnp.exp(_neg_abs(z[c]))) for c in chains]
        if diag:
            sp = [jnp.where(before, sp[c], 0.0) for c in chains]
        sums = [_dot(tri, sp[c].astype(BF16)) for c in chains]
        if diag:
            a = [jnp.where(before, jnp.exp(z[c] - sp[c] + sums[c][:tk]), 0.0) for c in chains]
        else:
            a = [jnp.exp(z[c] - sp[c] + sums[c][:tk] + carry[c][1]) for c in chains]
        pv = [_dot(vt_ref[c, j], a[c].astype(BF16)) for c in chains]
        if diag:
            return tuple((pv[c], sums[c][tk:tk + 1]) for c in chains)
        return tuple((carry[c][0] + pv[c], carry[c][1] + sums[c][tk:tk + 1]) for c in chains)

    res = lax.fori_loop(0, i, lambda jj, carry: tiles(i - 1 - jj, False, carry), tiles(i, True, None))
    for c in range(n_chain):
        acc = res[c][0]
        out_t = jnp.concatenate([acc[:HEAD_DIM, :tq], acc[HEAD_DIM:, tq:]], axis=0)
        o_ref[:, c * LANES:(c + 1) * LANES] = out_t.T.astype(o_ref.dtype)


def _sb_attn(qkv, tri, tq):
    b, s, n3 = qkv.shape
    w = LANES * SB_CHAINS
    nblk = n3 // 3 // w
    return pl.pallas_call(
        functools.partial(_sb_attn_kernel, tq=tq),
        grid=(b, nblk, s // tq),
        in_specs=[
            pl.BlockSpec((None, tq, w), lambda bi, p, i: (bi, i, p)),
            pl.BlockSpec((None, s, w), lambda bi, p, i: (bi, 0, nblk + p)),
            pl.BlockSpec((None, s, w), lambda bi, p, i: (bi, 0, 2 * nblk + p)),
            _const_spec(tri.shape),
        ],
        out_specs=pl.BlockSpec((None, tq, w), lambda bi, p, i: (bi, i, p)),
        out_shape=jax.ShapeDtypeStruct((b, s, nblk * w), BF16),
        scratch_shapes=[pltpu.VMEM((SB_CHAINS, s // tq, LANES, tq), BF16)],
        compiler_params=_params("parallel", "parallel", "arbitrary"),
        name="sb_attn",
    )(qkv, qkv, qkv, tri)


def _mix_ffn_kernel(h_ref, o_ref, wo_ref, g_ref, w1_ref, w3_ref, w2_ref, fg_ref, out_ref, a_ref, acc_ref,
                    *, final):
    h1 = h_ref[...] + _dot(o_ref[...], wo_ref[...])
    a_ref[...] = _rmsnorm(h1, g_ref[...]).astype(BF16)
    acc_ref[...] = h1

    for c in range(w1_ref.shape[0]):
        a = a_ref[...]
        u = _dot(a, w1_ref[c])
        v = _dot(a, w3_ref[c])
        t = (u * jax.nn.sigmoid(u)) * v
        acc_ref[...] += _dot(t.astype(BF16), w2_ref[c])
    h2 = acc_ref[...]
    out_ref[...] = _rmsnorm(h2, fg_ref[...]) if final else h2


def _mix_ffn(h, o, wo, g, w1, w3, w2, fg, tm, final):
    r, d = h.shape
    single = pl.Buffered(1)

    def wspec(shape):
        nd = len(shape)
        return pl.BlockSpec(shape, lambda *_: (0,) * nd, pipeline_mode=single)

    return pl.pallas_call(
        functools.partial(_mix_ffn_kernel, final=final),
        grid=(r // tm,),
        in_specs=[
            pl.BlockSpec((tm, d), lambda i: (i, 0)),
            pl.BlockSpec((tm, d), lambda i: (i, 0)),
            wspec(wo.shape),
            _const_spec((1, d)),
            wspec(w1.shape),
            wspec(w3.shape),
            wspec(w2.shape),
            _const_spec((1, d)),
        ],
        out_specs=pl.BlockSpec((tm, d), lambda i: (i, 0)),
        out_shape=jax.ShapeDtypeStruct((r, d), F32),
        scratch_shapes=[pltpu.VMEM((tm, d), BF16), pltpu.VMEM((tm, d), F32)],
        compiler_params=_params("parallel"),
        name="mix_ffn",
    )(h, o, wo, g, w1, w3, w2, fg)


def _nsa_layout(d):
    kvw = N_GROUPS * HEAD_DIM
    off = {"q": 0, "kc": d, "vc": d + kvw, "ksl": d + 2 * kvw, "vsl": d + 3 * kvw, "kw": d + 4 * kvw,
           "vw": d + 5 * kvw, "gt": d + 6 * kvw}
    main = []
    for g in range(N_GROUPS):
        main.append(np.arange(HPG * HEAD_DIM) + g * HPG * HEAD_DIM)
        for name in ("ksl", "kw", "vsl", "vw"):
            main.append(off[name] + g * HEAD_DIM + np.arange(HEAD_DIM))
    main = np.concatenate(main)
    cv = off["kc"] + np.arange(2 * kvw)
    gate = np.full((N_GROUPS * LANES,), -1, np.int64)
    for g in range(N_GROUPS):
        for br in range(N_BRANCH):
            for hh in range(HPG):
                gate[g * LANES + br * HPG + hh] = br * N_HEADS + g * HPG + hh
    return main, cv, gate, off["gt"]


def _bf16_pieces(x):
    x = np.asarray(x, np.float32)
    out = []
    for _ in range(3):
        p = x.astype(BF16).astype(np.float32)
        out.append(p)
        x = (x - p).astype(np.float32)
    return out


def _nsa_tables(s, tk):
    n_slc = s // SLC_BLOCK
    n_cmp = s // CMP_STRIDE - CMP_BLOCK // CMP_STRIDE + 1
    nc = s // CMP_STRIDE
    pos = np.arange(s)
    kpos = np.zeros((s, LANES), np.float32)
    kpos[pos, pos // SLC_BLOCK] = 1.0
    for c in range(3):
        kpos[:, POS_LANE + c] = (pos // tk) * tk
        kpos[:, POS_LANE + 3 + c] = pos % tk
    cs = np.arange(n_cmp) * CMP_STRIDE
    ss = np.arange(n_slc) * SLC_BLOCK
    ov = np.clip(np.minimum(cs[:, None] + CMP_BLOCK, ss[None] + SLC_BLOCK) - np.maximum(cs[:, None], ss[None]), 0, None)
    jp = -(-n_slc // 16) * 16
    cmapt = np.zeros((jp, nc), np.float32)
    cmapt[:n_slc, :n_cmp] = (ov / CMP_STRIDE).T
    slopes = np.asarray(2.0 ** (-8.0 * (np.arange(N_HEADS) + 1) / N_HEADS), np.float32)
    hconst = np.zeros((N_GROUPS, 2 * HPG, LANES), np.float32)
    slopes = (slopes.astype(np.float64) * LOG2E).astype(np.float32)
    pieces = _bf16_pieces(slopes)
    for g in range(N_GROUPS):
        for hh in range(HPG):
            for c in range(3):
                hconst[g, hh, POS_LANE + c] = pieces[c][g * HPG + hh]
                hconst[g, hh, POS_LANE + 3 + c] = pieces[c][g * HPG + hh]
            hconst[g, HPG + hh, :] = slopes[g * HPG + hh]
    n_kt = s // tk
    nwt = min(WINDOW // tk + 1, n_kt)
    tt = np.arange(tk)[None, :]

    def dist(r, nt):
        return r * tk + tt - np.arange(nt * tk)[:, None]

    wbias = np.stack([np.where((dist(r, nwt) >= 0) & (dist(r, nwt) < WINDOW), 0.0, NEG) for r in range(nwt)])
    sbias = np.stack([np.where(dist(r, 2) >= 0, 0.0, NEG) for r in range(2)])
    return (jnp.asarray(kpos, BF16), jnp.asarray(cmapt, BF16), jnp.asarray(hconst, F32),
            jnp.asarray(wbias, F32), jnp.asarray(sbias, F32))


def _ffn_chunk(d_ff):
    for tf in (512, 256, 128):
        if d_ff % tf == 0:
            return tf
    return d_ff


def kernel(x, mix_norm_g, ffn_norm_g, final_norm_g, nsa_w_in, nsa_gate_b, nsa_cmp_pe, nsa_cmp_w1, nsa_cmp_w2,
           nsa_w_out, sb_w_in, sb_w_out, ffn_w1, ffn_w3, ffn_w2):
    b, s, d = x.shape
    depth = mix_norm_g.shape[0]
    d_ff = ffn_w1.shape[2]
    tm = min(512, s)
    tq = min(256, s)
    tf = _ffn_chunk(d_ff)
    nch = d_ff // tf

    main_idx, cv_idx, gate_idx, gt_off = _nsa_layout(d)
    gate_valid = jnp.asarray(gate_idx >= 0)
    gate_cols = jnp.asarray(np.maximum(gate_idx, 0))
    kpos, cmapt, hconst, wbias, sbias = _nsa_tables(s, tq)
    tri = jnp.asarray(np.concatenate([-np.triu(np.ones((tq, tq), np.float32), 1),
                                      -np.ones((16, tq), np.float32)], axis=0), BF16)
    chunk = CMP_STRIDE * HEAD_DIM
    nc = s // CMP_STRIDE

    h = x
    for i in range(depth):
        j = i // 2
        gm = mix_norm_g[i].reshape(1, d)
        if i % 2 == 0:
            w_in = nsa_w_in[j]
            w_main = w_in[:, main_idx].astype(BF16)
            w_cv = w_in[:, cv_idx].astype(BF16)
            w_gate = jnp.where(gate_valid[None, :], w_in[:, gt_off + gate_cols], 0.0).astype(BF16)
            gate_b = jnp.where(gate_valid, nsa_gate_b[j][gate_cols], 0.0).reshape(1, -1)
            main, cv, gates = _nsa_proj(h, gm, w_main, w_cv, w_gate, gate_b, tm)
            c = cv.reshape(b, 2, N_GROUPS * nc, chunk)
            pe = nsa_cmp_pe[j].reshape(2, 2, chunk)
            w2pad = jnp.zeros((2, nsa_cmp_w2.shape[2], LANES), F32)
            w2pad = w2pad.at[0, :, :HEAD_DIM].set(nsa_cmp_w2[j, 0]).at[1, :, HEAD_DIM:].set(nsa_cmp_w2[j, 1])
            cmp = _nsa_compress(c, pe, nsa_cmp_w1[j].astype(BF16), w2pad.astype(BF16))
            o = _nsa_attn(main, cmp, gates, kpos, cmapt, hconst, wbias, sbias, tq)
            w_out = nsa_w_out[j]
        else:
            qkv = _sb_proj(h, gm, sb_w_in[j].astype(BF16), tm)
            o = _sb_attn(qkv, tri, tq)
            w_out = sb_w_out[j]
        w1 = ffn_w1[i].reshape(d, nch, tf).transpose(1, 0, 2).astype(BF16)
        w3 = ffn_w3[i].reshape(d, nch, tf).transpose(1, 0, 2).astype(BF16)
        w2 = ffn_w2[i].reshape(nch, tf, d).astype(BF16)
        final = i == depth - 1
        h = _mix_ffn(h.reshape(b * s, d), o.reshape(b * s, d), w_out.astype(BF16), ffn_norm_g[i].reshape(1, d),
                     w1, w3, w2, final_norm_g.reshape(1, d), tm, final).reshape(b, s, d)
    return h
```

```python
import functools

import numpy as np
import jax
import jax.numpy as jnp
from jax import lax
from jax.experimental import pallas as pl
from jax.experimental.pallas import tpu as pltpu

F32 = jnp.float32
BF16 = jnp.bfloat16

RMS_EPS = 1e-6
NEG = -1e30
BIG = 1e30

N_HEADS = 16
HEAD_DIM = 64
N_GROUPS = 4
HPG = N_HEADS // N_GROUPS
N_BRANCH = 3
CMP_BLOCK = 32
CMP_STRIDE = 16
SLC_BLOCK = 64
N_SLC_BLOCKS = 8
WINDOW = 512

LANES = 128
POS_LANE = 32
LOG2E = 1.4426950408889634
VMEM_LIMIT = 56 * 1024 * 1024
SB_CHAINS = 8
NSA_CHAINS = 4

_NT = (((1,), (1,)), ((), ()))


def _dot(a, b):
    return jnp.dot(a, b, preferred_element_type=F32)


def _dot_nt(a, b):
    return lax.dot_general(a, b, _NT, preferred_element_type=F32)


def _rmsnorm(x, g):
    ms = jnp.mean(x * x, axis=-1, keepdims=True)
    return (x * lax.rsqrt(ms + RMS_EPS)) * g


def _split_bf16(x):
    hi = x.astype(BF16)
    lo = (x - hi.astype(F32)).astype(BF16)
    return hi, lo


def _neg_abs(x):
    return lax.bitcast_convert_type(lax.bitcast_convert_type(x, jnp.uint32) | jnp.uint32(0x80000000), F32)


def _params(*sem):
    return pltpu.CompilerParams(dimension_semantics=sem, vmem_limit_bytes=VMEM_LIMIT)


def _const_spec(shape):
    nd = len(shape)
    return pl.BlockSpec(shape, lambda *_: (0,) * nd)


def _nsa_proj_kernel(x_ref, g_ref, wm_ref, wc_ref, wg_ref, gb_ref, main_ref, cv_ref, gate_ref):
    a = _rmsnorm(x_ref[...], g_ref[...]).astype(BF16)
    main_ref[...] = _dot(a, wm_ref[...]).astype(main_ref.dtype)
    cv = _dot(a, wc_ref[...])
    for kv in range(2):
        for g in range(N_GROUPS):
            c0 = (kv * N_GROUPS + g) * HEAD_DIM
            cv_ref[kv, g, :, :] = cv[:, c0:c0 + HEAD_DIM]
    gate_ref[...] = jax.nn.sigmoid(_dot(a, wg_ref[...]) + gb_ref[...])


def _nsa_proj(h, g, w_main, w_cv, w_gate, gate_b, tm):
    b, s, d = h.shape
    nm, ng = w_main.shape[1], w_gate.shape[1]
    return pl.pallas_call(
        _nsa_proj_kernel,
        grid=(b, s // tm),
        in_specs=[
            pl.BlockSpec((None, tm, d), lambda bi, r: (bi, r, 0)),
            _const_spec((1, d)),
            _const_spec(w_main.shape),
            _const_spec(w_cv.shape),
            _const_spec(w_gate.shape),
            _const_spec((1, ng)),
        ],
        out_specs=[
            pl.BlockSpec((None, tm, nm), lambda bi, r: (bi, r, 0)),
            pl.BlockSpec((None, 2, N_GROUPS, tm, HEAD_DIM), lambda bi, r: (bi, 0, 0, r, 0)),
            pl.BlockSpec((None, tm, ng), lambda bi, r: (bi, r, 0)),
        ],
        out_shape=[
            jax.ShapeDtypeStruct((b, s, nm), BF16),
            jax.ShapeDtypeStruct((b, 2, N_GROUPS, s, HEAD_DIM), F32),
            jax.ShapeDtypeStruct((b, s, ng), F32),
        ],
        compiler_params=_params("parallel", "parallel"),
        name="nsa_proj",
    )(h, g, w_main, w_cv, w_gate, gate_b)


def _sb_proj_kernel(x_ref, g_ref, w_ref, o_ref):
    a = _rmsnorm(x_ref[...], g_ref[...]).astype(BF16)
    o_ref[...] = _dot(a, w_ref[...]).astype(o_ref.dtype)


def _sb_proj(h, g, w, tm):
    b, s, d = h.shape
    n = w.shape[1]
    return pl.pallas_call(
        _sb_proj_kernel,
        grid=(b, s // tm),
        in_specs=[
            pl.BlockSpec((None, tm, d), lambda bi, r: (bi, r, 0)),
            _const_spec((1, d)),
            _const_spec(w.shape),
        ],
        out_specs=pl.BlockSpec((None, tm, n), lambda bi, r: (bi, r, 0)),
        out_shape=jax.ShapeDtypeStruct((b, s, n), BF16),
        compiler_params=_params("parallel", "parallel"),
        name="sb_proj",
    )(h, g, w)


def _nsa_compress_kernel(c_ref, pe_ref, w1_ref, w2_ref, o_ref):
    rows = c_ref.shape[1]
    half = c_ref.shape[2]
    out = jnp.zeros(o_ref.shape, F32)
    for kv in range(2):
        c = c_ref[kv]
        top = (c + pe_ref[kv, 0:1, :]).astype(BF16)
        bot = (c + pe_ref[kv, 1:2, :]).astype(BF16)
        a_top = _dot(top, w1_ref[kv, 0:half, :])
        a_bot = _dot(bot, w1_ref[kv, half:2 * half, :])
        hid = jax.nn.gelu(a_top + pltpu.roll(a_bot, rows - 1, 0), approximate=True)
        out = out + _dot(hid.astype(BF16), w2_ref[kv])
    o_ref[...] = out


def _nsa_compress(c, pe, w1, w2pad):
    b, _, rows, width = c.shape
    return pl.pallas_call(
        _nsa_compress_kernel,
        grid=(b,),
        in_specs=[
            pl.BlockSpec((None, 2, rows, width), lambda bi: (bi, 0, 0, 0)),
            _const_spec(pe.shape),
            _const_spec(w1.shape),
            _const_spec(w2pad.shape),
        ],
        out_specs=pl.BlockSpec((None, rows, LANES), lambda bi: (bi, 0, 0)),
        out_shape=jax.ShapeDtypeStruct((b, rows, LANES), F32),
        compiler_params=_params("parallel"),
        name="nsa_compress",
    )(c, pe, w1, w2pad)


def _nsa_attn_kernel(*refs, tq, n_slc, n_sel, n_chain):
    q_refs, kk_refs, vv_refs = refs[0:n_chain], refs[n_chain:2 * n_chain], refs[2 * n_chain:3 * n_chain]
    cmp_ref, gate_ref, kpos_ref, cmapt_ref, hconst_ref, wb_ref, sb_ref, o_ref = refs[3 * n_chain:3 * n_chain + 8]
    kaug_s_ref, kaug_w_ref, vts_ref, vtw_ref, qsel_ref, qwin_ref = refs[3 * n_chain + 8:]
    chains = range(n_chain)
    i = pl.program_id(2)
    t0 = i * tq
    tk = tq
    n_kt = kaug_s_ref.shape[1] // tk
    gw = HPG * HEAD_DIM

    @pl.when(i == 0)
    def _():
        top = lax.broadcasted_iota(jnp.int32, (LANES, tk), 0) < HEAD_DIM
        for c in chains:
            kk = kk_refs[c][...].astype(F32)
            kk_sw = pltpu.roll(kk, HEAD_DIM, 1)
            first = lax.broadcasted_iota(jnp.int32, kk.shape, 1) < HEAD_DIM
            kaug_s_ref[c, :, 0:LANES] = jnp.where(first, kk, kk_sw).astype(BF16)
            kaug_w_ref[c, :, 0:LANES] = jnp.where(first, kk_sw, kk).astype(BF16)
            kaug_s_ref[c, :, LANES:2 * LANES] = kpos_ref[...]
            kaug_w_ref[c, :, LANES:2 * LANES] = kpos_ref[...]
            for j in range(n_kt):
                vt = vv_refs[c][j * tk:(j + 1) * tk, :].astype(F32).T
                vts_ref[c, j] = jnp.where(top, vt, 1.0).astype(BF16)
                vtw_ref[c, j] = jnp.where(top, 1.0, vt).astype(BF16)

    low = lax.broadcasted_iota(jnp.int32, (tq, LANES), 1) < HEAD_DIM

    qa = []
    for c in chains:
        q4 = q_refs[c][...].astype(F32) * (HEAD_DIM ** -0.5 * LOG2E)
        q4_hi = q4.astype(BF16).astype(F32)
        q4_lo = q4 - q4_hi
        per_head = []
        for h in range(HPG):
            cols = slice(LANES * (h // 2), LANES * (h // 2 + 1))
            if h % 2 == 0:
                per_head.append(jnp.where(low, q4_hi[:, cols], pltpu.roll(q4_lo[:, cols], HEAD_DIM, 1)).astype(BF16))
            else:
                per_head.append(jnp.where(low, pltpu.roll(q4_hi[:, cols], HEAD_DIM, 1), q4_lo[:, cols]).astype(BF16))
        qa.append(per_head)

    def scores(kref, qref, c, j0, nt):
        kt = kref[c, pl.ds(pl.multiple_of(j0 * tk, tk), nt * tk), :]
        return _dot_nt(kt, qref[c])

    def pv(vt_ref, c, j0, nt, p):
        acc = _dot(vt_ref[c, j0], p[0:tk])
        for u in range(1, nt):
            acc = acc + _dot(vt_ref[c, j0 + u], p[u * tk:(u + 1) * tk])
        return acc

    def heads(bias):
        return jnp.concatenate([bias] * HPG, axis=1)

    for c in chains:
        for h in range(HPG):
            qwin_ref[c, h * tq:(h + 1) * tq, 0:LANES] = qa[c][h]
            qwin_ref[c, h * tq:(h + 1) * tq, LANES:2 * LANES] = jnp.broadcast_to(hconst_ref[c, h:h + 1, :],
                                                                                 (tq, LANES)).astype(BF16)
    nwt = min(WINDOW // tk + 1, n_kt)
    jw = jnp.clip(i - (nwt - 1), 0, n_kt - nwt)
    s_w = [scores(kaug_w_ref, qwin_ref, c, jw, nwt) for c in chains]
    wbias = wb_ref[i - jw]

    def window_head(c, h):
        s_h = s_w[c][:, h * tq:(h + 1) * tq] + wbias
        p_h = jnp.exp2(s_h - jnp.max(s_h, axis=0, keepdims=True)).astype(BF16)
        acc = pv(vtw_ref, c, jw, nwt, p_h)
        return acc[HEAD_DIM:] * (1.0 / acc[0:1])

    nc = cmp_ref.shape[0] // n_chain
    ck = [cmp_ref[c * nc:(c + 1) * nc, :] for c in chains]
    first_c = lax.broadcasted_iota(jnp.int32, (nc, LANES), 1) < HEAD_DIM
    sc_all = [_dot_nt(jnp.where(first_c, ck[c], pltpu.roll(ck[c], HEAD_DIM, 1)).astype(BF16),
                      jnp.concatenate(qa[c], axis=0)) for c in chains]
    dist_c = ((t0 + lax.broadcasted_iota(jnp.int32, (nc, tq), 1))
              - (lax.broadcasted_iota(jnp.int32, (nc, tq), 0) * CMP_STRIDE + (CMP_BLOCK - 1)))
    valid_c = dist_c >= 0
    dist_cf = dist_c.astype(F32)
    psum, p_heads = [], []
    for c in chains:
        ps = jnp.zeros((nc, tq), F32)
        ph = []
        for h in range(HPG):
            slope = hconst_ref[c, HPG + h:HPG + h + 1, 0:1]
            s_c = jnp.where(valid_c, sc_all[c][:, h * tq:(h + 1) * tq] - slope * dist_cf, NEG)
            m_c = jnp.max(s_c, axis=0, keepdims=True)
            e_c = jnp.where(valid_c, jnp.exp2(s_c - m_c), 0.0)
            den = jnp.sum(e_c, axis=0, keepdims=True)
            p_c = e_c / jnp.maximum(den, 1e-30)
            ps = ps + p_c
            ph.append(p_c.astype(BF16))
        psum.append(ps)
        p_heads.append(ph)
    o_cmp_t = [_dot(ck[c].T.astype(BF16), jnp.concatenate(p_heads[c], axis=1))[HEAD_DIM:] for c in chains]

    cmt = cmapt_ref[...]
    jp = cmt.shape[0]
    imp_t = []
    for c in chains:
        p_hi, p_lo = _split_bf16(psum[c])
        imp_t.append(_dot(cmt, p_hi) + _dot(cmt, p_lo))
    jb = lax.broadcasted_iota(jnp.int32, (jp, tq), 0)
    cur = jnp.right_shift(t0 + lax.broadcasted_iota(jnp.int32, (jp, tq), 1), SLC_BLOCK.bit_length() - 1)
    causal_j = jb <= cur
    forced = (jb == 0) | (jb == cur) | (jb == cur - 1)
    val = [jnp.where(causal_j & forced, BIG, jnp.where(causal_j, imp_t[c], -BIG)) for c in chains]
    rank = [jnp.zeros((jp, tq), jnp.int32) for c in chains]
    o_win_heads = [[] for c in chains]
    for j in range(n_slc):
        if j % (n_slc // HPG) == 0:
            for c in chains:
                o_win_heads[c].append(window_head(c, j // (n_slc // HPG)))
        for c in chains:
            row = val[c][j:j + 1, :]
            beats = (row > val[c]) | ((row == val[c]) & (jb > j))
            rank[c] = rank[c] + beats.astype(jnp.int32)
    o_win_t = [jnp.concatenate(o_win_heads[c], axis=1) for c in chains]

    for c in chains:
        mb_t = jnp.where(causal_j & (rank[c] < n_sel), 0.0, NEG)
        if jp < LANES:
            mb_t = jnp.concatenate([mb_t, jnp.zeros((LANES - jp, tq), F32)], axis=0)
        mb = mb_t.T
        for h in range(HPG):
            qsel_ref[c, h * tq:(h + 1) * tq, 0:LANES] = qa[c][h]
            qsel_ref[c, h * tq:(h + 1) * tq, LANES:2 * LANES] = (mb + hconst_ref[c, h:h + 1, :]).astype(BF16)
    pair = i // 2
    sbias = heads(sb_ref[i - 2 * pair])
    s_d = [scores(kaug_s_ref, qsel_ref, c, 2 * pair, 2) + sbias for c in chains]
    m_d = [jnp.max(s_d[c], axis=0, keepdims=True) for c in chains]
    p_d = [jnp.exp2(s_d[c] - m_d[c]).astype(BF16) for c in chains]
    acc_d = [pv(vts_ref, c, 2 * pair, 2, p_d[c]) for c in chains]

    def sel_body(jj, carry):
        j0 = 2 * (pair - 1 - jj)
        s = [scores(kaug_s_ref, qsel_ref, c, j0, 2) for c in chains]
        m_new = [jnp.maximum(carry[c][0], jnp.max(s[c], axis=0, keepdims=True)) for c in chains]
        p = [jnp.exp2(s[c] - m_new[c]).astype(BF16) for c in chains]
        acc = [jnp.exp2(carry[c][0] - m_new[c]) * carry[c][1] + pv(vts_ref, c, j0, 2, p[c]) for c in chains]
        return tuple((m_new[c], acc[c]) for c in chains)

    res = lax.fori_loop(0, pair, sel_body, tuple((m_d[c], acc_d[c]) for c in chains))

    for c in chains:
        acc_s = res[c][1]
        o_sel_t = acc_s[:HEAD_DIM] * (1.0 / acc_s[HEAD_DIM:HEAD_DIM + 1])
        gt_t = gate_ref[:, c * LANES:(c + 1) * LANES].T
        rows = []
        for h in range(HPG):
            sl = slice(h * tq, (h + 1) * tq)
            rows.append(gt_t[h:h + 1] * o_cmp_t[c][:, sl] + gt_t[HPG + h:HPG + h + 1] * o_sel_t[:, sl]
                        + gt_t[2 * HPG + h:2 * HPG + h + 1] * o_win_t[c][:, sl])
        o_ref[:, c * gw:(c + 1) * gw] = jnp.concatenate(rows, axis=0).T.astype(o_ref.dtype)


def _nsa_attn(main, cmp, gates, kpos, cmapt, hconst, wbias, sbias, tq):
    b, s, _ = main.shape
    assert (s // tq) % 2 == 0 and N_GROUPS % NSA_CHAINS == 0
    nch = NSA_CHAINS
    nc = cmp.shape[1] // N_GROUPS
    n_slc = s // SLC_BLOCK
    gw = HPG * HEAD_DIM
    per_g = (gw + 2 * LANES) // LANES

    def q_spec(c):
        return pl.BlockSpec((None, tq, gw), lambda bi, gp, i: (bi, i, (gp * nch + c) * per_g // 2))

    def kv_spec(c, off):
        return pl.BlockSpec((None, s, LANES), lambda bi, gp, i: (bi, 0, (gp * nch + c) * per_g + off))

    kern = functools.partial(_nsa_attn_kernel, tq=tq, n_slc=n_slc, n_sel=min(N_SLC_BLOCKS, n_slc), n_chain=nch)
    return pl.pallas_call(
        kern,
        grid=(b, N_GROUPS // nch, s // tq),
        in_specs=(
            [q_spec(c) for c in range(nch)]
            + [kv_spec(c, 2) for c in range(nch)]
            + [kv_spec(c, 3) for c in range(nch)]
            + [
                pl.BlockSpec((None, nch * nc, LANES), lambda bi, gp, i: (bi, gp, 0)),
                pl.BlockSpec((None, tq, nch * LANES), lambda bi, gp, i: (bi, i, gp)),
                _const_spec(kpos.shape),
                _const_spec(cmapt.shape),
                pl.BlockSpec((nch, 2 * HPG, LANES), lambda bi, gp, i: (gp, 0, 0)),
                _const_spec(wbias.shape),
                _const_spec(sbias.shape),
            ]),
        out_specs=pl.BlockSpec((None, tq, nch * gw), lambda bi, gp, i: (bi, i, gp)),
        out_shape=jax.ShapeDtypeStruct((b, s, N_GROUPS * gw), BF16),
        scratch_shapes=[
            pltpu.VMEM((nch, s, 2 * LANES), BF16),
            pltpu.VMEM((nch, s, 2 * LANES), BF16),
            pltpu.VMEM((nch, s // tq, LANES, tq), BF16),
            pltpu.VMEM((nch, s // tq, LANES, tq), BF16),
            pltpu.VMEM((nch, HPG * tq, 2 * LANES), BF16),
            pltpu.VMEM((nch, HPG * tq, 2 * LANES), BF16),
        ],
        compiler_params=_params("parallel", "parallel", "arbitrary"),
        name="nsa_attn",
    )(*([main] * (3 * nch)), cmp, gates, kpos, cmapt, hconst, wbias, sbias)


def _sb_attn_kernel(q_ref, k_ref, v_ref, tri_ref, o_ref, vt_ref, acc_ref, *, tq):
    i = pl.program_id(2)
    tk = tq
    n_kt = k_ref.shape[0] // tk
    n_chain = q_ref.shape[1] // LANES

    @pl.when(i == 0)
    def _():
        for c in range(n_chain):
            for j in range(n_kt):
                vt_ref[c, j] = v_ref[j * tk:(j + 1) * tk, c * LANES:(c + 1) * LANES].astype(F32).T.astype(BF16)

    low = lax.broadcasted_iota(jnp.int32, (tq, LANES), 1) < HEAD_DIM
    before = (lax.broadcasted_iota(jnp.int32, (tk, 2 * tq), 0)
              < (lax.broadcasted_iota(jnp.int32, (tk, 2 * tq), 1) & (tq - 1)))
    tri = tri_ref[...]

    qst = []
    for c in range(n_chain):
        q2 = q_ref[:, c * LANES:(c + 1) * LANES].astype(F32) * (HEAD_DIM ** -0.5)
        qst.append(jnp.concatenate([jnp.where(low, q2, 0.0), jnp.where(low, 0.0, q2)], axis=0).astype(BF16))

    chains = range(n_chain)

    def tiles(j, diag, carry):
        off = pl.multiple_of(j * tk, tk)
        z = [_dot_nt(k_ref[pl.ds(off, tk), c * LANES:(c + 1) * LANES], qst[c]) for c in chains]
        sp = [jnp.maximum(z[c], 0.0) + jnp.log(1.0 + jnp.exp(_neg_abs(z[c]))) for c in chains]
        if diag:
            sp = [jnp.where(before, sp[c], 0.0) for c in chains]
        sums = [_dot(tri, sp[c].astype(BF16)) for c in chains]
        if diag:
            a = [jnp.where(before, jnp.exp(z[c] - sp[c] + sums[c][:tk]), 0.0) for c in chains]
        else:
            a = [jnp.exp(z[c] - sp[c] + sums[c][:tk] + carry[c]) for c in chains]
        for c in chains:
            pv = _dot(vt_ref[c, j], a[c].astype(BF16))
            if diag:
                acc_ref[c] = pv
            else:
                acc_ref[c] += pv
        if diag:
            return tuple(sums[c][tk:tk + 1] for c in chains)
        return tuple(carry[c] + sums[c][tk:tk + 1] for c in chains)

    res = lax.fori_loop(0, i, lambda jj, carry: tiles(i - 1 - jj, False, carry), tiles(i, True, None))
    for c in range(n_chain):
        acc = acc_ref[c]
        out_t = jnp.concatenate([acc[:HEAD_DIM, :tq], acc[HEAD_DIM:, tq:]], axis=0)
        o_ref[:, c * LANES:(c + 1) * LANES] = out_t.T.astype(o_ref.dtype)


def _sb_attn(qkv, tri, tq):
    b, s, n3 = qkv.shape
    w = LANES * SB_CHAINS
    nblk = n3 // 3 // w
    return pl.pallas_call(
        functools.partial(_sb_attn_kernel, tq=tq),
        grid=(b, nblk, s // tq),
        in_specs=[
            pl.BlockSpec((None, tq, w), lambda bi, p, i: (bi, i, p)),
            pl.BlockSpec((None, s, w), lambda bi, p, i: (bi, 0, nblk + p)),
            pl.BlockSpec((None, s, w), lambda bi, p, i: (bi, 0, 2 * nblk + p)),
            _const_spec(tri.shape),
        ],
        out_specs=pl.BlockSpec((None, tq, w), lambda bi, p, i: (bi, i, p)),
        out_shape=jax.ShapeDtypeStruct((b, s, nblk * w), BF16),
        scratch_shapes=[pltpu.VMEM((SB_CHAINS, s // tq, LANES, tq), BF16),
                        pltpu.VMEM((SB_CHAINS, LANES, 2 * tq), F32)],
        compiler_params=_params("parallel", "parallel", "arbitrary"),
        name="sb_attn",
    )(qkv, qkv, qkv, tri)


def _mix_ffn_kernel(h_ref, o_ref, wo_ref, g_ref, w1_ref, w3_ref, w2_ref, fg_ref, out_ref, a_ref, acc_ref,
                    *, final):
    h1 = h_ref[...] + _dot(o_ref[...], wo_ref[...])
    a_ref[...] = _rmsnorm(h1, g_ref[...]).astype(BF16)
    acc_ref[...] = h1

    for c in range(w1_ref.shape[0]):
        a = a_ref[...]
        u = _dot(a, w1_ref[c])
        v = _dot(a, w3_ref[c])
        t = (u * jax.nn.sigmoid(u)) * v
        acc_ref[...] += _dot(t.astype(BF16), w2_ref[c])
    h2 = acc_ref[...]
    out_ref[...] = _rmsnorm(h2, fg_ref[...]) if final else h2


def _mix_ffn(h, o, wo, g, w1, w3, w2, fg, tm, final):
    r, d = h.shape
    single = pl.Buffered(1)

    def wspec(shape):
        nd = len(shape)
        return pl.BlockSpec(shape, lambda *_: (0,) * nd, pipeline_mode=single)

    return pl.pallas_call(
        functools.partial(_mix_ffn_kernel, final=final),
        grid=(r // tm,),
        in_specs=[
            pl.BlockSpec((tm, d), lambda i: (i, 0)),
            pl.BlockSpec((tm, d), lambda i: (i, 0)),
            wspec(wo.shape),
            _const_spec((1, d)),
            wspec(w1.shape),
            wspec(w3.shape),
            wspec(w2.shape),
            _const_spec((1, d)),
        ],
        out_specs=pl.BlockSpec((tm, d), lambda i: (i, 0)),
        out_shape=jax.ShapeDtypeStruct((r, d), F32),
        scratch_shapes=[pltpu.VMEM((tm, d), BF16), pltpu.VMEM((tm, d), F32)],
        compiler_params=_params("parallel"),
        name="mix_ffn",
    )(h, o, wo, g, w1, w3, w2, fg)


def _nsa_layout(d):
    kvw = N_GROUPS * HEAD_DIM
    off = {"q": 0, "kc": d, "vc": d + kvw, "ksl": d + 2 * kvw, "vsl": d + 3 * kvw, "kw": d + 4 * kvw,
           "vw": d + 5 * kvw, "gt": d + 6 * kvw}
    main = []
    for g in range(N_GROUPS):
        main.append(np.arange(HPG * HEAD_DIM) + g * HPG * HEAD_DIM)
        for name in ("ksl", "kw", "vsl", "vw"):
            main.append(off[name] + g * HEAD_DIM + np.arange(HEAD_DIM))
    main = np.concatenate(main)
    cv = off["kc"] + np.arange(2 * kvw)
    gate = np.full((N_GROUPS * LANES,), -1, np.int64)
    for g in range(N_GROUPS):
        for br in range(N_BRANCH):
            for hh in range(HPG):
                gate[g * LANES + br * HPG + hh] = br * N_HEADS + g * HPG + hh
    return main, cv, gate, off["gt"]


def _bf16_pieces(x):
    x = np.asarray(x, np.float32)
    out = []
    for _ in range(3):
        p = x.astype(BF16).astype(np.float32)
        out.append(p)
        x = (x - p).astype(np.float32)
    return out


def _nsa_tables(s, tk):
    n_slc = s // SLC_BLOCK
    n_cmp = s // CMP_STRIDE - CMP_BLOCK // CMP_STRIDE + 1
    nc = s // CMP_STRIDE
    pos = np.arange(s)
    kpos = np.zeros((s, LANES), np.float32)
    kpos[pos, pos // SLC_BLOCK] = 1.0
    for c in range(3):
        kpos[:, POS_LANE + c] = (pos // tk) * tk
        kpos[:, POS_LANE + 3 + c] = pos % tk
    cs = np.arange(n_cmp) * CMP_STRIDE
    ss = np.arange(n_slc) * SLC_BLOCK
    ov = np.clip(np.minimum(cs[:, None] + CMP_BLOCK, ss[None] + SLC_BLOCK) - np.maximum(cs[:, None], ss[None]), 0, None)
    jp = -(-n_slc // 16) * 16
    cmapt = np.zeros((jp, nc), np.float32)
    cmapt[:n_slc, :n_cmp] = (ov / CMP_STRIDE).T
    slopes = np.asarray(2.0 ** (-8.0 * (np.arange(N_HEADS) + 1) / N_HEADS), np.float32)
    hconst = np.zeros((N_GROUPS, 2 * HPG, LANES), np.float32)
    slopes = (slopes.astype(np.float64) * LOG2E).astype(np.float32)
    pieces = _bf16_pieces(slopes)
    for g in range(N_GROUPS):
        for hh in range(HPG):
            for c in range(3):
                hconst[g, hh, POS_LANE + c] = pieces[c][g * HPG + hh]
                hconst[g, hh, POS_LANE + 3 + c] = pieces[c][g * HPG + hh]
            hconst[g, HPG + hh, :] = slopes[g * HPG + hh]
    n_kt = s // tk
    nwt = min(WINDOW // tk + 1, n_kt)
    tt = np.arange(tk)[None, :]

    def dist(r, nt):
        return r * tk + tt - np.arange(nt * tk)[:, None]

    wbias = np.stack([np.where((dist(r, nwt) >= 0) & (dist(r, nwt) < WINDOW), 0.0, NEG) for r in range(nwt)])
    sbias = np.stack([np.where(dist(r, 2) >= 0, 0.0, NEG) for r in range(2)])
    return (jnp.asarray(kpos, BF16), jnp.asarray(cmapt, BF16), jnp.asarray(hconst, F32),
            jnp.asarray(wbias, F32), jnp.asarray(sbias, F32))


def _ffn_chunk(d_ff):
    for tf in (512, 256, 128):
        if d_ff % tf == 0:
            return tf
    return d_ff


def kernel(x, mix_norm_g, ffn_norm_g, final_norm_g, nsa_w_in, nsa_gate_b, nsa_cmp_pe, nsa_cmp_w1, nsa_cmp_w2,
           nsa_w_out, sb_w_in, sb_w_out, ffn_w1, ffn_w3, ffn_w2):
    b, s, d = x.shape
    depth = mix_norm_g.shape[0]
    d_ff = ffn_w1.shape[2]
    tm = min(512, s)
    tq = min(256, s)
    tf = _ffn_chunk(d_ff)
    nch = d_ff // tf

    main_idx, cv_idx, gate_idx, gt_off = _nsa_layout(d)
    gate_valid = jnp.asarray(gate_idx >= 0)
    gate_cols = jnp.asarray(np.maximum(gate_idx, 0))
    kpos, cmapt, hconst, wbias, sbias = _nsa_tables(s, tq)
    tri = jnp.asarray(np.concatenate([-np.triu(np.ones((tq, tq), np.float32), 1),
                                      -np.ones((16, tq), np.float32)], axis=0), BF16)
    chunk = CMP_STRIDE * HEAD_DIM
    nc = s // CMP_STRIDE

    h = x
    for i in range(depth):
        j = i // 2
        gm = mix_norm_g[i].reshape(1, d)
        if i % 2 == 0:
            w_in = nsa_w_in[j]
            w_main = w_in[:, main_idx].astype(BF16)
            w_cv = w_in[:, cv_idx].astype(BF16)
            w_gate = jnp.where(gate_valid[None, :], w_in[:, gt_off + gate_cols], 0.0).astype(BF16)
            gate_b = jnp.where(gate_valid, nsa_gate_b[j][gate_cols], 0.0).reshape(1, -1)
            main, cv, gates = _nsa_proj(h, gm, w_main, w_cv, w_gate, gate_b, tm)
            c = cv.reshape(b, 2, N_GROUPS * nc, chunk)
            pe = nsa_cmp_pe[j].reshape(2, 2, chunk)
            w2pad = jnp.zeros((2, nsa_cmp_w2.shape[2], LANES), F32)
            w2pad = w2pad.at[0, :, :HEAD_DIM].set(nsa_cmp_w2[j, 0]).at[1, :, HEAD_DIM:].set(nsa_cmp_w2[j, 1])
            cmp = _nsa_compress(c, pe, nsa_cmp_w1[j].astype(BF16), w2pad.astype(BF16))
            o = _nsa_attn(main, cmp, gates, kpos, cmapt, hconst, wbias, sbias, tq)
            w_out = nsa_w_out[j]
        else:
            qkv = _sb_proj(h, gm, sb_w_in[j].astype(BF16), tm)
            o = _sb_attn(qkv, tri, tq)
            w_out = sb_w_out[j]
        w1 = ffn_w1[i].reshape(d, nch, tf).transpose(1, 0, 2).astype(BF16)
        w3 = ffn_w3[i].reshape(d, nch, tf).transpose(1, 0, 2).astype(BF16)
        w2 = ffn_w2[i].reshape(nch, tf, d).astype(BF16)
        final = i == depth - 1
        h = _mix_ffn(h.reshape(b * s, d), o.reshape(b * s, d), w_out.astype(BF16), ffn_norm_g[i].reshape(1, d),
                     w1, w3, w2, final_norm_g.reshape(1, d), tm, final).reshape(b, s, d)
    return h
```

```python
import functools

import numpy as np
import jax
import jax.numpy as jnp
from jax import lax
from jax.experimental import pallas as pl
from jax.experimental.pallas import tpu as pltpu

F32 = jnp.float32
BF16 = jnp.bfloat16

RMS_EPS = 1e-6
NEG = -1e30
BIG = 1e30

N_HEADS = 16
HEAD_DIM = 64
N_GROUPS = 4
HPG = N_HEADS // N_GROUPS
N_BRANCH = 3
CMP_BLOCK = 32
CMP_STRIDE = 16
SLC_BLOCK = 64
N_SLC_BLOCKS = 8
WINDOW = 512

LANES = 128
POS_LANE = 32
LOG2E = 1.4426950408889634
VMEM_LIMIT = 56 * 1024 * 1024
SB_CHAINS = 8
NSA_CHAINS = 4

_NT = (((1,), (1,)), ((), ()))


def _dot(a, b):
    return jnp.dot(a, b, preferred_element_type=F32)


def _dot_nt(a, b):
    return lax.dot_general(a, b, _NT, preferred_element_type=F32)


def _rmsnorm(x, g):
    ms = jnp.mean(x * x, axis=-1, keepdims=True)
    return (x * lax.rsqrt(ms + RMS_EPS)) * g


def _split_bf16(x):
    hi = x.astype(BF16)
    lo = (x - hi.astype(F32)).astype(BF16)
    return hi, lo


def _neg_abs(x):
    return lax.bitcast_convert_type(lax.bitcast_convert_type(x, jnp.uint32) | jnp.uint32(0x80000000), F32)


def _params(*sem):
    return pltpu.CompilerParams(dimension_semantics=sem, vmem_limit_bytes=VMEM_LIMIT)


def _const_spec(shape):
    nd = len(shape)
    return pl.BlockSpec(shape, lambda *_: (0,) * nd)


def _nsa_proj_kernel(x_ref, g_ref, wm_ref, wc_ref, wg_ref, gb_ref, main_ref, cv_ref, gate_ref):
    a = _rmsnorm(x_ref[...], g_ref[...]).astype(BF16)
    main_ref[...] = _dot(a, wm_ref[...]).astype(main_ref.dtype)
    cv = _dot(a, wc_ref[...])
    for kv in range(2):
        for gp in range(N_GROUPS // 2):
            c0 = (kv * N_GROUPS + 2 * gp) * HEAD_DIM
            cv_ref[kv, gp, :, :] = cv[:, c0:c0 + LANES]
    gate_ref[...] = jax.nn.sigmoid(_dot(a, wg_ref[...]) + gb_ref[...])


def _nsa_proj(h, g, w_main, w_cv, w_gate, gate_b, tm):
    b, s, d = h.shape
    nm, ng = w_main.shape[1], w_gate.shape[1]
    return pl.pallas_call(
        _nsa_proj_kernel,
        grid=(b, s // tm),
        in_specs=[
            pl.BlockSpec((None, tm, d), lambda bi, r: (bi, r, 0)),
            _const_spec((1, d)),
            _const_spec(w_main.shape),
            _const_spec(w_cv.shape),
            _const_spec(w_gate.shape),
            _const_spec((1, ng)),
        ],
        out_specs=[
            pl.BlockSpec((None, tm, nm), lambda bi, r: (bi, r, 0)),
            pl.BlockSpec((None, 2, N_GROUPS // 2, tm, LANES), lambda bi, r: (bi, 0, 0, r, 0)),
            pl.BlockSpec((None, tm, ng), lambda bi, r: (bi, r, 0)),
        ],
        out_shape=[
            jax.ShapeDtypeStruct((b, s, nm), BF16),
            jax.ShapeDtypeStruct((b, 2, N_GROUPS // 2, s, LANES), F32),
            jax.ShapeDtypeStruct((b, s, ng), F32),
        ],
        compiler_params=_params("parallel", "parallel"),
        name="nsa_proj",
    )(h, g, w_main, w_cv, w_gate, gate_b)


def _sb_proj_kernel(x_ref, g_ref, w_ref, o_ref):
    a = _rmsnorm(x_ref[...], g_ref[...]).astype(BF16)
    o_ref[...] = _dot(a, w_ref[...]).astype(o_ref.dtype)


def _sb_proj(h, g, w, tm):
    b, s, d = h.shape
    n = w.shape[1]
    return pl.pallas_call(
        _sb_proj_kernel,
        grid=(b, s // tm),
        in_specs=[
            pl.BlockSpec((None, tm, d), lambda bi, r: (bi, r, 0)),
            _const_spec((1, d)),
            _const_spec(w.shape),
        ],
        out_specs=pl.BlockSpec((None, tm, n), lambda bi, r: (bi, r, 0)),
        out_shape=jax.ShapeDtypeStruct((b, s, n), BF16),
        compiler_params=_params("parallel", "parallel"),
        name="sb_proj",
    )(h, g, w)


def _nsa_compress_kernel(c_ref, pe_ref, w1_ref, w2_ref, o_ref):
    rows = c_ref.shape[1]
    nc = rows // (N_GROUPS // 2)
    out = [jnp.zeros((rows, LANES), F32) for _ in range(2)]
    for kv in range(2):
        c = c_ref[kv]
        top = (c + pe_ref[kv, 0:1, :]).astype(BF16)
        bot = (c + pe_ref[kv, 1:2, :]).astype(BF16)
        for e in range(2):
            a_top = _dot(top, w1_ref[kv, 0, e])
            a_bot = _dot(bot, w1_ref[kv, 1, e])
            hid = jax.nn.gelu(a_top + pltpu.roll(a_bot, rows - 1, 0), approximate=True)
            out[e] = out[e] + _dot(hid.astype(BF16), w2_ref[kv])
    for gp in range(N_GROUPS // 2):
        for e in range(2):
            g = 2 * gp + e
            o_ref[g * nc:(g + 1) * nc, :] = out[e][gp * nc:(gp + 1) * nc]


def _nsa_compress(c, pe, w1, w2pad):
    b, _, rows, width = c.shape
    return pl.pallas_call(
        _nsa_compress_kernel,
        grid=(b,),
        in_specs=[
            pl.BlockSpec((None, 2, rows, width), lambda bi: (bi, 0, 0, 0)),
            _const_spec(pe.shape),
            _const_spec(w1.shape),
            _const_spec(w2pad.shape),
        ],
        out_specs=pl.BlockSpec((None, 2 * rows, LANES), lambda bi: (bi, 0, 0)),
        out_shape=jax.ShapeDtypeStruct((b, 2 * rows, LANES), F32),
        compiler_params=_params("parallel"),
        name="nsa_compress",
    )(c, pe, w1, w2pad)


def _nsa_attn_kernel(*refs, tq, n_slc, n_sel, n_chain):
    q_refs, kk_refs, vv_refs = refs[0:n_chain], refs[n_chain:2 * n_chain], refs[2 * n_chain:3 * n_chain]
    cmp_ref, gate_ref, kpos_ref, cmapt_ref, hconst_ref, wb_ref, sb_ref, o_ref = refs[3 * n_chain:3 * n_chain + 8]
    kaug_s_ref, kaug_w_ref, vts_ref, vtw_ref, qsel_ref, qwin_ref = refs[3 * n_chain + 8:]
    chains = range(n_chain)
    i = pl.program_id(2)
    t0 = i * tq
    tk = tq
    n_kt = kaug_s_ref.shape[1] // tk
    gw = HPG * HEAD_DIM

    @pl.when(i == 0)
    def _():
        top = lax.broadcasted_iota(jnp.int32, (LANES, tk), 0) < HEAD_DIM
        for c in chains:
            kk = kk_refs[c][...].astype(F32)
            kk_sw = pltpu.roll(kk, HEAD_DIM, 1)
            first = lax.broadcasted_iota(jnp.int32, kk.shape, 1) < HEAD_DIM
            kaug_s_ref[c, :, 0:LANES] = jnp.where(first, kk, kk_sw).astype(BF16)
            kaug_w_ref[c, :, 0:LANES] = jnp.where(first, kk_sw, kk).astype(BF16)
            kaug_s_ref[c, :, LANES:2 * LANES] = kpos_ref[...]
            kaug_w_ref[c, :, LANES:2 * LANES] = kpos_ref[...]
            for j in range(n_kt):
                vt = vv_refs[c][j * tk:(j + 1) * tk, :].astype(F32).T
                vts_ref[c, j] = jnp.where(top, vt, 1.0).astype(BF16)
                vtw_ref[c, j] = jnp.where(top, 1.0, vt).astype(BF16)

    low = lax.broadcasted_iota(jnp.int32, (tq, LANES), 1) < HEAD_DIM

    qa = []
    for c in chains:
        q4 = q_refs[c][...].astype(F32) * (HEAD_DIM ** -0.5 * LOG2E)
        q4_hi = q4.astype(BF16).astype(F32)
        q4_lo = q4 - q4_hi
        per_head = []
        for h in range(HPG):
            cols = slice(LANES * (h // 2), LANES * (h // 2 + 1))
            if h % 2 == 0:
                per_head.append(jnp.where(low, q4_hi[:, cols], pltpu.roll(q4_lo[:, cols], HEAD_DIM, 1)).astype(BF16))
            else:
                per_head.append(jnp.where(low, pltpu.roll(q4_hi[:, cols], HEAD_DIM, 1), q4_lo[:, cols]).astype(BF16))
        qa.append(per_head)

    def scores(kref, qref, c, j0, nt):
        kt = kref[c, pl.ds(pl.multiple_of(j0 * tk, tk), nt * tk), :]
        return _dot_nt(kt, qref[c])

    def pv(vt_ref, c, j0, nt, p):
        acc = _dot(vt_ref[c, j0], p[0:tk])
        for u in range(1, nt):
            acc = acc + _dot(vt_ref[c, j0 + u], p[u * tk:(u + 1) * tk])
        return acc

    def heads(bias):
        return jnp.concatenate([bias] * HPG, axis=1)

    for c in chains:
        for h in range(HPG):
            qwin_ref[c, h * tq:(h + 1) * tq, 0:LANES] = qa[c][h]
            qwin_ref[c, h * tq:(h + 1) * tq, LANES:2 * LANES] = jnp.broadcast_to(hconst_ref[c, h:h + 1, :],
                                                                                 (tq, LANES)).astype(BF16)
    nwt = min(WINDOW // tk + 1, n_kt)
    jw = jnp.clip(i - (nwt - 1), 0, n_kt - nwt)
    s_w = [scores(kaug_w_ref, qwin_ref, c, jw, nwt) for c in chains]
    wbias = wb_ref[i - jw]

    def window_head(c, h):
        s_h = s_w[c][:, h * tq:(h + 1) * tq] + wbias
        p_h = jnp.exp2(s_h - jnp.max(s_h, axis=0, keepdims=True)).astype(BF16)
        acc = pv(vtw_ref, c, jw, nwt, p_h)
        return acc[HEAD_DIM:] * (1.0 / acc[0:1])

    nc = cmp_ref.shape[0] // n_chain
    ck = [cmp_ref[c * nc:(c + 1) * nc, :] for c in chains]
    first_c = lax.broadcasted_iota(jnp.int32, (nc, LANES), 1) < HEAD_DIM
    sc_all = [_dot_nt(jnp.where(first_c, ck[c], pltpu.roll(ck[c], HEAD_DIM, 1)).astype(BF16),
                      jnp.concatenate(qa[c], axis=0)) for c in chains]
    dist_c = ((t0 + lax.broadcasted_iota(jnp.int32, (nc, tq), 1))
              - (lax.broadcasted_iota(jnp.int32, (nc, tq), 0) * CMP_STRIDE + (CMP_BLOCK - 1)))
    valid_c = dist_c >= 0
    dist_cf = dist_c.astype(F32)
    psum, p_heads = [], []
    for c in chains:
        ps = jnp.zeros((nc, tq), F32)
        ph = []
        for h in range(HPG):
            slope = hconst_ref[c, HPG + h:HPG + h + 1, 0:1]
            s_c = jnp.where(valid_c, sc_all[c][:, h * tq:(h + 1) * tq] - slope * dist_cf, NEG)
            m_c = jnp.max(s_c, axis=0, keepdims=True)
            e_c = jnp.where(valid_c, jnp.exp2(s_c - m_c), 0.0)
            den = jnp.sum(e_c, axis=0, keepdims=True)
            p_c = e_c / jnp.maximum(den, 1e-30)
            ps = ps + p_c
            ph.append(p_c.astype(BF16))
        psum.append(ps)
        p_heads.append(ph)
    o_cmp_t = [_dot(ck[c].T.astype(BF16), jnp.concatenate(p_heads[c], axis=1))[HEAD_DIM:] for c in chains]

    cmt = cmapt_ref[...]
    jp = cmt.shape[0]
    imp_t = []
    for c in chains:
        p_hi, p_lo = _split_bf16(psum[c])
        imp_t.append(_dot(cmt, p_hi) + _dot(cmt, p_lo))
    jb = lax.broadcasted_iota(jnp.int32, (jp, tq), 0)
    cur = jnp.right_shift(t0 + lax.broadcasted_iota(jnp.int32, (jp, tq), 1), SLC_BLOCK.bit_length() - 1)
    causal_j = jb <= cur
    forced = (jb == 0) | (jb == cur) | (jb == cur - 1)
    val = [jnp.where(causal_j & forced, BIG, jnp.where(causal_j, imp_t[c], -BIG)) for c in chains]
    rank = [jnp.zeros((jp, tq), jnp.int32) for c in chains]
    o_win_heads = [[] for c in chains]
    for j in range(n_slc):
        if j % (n_slc // HPG) == 0:
            for c in chains:
                o_win_heads[c].append(window_head(c, j // (n_slc // HPG)))
        for c in chains:
            row = val[c][j:j + 1, :]
            beats = (row > val[c]) | ((row == val[c]) & (jb > j))
            rank[c] = rank[c] + beats.astype(jnp.int32)
    o_win_t = [jnp.concatenate(o_win_heads[c], axis=1) for c in chains]

    for c in chains:
        mb_t = jnp.where(causal_j & (rank[c] < n_sel), 0.0, NEG)
        if jp < LANES:
            mb_t = jnp.concatenate([mb_t, jnp.zeros((LANES - jp, tq), F32)], axis=0)
        mb = mb_t.T
        for h in range(HPG):
            qsel_ref[c, h * tq:(h + 1) * tq, 0:LANES] = qa[c][h]
            qsel_ref[c, h * tq:(h + 1) * tq, LANES:2 * LANES] = (mb + hconst_ref[c, h:h + 1, :]).astype(BF16)
    pair = i // 2
    sbias = heads(sb_ref[i - 2 * pair])
    s_d = [scores(kaug_s_ref, qsel_ref, c, 2 * pair, 2) + sbias for c in chains]
    m_d = [jnp.max(s_d[c], axis=0, keepdims=True) for c in chains]
    p_d = [jnp.exp2(s_d[c] - m_d[c]).astype(BF16) for c in chains]
    acc_d = [pv(vts_ref, c, 2 * pair, 2, p_d[c]) for c in chains]

    def sel_body(jj, carry):
        j0 = 2 * (pair - 1 - jj)
        s = [scores(kaug_s_ref, qsel_ref, c, j0, 2) for c in chains]
        m_new = [jnp.maximum(carry[c][0], jnp.max(s[c], axis=0, keepdims=True)) for c in chains]
        p = [jnp.exp2(s[c] - m_new[c]).astype(BF16) for c in chains]
        acc = [jnp.exp2(carry[c][0] - m_new[c]) * carry[c][1] + pv(vts_ref, c, j0, 2, p[c]) for c in chains]
        return tuple((m_new[c], acc[c]) for c in chains)

    res = lax.fori_loop(0, pair, sel_body, tuple((m_d[c], acc_d[c]) for c in chains))

    gt_t = gate_ref[...].T
    for c in chains:
        acc_s = res[c][1]
        o_sel_t = acc_s[:HEAD_DIM] * (1.0 / acc_s[HEAD_DIM:HEAD_DIM + 1])
        rows = []
        for h in range(HPG):
            sl = slice(h * tq, (h + 1) * tq)
            g_cmp, g_sel, g_win = (gt_t[br * N_HEADS + c * HPG + h:br * N_HEADS + c * HPG + h + 1]
                                   for br in range(N_BRANCH))
            rows.append(g_cmp * o_cmp_t[c][:, sl] + g_sel * o_sel_t[:, sl] + g_win * o_win_t[c][:, sl])
        o_ref[:, c * gw:(c + 1) * gw] = jnp.concatenate(rows, axis=0).T.astype(o_ref.dtype)


def _nsa_attn(main, cmp, gates, kpos, cmapt, hconst, wbias, sbias, tq):
    b, s, _ = main.shape
    assert (s // tq) % 2 == 0 and NSA_CHAINS == N_GROUPS
    nch = NSA_CHAINS
    nc = cmp.shape[1] // N_GROUPS
    n_slc = s // SLC_BLOCK
    gw = HPG * HEAD_DIM
    per_g = (gw + 2 * LANES) // LANES

    def q_spec(c):
        return pl.BlockSpec((None, tq, gw), lambda bi, gp, i: (bi, i, (gp * nch + c) * per_g // 2))

    def kv_spec(c, off):
        return pl.BlockSpec((None, s, LANES), lambda bi, gp, i: (bi, 0, (gp * nch + c) * per_g + off))

    kern = functools.partial(_nsa_attn_kernel, tq=tq, n_slc=n_slc, n_sel=min(N_SLC_BLOCKS, n_slc), n_chain=nch)
    return pl.pallas_call(
        kern,
        grid=(b, N_GROUPS // nch, s // tq),
        in_specs=(
            [q_spec(c) for c in range(nch)]
            + [kv_spec(c, 2) for c in range(nch)]
            + [kv_spec(c, 3) for c in range(nch)]
            + [
                pl.BlockSpec((None, nch * nc, LANES), lambda bi, gp, i: (bi, gp, 0)),
                pl.BlockSpec((None, tq, LANES), lambda bi, gp, i: (bi, i, 0)),
                _const_spec(kpos.shape),
                _const_spec(cmapt.shape),
                pl.BlockSpec((nch, 2 * HPG, LANES), lambda bi, gp, i: (gp, 0, 0)),
                _const_spec(wbias.shape),
                _const_spec(sbias.shape),
            ]),
        out_specs=pl.BlockSpec((None, tq, nch * gw), lambda bi, gp, i: (bi, i, gp)),
        out_shape=jax.ShapeDtypeStruct((b, s, N_GROUPS * gw), BF16),
        scratch_shapes=[
            pltpu.VMEM((nch, s, 2 * LANES), BF16),
            pltpu.VMEM((nch, s, 2 * LANES), BF16),
            pltpu.VMEM((nch, s // tq, LANES, tq), BF16),
            pltpu.VMEM((nch, s // tq, LANES, tq), BF16),
            pltpu.VMEM((nch, HPG * tq, 2 * LANES), BF16),
            pltpu.VMEM((nch, HPG * tq, 2 * LANES), BF16),
        ],
        compiler_params=_params("parallel", "parallel", "arbitrary"),
        name="nsa_attn",
    )(*([main] * (3 * nch)), cmp, gates, kpos, cmapt, hconst, wbias, sbias)


def _sb_attn_kernel(q_ref, k_ref, v_ref, tri_ref, o_ref, vt_ref, acc_ref, *, tq):
    i = pl.program_id(2)
    tk = tq
    n_kt = k_ref.shape[0] // tk
    n_chain = q_ref.shape[1] // LANES

    @pl.when(i == 0)
    def _():
        for c in range(n_chain):
            for j in range(n_kt):
                vt_ref[c, j] = v_ref[j * tk:(j + 1) * tk, c * LANES:(c + 1) * LANES].astype(F32).T.astype(BF16)

    low = lax.broadcasted_iota(jnp.int32, (tq, LANES), 1) < HEAD_DIM
    before = (lax.broadcasted_iota(jnp.int32, (tk, 2 * tq), 0)
              < (lax.broadcasted_iota(jnp.int32, (tk, 2 * tq), 1) & (tq - 1)))
    tri = tri_ref[...]

    qst = []
    for c in range(n_chain):
        q2 = q_ref[:, c * LANES:(c + 1) * LANES].astype(F32) * (HEAD_DIM ** -0.5)
        qst.append(jnp.concatenate([jnp.where(low, q2, 0.0), jnp.where(low, 0.0, q2)], axis=0).astype(BF16))

    chains = range(n_chain)

    def tiles(j, diag, carry):
        off = pl.multiple_of(j * tk, tk)
        z = [_dot_nt(k_ref[pl.ds(off, tk), c * LANES:(c + 1) * LANES], qst[c]) for c in chains]
        sp = [jnp.maximum(z[c], 0.0) + jnp.log(1.0 + jnp.exp(_neg_abs(z[c]))) for c in chains]
        if diag:
            sp = [jnp.where(before, sp[c], 0.0) for c in chains]
        sums = [_dot(tri, sp[c].astype(BF16)) for c in chains]
        if diag:
            a = [jnp.where(before, jnp.exp(z[c] - sp[c] + sums[c][:tk]), 0.0) for c in chains]
        else:
            a = [jnp.exp(z[c] - sp[c] + sums[c][:tk] + carry[c]) for c in chains]
        for c in chains:
            pv = _dot(vt_ref[c, j], a[c].astype(BF16))
            if diag:
                acc_ref[c] = pv
            else:
                acc_ref[c] += pv
        if diag:
            return tuple(sums[c][tk:tk + 1] for c in chains)
        return tuple(carry[c] + sums[c][tk:tk + 1] for c in chains)

    res = lax.fori_loop(0, i, lambda jj, carry: tiles(i - 1 - jj, False, carry), tiles(i, True, None))
    for c in range(n_chain):
        acc = acc_ref[c]
        out_t = jnp.concatenate([acc[:HEAD_DIM, :tq], acc[HEAD_DIM:, tq:]], axis=0)
        o_ref[:, c * LANES:(c + 1) * LANES] = out_t.T.astype(o_ref.dtype)


def _sb_attn(qkv, tri, tq):
    b, s, n3 = qkv.shape
    w = LANES * SB_CHAINS
    nblk = n3 // 3 // w
    return pl.pallas_call(
        functools.partial(_sb_attn_kernel, tq=tq),
        grid=(b, nblk, s // tq),
        in_specs=[
            pl.BlockSpec((None, tq, w), lambda bi, p, i: (bi, i, p)),
            pl.BlockSpec((None, s, w), lambda bi, p, i: (bi, 0, nblk + p)),
            pl.BlockSpec((None, s, w), lambda bi, p, i: (bi, 0, 2 * nblk + p)),
            _const_spec(tri.shape),
        ],
        out_specs=pl.BlockSpec((None, tq, w), lambda bi, p, i: (bi, i, p)),
        out_shape=jax.ShapeDtypeStruct((b, s, nblk * w), BF16),
        scratch_shapes=[pltpu.VMEM((SB_CHAINS, s // tq, LANES, tq), BF16),
                        pltpu.VMEM((SB_CHAINS, LANES, 2 * tq), F32)],
        compiler_params=_params("parallel", "parallel", "arbitrary"),
        name="sb_attn",
    )(qkv, qkv, qkv, tri)


def _mix_ffn_kernel(h_ref, o_ref, wo_ref, g_ref, w1_ref, w3_ref, w2_ref, fg_ref, out_ref, a_ref, acc_ref,
                    *, final, tf):
    h1 = h_ref[...] + _dot(o_ref[...], wo_ref[...])
    a_ref[...] = _rmsnorm(h1, g_ref[...]).astype(BF16)
    acc_ref[...] = h1

    for c in range(w1_ref.shape[1] // tf):
        cols = slice(c * tf, (c + 1) * tf)
        a = a_ref[...]
        u = _dot(a, w1_ref[:, cols])
        v = _dot(a, w3_ref[:, cols])
        t = (u * jax.nn.sigmoid(u)) * v
        acc_ref[...] += _dot(t.astype(BF16), w2_ref[cols, :])
    h2 = acc_ref[...]
    out_ref[...] = _rmsnorm(h2, fg_ref[...]) if final else h2


def _mix_ffn(h, o, wo, g, w1, w3, w2, fg, tm, tf, final):
    r, d = h.shape
    single = pl.Buffered(1)

    def wspec(shape):
        nd = len(shape)
        return pl.BlockSpec(shape, lambda *_: (0,) * nd, pipeline_mode=single)

    return pl.pallas_call(
        functools.partial(_mix_ffn_kernel, final=final, tf=tf),
        grid=(r // tm,),
        in_specs=[
            pl.BlockSpec((tm, d), lambda i: (i, 0)),
            pl.BlockSpec((tm, d), lambda i: (i, 0)),
            wspec(wo.shape),
            _const_spec((1, d)),
            wspec(w1.shape),
            wspec(w3.shape),
            wspec(w2.shape),
            _const_spec((1, d)),
        ],
        out_specs=pl.BlockSpec((tm, d), lambda i: (i, 0)),
        out_shape=jax.ShapeDtypeStruct((r, d), F32),
        scratch_shapes=[pltpu.VMEM((tm, d), BF16), pltpu.VMEM((tm, d), F32)],
        compiler_params=_params("parallel"),
        name="mix_ffn",
    )(h, o, wo, g, w1, w3, w2, fg)


def _nsa_layout(d):
    kvw = N_GROUPS * HEAD_DIM
    off = {"q": 0, "kc": d, "vc": d + kvw, "ksl": d + 2 * kvw, "vsl": d + 3 * kvw, "kw": d + 4 * kvw,
           "vw": d + 5 * kvw, "gt": d + 6 * kvw}
    main = []
    for g in range(N_GROUPS):
        main.append(np.arange(HPG * HEAD_DIM) + g * HPG * HEAD_DIM)
        for name in ("ksl", "kw", "vsl", "vw"):
            main.append(off[name] + g * HEAD_DIM + np.arange(HEAD_DIM))
    main = np.concatenate(main)
    cv = off["kc"] + np.arange(2 * kvw)
    return main, cv, off["gt"]


def _bf16_pieces(x):
    x = np.asarray(x, np.float32)
    out = []
    for _ in range(3):
        p = x.astype(BF16).astype(np.float32)
        out.append(p)
        x = (x - p).astype(np.float32)
    return out


def _nsa_tables(s, tk):
    n_slc = s // SLC_BLOCK
    n_cmp = s // CMP_STRIDE - CMP_BLOCK // CMP_STRIDE + 1
    nc = s // CMP_STRIDE
    pos = np.arange(s)
    kpos = np.zeros((s, LANES), np.float32)
    kpos[pos, pos // SLC_BLOCK] = 1.0
    for c in range(3):
        kpos[:, POS_LANE + c] = (pos // tk) * tk
        kpos[:, POS_LANE + 3 + c] = pos % tk
    cs = np.arange(n_cmp) * CMP_STRIDE
    ss = np.arange(n_slc) * SLC_BLOCK
    ov = np.clip(np.minimum(cs[:, None] + CMP_BLOCK, ss[None] + SLC_BLOCK) - np.maximum(cs[:, None], ss[None]), 0, None)
    jp = -(-n_slc // 16) * 16
    cmapt = np.zeros((jp, nc), np.float32)
    cmapt[:n_slc, :n_cmp] = (ov / CMP_STRIDE).T
    slopes = np.asarray(2.0 ** (-8.0 * (np.arange(N_HEADS) + 1) / N_HEADS), np.float32)
    hconst = np.zeros((N_GROUPS, 2 * HPG, LANES), np.float32)
    slopes = (slopes.astype(np.float64) * LOG2E).astype(np.float32)
    pieces = _bf16_pieces(slopes)
    for g in range(N_GROUPS):
        for hh in range(HPG):
            for c in range(3):
                hconst[g, hh, POS_LANE + c] = pieces[c][g * HPG + hh]
                hconst[g, hh, POS_LANE + 3 + c] = pieces[c][g * HPG + hh]
            hconst[g, HPG + hh, :] = slopes[g * HPG + hh]
    n_kt = s // tk
    nwt = min(WINDOW // tk + 1, n_kt)
    tt = np.arange(tk)[None, :]

    def dist(r, nt):
        return r * tk + tt - np.arange(nt * tk)[:, None]

    wbias = np.stack([np.where((dist(r, nwt) >= 0) & (dist(r, nwt) < WINDOW), 0.0, NEG) for r in range(nwt)])
    sbias = np.stack([np.where(dist(r, 2) >= 0, 0.0, NEG) for r in range(2)])
    return (jnp.asarray(kpos, BF16), jnp.asarray(cmapt, BF16), jnp.asarray(hconst, F32),
            jnp.asarray(wbias, F32), jnp.asarray(sbias, F32))


def _ffn_chunk(d_ff):
    for tf in (512, 256, 128):
        if d_ff % tf == 0:
            return tf
    return d_ff


def kernel(x, mix_norm_g, ffn_norm_g, final_norm_g, nsa_w_in, nsa_gate_b, nsa_cmp_pe, nsa_cmp_w1, nsa_cmp_w2,
           nsa_w_out, sb_w_in, sb_w_out, ffn_w1, ffn_w3, ffn_w2):
    b, s, d = x.shape
    depth = mix_norm_g.shape[0]
    d_ff = ffn_w1.shape[2]
    tm = min(512, s)
    tq = min(256, s)
    tf = _ffn_chunk(d_ff)

    main_idx, cv_idx, gt_off = _nsa_layout(d)
    n_gate = N_BRANCH * N_HEADS
    kpos, cmapt, hconst, wbias, sbias = _nsa_tables(s, tq)
    tri = jnp.asarray(np.concatenate([-np.triu(np.ones((tq, tq), np.float32), 1),
                                      -np.ones((16, tq), np.float32)], axis=0), BF16)
    chunk = CMP_STRIDE * HEAD_DIM
    nc = s // CMP_STRIDE

    h = x
    for i in range(depth):
        j = i // 2
        gm = mix_norm_g[i].reshape(1, d)
        if i % 2 == 0:
            w_in = nsa_w_in[j]
            w_main = w_in[:, main_idx].astype(BF16)
            w_cv = w_in[:, cv_idx].astype(BF16)
            w_gate = jnp.pad(w_in[:, gt_off:gt_off + n_gate], ((0, 0), (0, LANES - n_gate))).astype(BF16)
            gate_b = jnp.pad(nsa_gate_b[j], (0, LANES - n_gate)).reshape(1, LANES)
            main, cv, gates = _nsa_proj(h, gm, w_main, w_cv, w_gate, gate_b, tm)
            c = cv.reshape(b, 2, N_GROUPS // 2 * nc, 2 * chunk)
            pe = jnp.broadcast_to(nsa_cmp_pe[j].reshape(2, 2, CMP_STRIDE, 1, HEAD_DIM),
                                  (2, 2, CMP_STRIDE, 2, HEAD_DIM)).reshape(2, 2, 2 * chunk)
            w1 = nsa_cmp_w1[j].reshape(2, 2, CMP_STRIDE, 1, HEAD_DIM, -1)
            w1 = jnp.stack([jnp.concatenate([w1, jnp.zeros_like(w1)], axis=3),
                            jnp.concatenate([jnp.zeros_like(w1), w1], axis=3)], axis=2)
            w1 = w1.reshape(2, 2, 2, 2 * chunk, -1)
            w2pad = jnp.zeros((2, nsa_cmp_w2.shape[2], LANES), F32)
            w2pad = w2pad.at[0, :, :HEAD_DIM].set(nsa_cmp_w2[j, 0]).at[1, :, HEAD_DIM:].set(nsa_cmp_w2[j, 1])
            cmp = _nsa_compress(c, pe, w1.astype(BF16), w2pad.astype(BF16))
            o = _nsa_attn(main, cmp, gates, kpos, cmapt, hconst, wbias, sbias, tq)
            w_out = nsa_w_out[j]
        else:
            qkv = _sb_proj(h, gm, sb_w_in[j].astype(BF16), tm)
            o = _sb_attn(qkv, tri, tq)
            w_out = sb_w_out[j]
        final = i == depth - 1
        h = _mix_ffn(h.reshape(b * s, d), o.reshape(b * s, d), w_out.astype(BF16), ffn_norm_g[i].reshape(1, d),
                     ffn_w1[i].astype(BF16), ffn_w3[i].astype(BF16), ffn_w2[i].astype(BF16),
                     final_norm_g.reshape(1, d), tm, tf, final).reshape(b, s, d)
    return h
```

```python
import functools

import numpy as np
import jax
import jax.numpy as jnp
from jax import lax
from jax.experimental import pallas as pl
from jax.experimental.pallas import tpu as pltpu

F32 = jnp.float32
BF16 = jnp.bfloat16

RMS_EPS = 1e-6
NEG = -1e30
BIG = 1e30

N_HEADS = 16
HEAD_DIM = 64
N_GROUPS = 4
HPG = N_HEADS // N_GROUPS
N_BRANCH = 3
CMP_BLOCK = 32
CMP_STRIDE = 16
SLC_BLOCK = 64
N_SLC_BLOCKS = 8
WINDOW = 512

LANES = 128
POS_LANE = 96
LOG2E = 1.4426950408889634
VMEM_LIMIT = 56 * 1024 * 1024
SB_CHAINS = 8
NSA_CHAINS = 4

_NT = (((1,), (1,)), ((), ()))


def _dot(a, b):
    return jnp.dot(a, b, preferred_element_type=F32)


def _dot_nt(a, b):
    return lax.dot_general(a, b, _NT, preferred_element_type=F32)


def _rmsnorm(x, g):
    ms = jnp.mean(x * x, axis=-1, keepdims=True)
    return (x * lax.rsqrt(ms + RMS_EPS)) * g


def _split_bf16(x):
    hi = x.astype(BF16)
    lo = (x - hi.astype(F32)).astype(BF16)
    return hi, lo


def _neg_abs(x):
    return lax.bitcast_convert_type(lax.bitcast_convert_type(x, jnp.uint32) | jnp.uint32(0x80000000), F32)


def _params(*sem):
    return pltpu.CompilerParams(dimension_semantics=sem, vmem_limit_bytes=VMEM_LIMIT)


def _const_spec(shape):
    nd = len(shape)
    return pl.BlockSpec(shape, lambda *_: (0,) * nd)


def _nsa_proj_kernel(x_ref, g_ref, wm_ref, wc_ref, wg_ref, gb_ref, main_ref, cv_ref, gate_ref):
    a = _rmsnorm(x_ref[...], g_ref[...]).astype(BF16)
    main_ref[...] = _dot(a, wm_ref[...]).astype(main_ref.dtype)
    cv = _dot(a, wc_ref[...])
    for kv in range(2):
        for gp in range(N_GROUPS // 2):
            c0 = (kv * N_GROUPS + 2 * gp) * HEAD_DIM
            cv_ref[kv, gp, :, :] = cv[:, c0:c0 + LANES]
    gate_ref[...] = jax.nn.sigmoid(_dot(a, wg_ref[...]) + gb_ref[...])


def _nsa_proj(h, g, w_main, w_cv, w_gate, gate_b, tm):
    b, s, d = h.shape
    nm, ng = w_main.shape[1], w_gate.shape[1]
    return pl.pallas_call(
        _nsa_proj_kernel,
        grid=(b, s // tm),
        in_specs=[
            pl.BlockSpec((None, tm, d), lambda bi, r: (bi, r, 0)),
            _const_spec((1, d)),
            _const_spec(w_main.shape),
            _const_spec(w_cv.shape),
            _const_spec(w_gate.shape),
            _const_spec((1, ng)),
        ],
        out_specs=[
            pl.BlockSpec((None, tm, nm), lambda bi, r: (bi, r, 0)),
            pl.BlockSpec((None, 2, N_GROUPS // 2, tm, LANES), lambda bi, r: (bi, 0, 0, r, 0)),
            pl.BlockSpec((None, tm, ng), lambda bi, r: (bi, r, 0)),
        ],
        out_shape=[
            jax.ShapeDtypeStruct((b, s, nm), BF16),
            jax.ShapeDtypeStruct((b, 2, N_GROUPS // 2, s, LANES), F32),
            jax.ShapeDtypeStruct((b, s, ng), F32),
        ],
        compiler_params=_params("parallel", "parallel"),
        name="nsa_proj",
    )(h, g, w_main, w_cv, w_gate, gate_b)


def _sb_proj_kernel(x_ref, g_ref, w_ref, o_ref):
    a = _rmsnorm(x_ref[...], g_ref[...]).astype(BF16)
    o_ref[...] = _dot(a, w_ref[...]).astype(o_ref.dtype)


def _sb_proj(h, g, w, tm):
    b, s, d = h.shape
    n = w.shape[1]
    return pl.pallas_call(
        _sb_proj_kernel,
        grid=(b, s // tm),
        in_specs=[
            pl.BlockSpec((None, tm, d), lambda bi, r: (bi, r, 0)),
            _const_spec((1, d)),
            _const_spec(w.shape),
        ],
        out_specs=pl.BlockSpec((None, tm, n), lambda bi, r: (bi, r, 0)),
        out_shape=jax.ShapeDtypeStruct((b, s, n), BF16),
        compiler_params=_params("parallel", "parallel"),
        name="sb_proj",
    )(h, g, w)


def _nsa_compress_kernel(c_ref, pe_ref, w1_ref, w2_ref, o_ref):
    rows = c_ref.shape[1]
    nc = rows // (N_GROUPS // 2)
    out = [jnp.zeros((rows, LANES), F32) for _ in range(2)]
    for kv in range(2):
        c = c_ref[kv]
        top = (c + pe_ref[kv, 0:1, :]).astype(BF16)
        bot = (c + pe_ref[kv, 1:2, :]).astype(BF16)
        for e in range(2):
            a_top = _dot(top, w1_ref[kv, 0, e])
            a_bot = _dot(bot, w1_ref[kv, 1, e])
            hid = jax.nn.gelu(a_top + pltpu.roll(a_bot, rows - 1, 0), approximate=True)
            out[e] = out[e] + _dot(hid.astype(BF16), w2_ref[kv])
    for gp in range(N_GROUPS // 2):
        for e in range(2):
            g = 2 * gp + e
            o_ref[g * nc:(g + 1) * nc, :] = out[e][gp * nc:(gp + 1) * nc]


def _nsa_compress(c, pe, w1, w2pad):
    b, _, rows, width = c.shape
    return pl.pallas_call(
        _nsa_compress_kernel,
        grid=(b,),
        in_specs=[
            pl.BlockSpec((None, 2, rows, width), lambda bi: (bi, 0, 0, 0)),
            _const_spec(pe.shape),
            _const_spec(w1.shape),
            _const_spec(w2pad.shape),
        ],
        out_specs=pl.BlockSpec((None, 2 * rows, LANES), lambda bi: (bi, 0, 0)),
        out_shape=jax.ShapeDtypeStruct((b, 2 * rows, LANES), F32),
        compiler_params=_params("parallel"),
        name="nsa_compress",
    )(c, pe, w1, w2pad)


def _nsa_attn_kernel(*refs, tq, n_slc, n_sel, n_chain):
    q_refs, kk_refs, vv_refs = refs[0:n_chain], refs[n_chain:2 * n_chain], refs[2 * n_chain:3 * n_chain]
    cmp_ref, gate_ref, kpos_ref, cmapt_ref, hconst_ref, wb_ref, sb_ref, o_ref = refs[3 * n_chain:3 * n_chain + 8]
    kaug_s_ref, kaug_w_ref, vts_ref, vtw_ref, qsel_ref, qwin_ref = refs[3 * n_chain + 8:]
    chains = range(n_chain)
    i = pl.program_id(2)
    t0 = i * tq
    tk = tq
    n_kt = kaug_s_ref.shape[1] // tk
    gw = HPG * HEAD_DIM

    @pl.when(i == 0)
    def _():
        top = lax.broadcasted_iota(jnp.int32, (LANES, tk), 0) < HEAD_DIM
        for c in chains:
            kk = kk_refs[c][...].astype(F32)
            first = lax.broadcasted_iota(jnp.int32, kk.shape, 1) < HEAD_DIM
            aux = kpos_ref[...].astype(F32)
            kaug_s_ref[c] = jnp.where(first, kk, aux).astype(BF16)
            kaug_w_ref[c] = jnp.where(first, pltpu.roll(kk, HEAD_DIM, 1), aux).astype(BF16)
            for j in range(n_kt):
                vt = vv_refs[c][j * tk:(j + 1) * tk, :].astype(F32).T
                vts_ref[c, j] = jnp.where(top, vt, 1.0).astype(BF16)
                vtw_ref[c, j] = jnp.where(top, 1.0, vt).astype(BF16)

    low = lax.broadcasted_iota(jnp.int32, (tq, LANES), 1) < HEAD_DIM

    qa = []
    for c in chains:
        q4 = q_refs[c][...].astype(F32)
        per_head = []
        for h in range(HPG):
            blk = q4[:, LANES * (h // 2):LANES * (h // 2 + 1)]
            per_head.append(blk if h % 2 == 0 else pltpu.roll(blk, HEAD_DIM, 1))
        qa.append(per_head)

    def q_operand(c, h, aux):
        return jnp.where(low, qa[c][h], aux).astype(BF16)

    def scores(kref, qref, c, j0, nt):
        kt = kref[c, pl.ds(pl.multiple_of(j0 * tk, tk), nt * tk), :]
        return _dot_nt(kt, qref[c])

    def pv(vt_ref, c, j0, nt, p):
        acc = _dot(vt_ref[c, j0], p[0:tk])
        for u in range(1, nt):
            acc = acc + _dot(vt_ref[c, j0 + u], p[u * tk:(u + 1) * tk])
        return acc

    def heads(bias):
        return jnp.concatenate([bias] * HPG, axis=1)

    for c in chains:
        for h in range(HPG):
            qwin_ref[c, h * tq:(h + 1) * tq, :] = q_operand(c, h, hconst_ref[c, h:h + 1, :])
    nwt = min(WINDOW // tk + 1, n_kt)
    jw = jnp.clip(i - (nwt - 1), 0, n_kt - nwt)
    s_w = [scores(kaug_w_ref, qwin_ref, c, jw, nwt) for c in chains]
    wbias = wb_ref[i - jw]

    def window_head(c, h):
        s_h = s_w[c][:, h * tq:(h + 1) * tq] + wbias
        p_h = jnp.exp2(s_h - jnp.max(s_h, axis=0, keepdims=True)).astype(BF16)
        acc = pv(vtw_ref, c, jw, nwt, p_h)
        return acc[HEAD_DIM:] * (1.0 / acc[0:1])

    nc = cmp_ref.shape[0] // n_chain
    ck = [cmp_ref[c * nc:(c + 1) * nc, :] for c in chains]
    first_c = lax.broadcasted_iota(jnp.int32, (nc, LANES), 1) < HEAD_DIM
    sc_all = [_dot_nt(jnp.where(first_c, ck[c], 0.0).astype(BF16), qwin_ref[c]) for c in chains]
    dist_c = ((t0 + lax.broadcasted_iota(jnp.int32, (nc, tq), 1))
              - (lax.broadcasted_iota(jnp.int32, (nc, tq), 0) * CMP_STRIDE + (CMP_BLOCK - 1)))
    valid_c = dist_c >= 0
    dist_cf = dist_c.astype(F32)
    psum, p_heads = [], []
    for c in chains:
        ps = jnp.zeros((nc, tq), F32)
        ph = []
        for h in range(HPG):
            slope = hconst_ref[c, HPG + h:HPG + h + 1, 0:1]
            s_c = jnp.where(valid_c, sc_all[c][:, h * tq:(h + 1) * tq] - slope * dist_cf, NEG)
            m_c = jnp.max(s_c, axis=0, keepdims=True)
            e_c = jnp.where(valid_c, jnp.exp2(s_c - m_c), 0.0)
            den = jnp.sum(e_c, axis=0, keepdims=True)
            p_c = e_c / jnp.maximum(den, 1e-30)
            ps = ps + p_c
            ph.append(p_c.astype(BF16))
        psum.append(ps)
        p_heads.append(ph)
    o_cmp_t = [_dot(ck[c].T.astype(BF16), jnp.concatenate(p_heads[c], axis=1))[HEAD_DIM:] for c in chains]

    cmt = cmapt_ref[...]
    jp = cmt.shape[0]
    imp_t = []
    for c in chains:
        p_hi, p_lo = _split_bf16(psum[c])
        imp_t.append(_dot(cmt, p_hi) + _dot(cmt, p_lo))
    jb = lax.broadcasted_iota(jnp.int32, (jp, tq), 0)
    cur = jnp.right_shift(t0 + lax.broadcasted_iota(jnp.int32, (jp, tq), 1), SLC_BLOCK.bit_length() - 1)
    causal_j = jb <= cur
    forced = (jb == 0) | (jb == cur) | (jb == cur - 1)
    val = [jnp.where(causal_j & forced, BIG, jnp.where(causal_j, imp_t[c], -BIG)) for c in chains]
    rank = [jnp.zeros((jp, tq), jnp.int32) for c in chains]
    o_win_heads = [[] for c in chains]
    for j in range(n_slc):
        if j % (n_slc // HPG) == 0:
            for c in chains:
                o_win_heads[c].append(window_head(c, j // (n_slc // HPG)))
        for c in chains:
            row = val[c][j:j + 1, :]
            beats = (row > val[c]) | ((row == val[c]) & (jb > j))
            rank[c] = rank[c] + beats.astype(jnp.int32)
    o_win_t = [jnp.concatenate(o_win_heads[c], axis=1) for c in chains]

    for c in chains:
        mb_t = jnp.where(causal_j & (rank[c] < n_sel), 0.0, NEG)
        mb = jnp.concatenate([jnp.zeros((HEAD_DIM, tq), F32), mb_t,
                              jnp.zeros((LANES - HEAD_DIM - jp, tq), F32)], axis=0).T
        for h in range(HPG):
            qsel_ref[c, h * tq:(h + 1) * tq, :] = q_operand(c, h, mb + hconst_ref[c, h:h + 1, :])
    pair = i // 2
    sbias = heads(sb_ref[i - 2 * pair])
    s_d = [scores(kaug_s_ref, qsel_ref, c, 2 * pair, 2) + sbias for c in chains]
    m_d = [jnp.max(s_d[c], axis=0, keepdims=True) for c in chains]
    p_d = [jnp.exp2(s_d[c] - m_d[c]).astype(BF16) for c in chains]
    acc_d = [pv(vts_ref, c, 2 * pair, 2, p_d[c]) for c in chains]

    def sel_body(jj, carry):
        j0 = 2 * (pair - 1 - jj)
        s = [scores(kaug_s_ref, qsel_ref, c, j0, 2) for c in chains]
        m_new = [jnp.maximum(carry[c][0], jnp.max(s[c], axis=0, keepdims=True)) for c in chains]
        p = [jnp.exp2(s[c] - m_new[c]).astype(BF16) for c in chains]
        acc = [jnp.exp2(carry[c][0] - m_new[c]) * carry[c][1] + pv(vts_ref, c, j0, 2, p[c]) for c in chains]
        return tuple((m_new[c], acc[c]) for c in chains)

    res = lax.fori_loop(0, pair, sel_body, tuple((m_d[c], acc_d[c]) for c in chains))

    gt_t = gate_ref[...].T
    for c in chains:
        acc_s = res[c][1]
        o_sel_t = acc_s[:HEAD_DIM] * (1.0 / acc_s[HEAD_DIM:HEAD_DIM + 1])
        rows = []
        for h in range(HPG):
            sl = slice(h * tq, (h + 1) * tq)
            g_cmp, g_sel, g_win = (gt_t[br * N_HEADS + c * HPG + h:br * N_HEADS + c * HPG + h + 1]
                                   for br in range(N_BRANCH))
            rows.append(g_cmp * o_cmp_t[c][:, sl] + g_sel * o_sel_t[:, sl] + g_win * o_win_t[c][:, sl])
        o_ref[:, c * gw:(c + 1) * gw] = jnp.concatenate(rows, axis=0).T.astype(o_ref.dtype)


def _nsa_attn(main, cmp, gates, kpos, cmapt, hconst, wbias, sbias, tq):
    b, s, _ = main.shape
    assert (s // tq) % 2 == 0 and NSA_CHAINS == N_GROUPS
    nch = NSA_CHAINS
    nc = cmp.shape[1] // N_GROUPS
    n_slc = s // SLC_BLOCK
    gw = HPG * HEAD_DIM
    per_g = (gw + 2 * LANES) // LANES

    def q_spec(c):
        return pl.BlockSpec((None, tq, gw), lambda bi, gp, i: (bi, i, (gp * nch + c) * per_g // 2))

    def kv_spec(c, off):
        return pl.BlockSpec((None, s, LANES), lambda bi, gp, i: (bi, 0, (gp * nch + c) * per_g + off))

    kern = functools.partial(_nsa_attn_kernel, tq=tq, n_slc=n_slc, n_sel=min(N_SLC_BLOCKS, n_slc), n_chain=nch)
    return pl.pallas_call(
        kern,
        grid=(b, N_GROUPS // nch, s // tq),
        in_specs=(
            [q_spec(c) for c in range(nch)]
            + [kv_spec(c, 2) for c in range(nch)]
            + [kv_spec(c, 3) for c in range(nch)]
            + [
                pl.BlockSpec((None, nch * nc, LANES), lambda bi, gp, i: (bi, gp, 0)),
                pl.BlockSpec((None, tq, LANES), lambda bi, gp, i: (bi, i, 0)),
                _const_spec(kpos.shape),
                _const_spec(cmapt.shape),
                pl.BlockSpec((nch, 2 * HPG, LANES), lambda bi, gp, i: (gp, 0, 0)),
                _const_spec(wbias.shape),
                _const_spec(sbias.shape),
            ]),
        out_specs=pl.BlockSpec((None, tq, nch * gw), lambda bi, gp, i: (bi, i, gp)),
        out_shape=jax.ShapeDtypeStruct((b, s, N_GROUPS * gw), BF16),
        scratch_shapes=[
            pltpu.VMEM((nch, s, LANES), BF16),
            pltpu.VMEM((nch, s, LANES), BF16),
            pltpu.VMEM((nch, s // tq, LANES, tq), BF16),
            pltpu.VMEM((nch, s // tq, LANES, tq), BF16),
            pltpu.VMEM((nch, HPG * tq, LANES), BF16),
            pltpu.VMEM((nch, HPG * tq, LANES), BF16),
        ],
        compiler_params=_params("parallel", "parallel", "arbitrary"),
        name="nsa_attn",
    )(*([main] * (3 * nch)), cmp, gates, kpos, cmapt, hconst, wbias, sbias)


def _sb_attn_kernel(q_ref, k_ref, v_ref, tri_ref, o_ref, vt_ref, acc_ref, *, tq):
    i = pl.program_id(2)
    tk = tq
    n_kt = k_ref.shape[0] // tk
    n_chain = q_ref.shape[1] // LANES

    @pl.when(i == 0)
    def _():
        for c in range(n_chain):
            for j in range(n_kt):
                vt_ref[c, j] = v_ref[j * tk:(j + 1) * tk, c * LANES:(c + 1) * LANES].astype(F32).T.astype(BF16)

    low = lax.broadcasted_iota(jnp.int32, (tq, LANES), 1) < HEAD_DIM
    before = (lax.broadcasted_iota(jnp.int32, (tk, 2 * tq), 0)
              < (lax.broadcasted_iota(jnp.int32, (tk, 2 * tq), 1) & (tq - 1)))
    tri = tri_ref[...]

    qst = []
    for c in range(n_chain):
        q2 = q_ref[:, c * LANES:(c + 1) * LANES].astype(F32)
        qst.append(jnp.concatenate([jnp.where(low, q2, 0.0), jnp.where(low, 0.0, q2)], axis=0).astype(BF16))

    chains = range(n_chain)

    def tiles(j, diag, carry):
        off = pl.multiple_of(j * tk, tk)
        z = [_dot_nt(k_ref[pl.ds(off, tk), c * LANES:(c + 1) * LANES], qst[c]) for c in chains]
        sp = [jnp.maximum(z[c], 0.0) + jnp.log(1.0 + jnp.exp2(_neg_abs(z[c]))) * LOG2E for c in chains]
        if diag:
            sp = [jnp.where(before, sp[c], 0.0) for c in chains]
        sums = [_dot(tri, sp[c].astype(BF16)) for c in chains]
        if diag:
            a = [jnp.where(before, jnp.exp2(z[c] - sp[c] + sums[c][:tk]), 0.0) for c in chains]
        else:
            a = [jnp.exp2(z[c] - sp[c] + sums[c][:tk] + carry[c]) for c in chains]
        for c in chains:
            pv = _dot(vt_ref[c, j], a[c].astype(BF16))
            if diag:
                acc_ref[c] = pv
            else:
                acc_ref[c] += pv
        if diag:
            return tuple(sums[c][tk:tk + 1] for c in chains)
        return tuple(carry[c] + sums[c][tk:tk + 1] for c in chains)

    res = lax.fori_loop(0, i, lambda jj, carry: tiles(i - 1 - jj, False, carry), tiles(i, True, None))
    for c in range(n_chain):
        acc = acc_ref[c]
        out_t = jnp.concatenate([acc[:HEAD_DIM, :tq], acc[HEAD_DIM:, tq:]], axis=0)
        o_ref[:, c * LANES:(c + 1) * LANES] = out_t.T.astype(o_ref.dtype)


def _sb_attn(qkv, tri, tq):
    b, s, n3 = qkv.shape
    w = LANES * SB_CHAINS
    nblk = n3 // 3 // w
    return pl.pallas_call(
        functools.partial(_sb_attn_kernel, tq=tq),
        grid=(b, nblk, s // tq),
        in_specs=[
            pl.BlockSpec((None, tq, w), lambda bi, p, i: (bi, i, p)),
            pl.BlockSpec((None, s, w), lambda bi, p, i: (bi, 0, nblk + p)),
            pl.BlockSpec((None, s, w), lambda bi, p, i: (bi, 0, 2 * nblk + p)),
            _const_spec(tri.shape),
        ],
        out_specs=pl.BlockSpec((None, tq, w), lambda bi, p, i: (bi, i, p)),
        out_shape=jax.ShapeDtypeStruct((b, s, nblk * w), BF16),
        scratch_shapes=[pltpu.VMEM((SB_CHAINS, s // tq, LANES, tq), BF16),
                        pltpu.VMEM((SB_CHAINS, LANES, 2 * tq), F32)],
        compiler_params=_params("parallel", "parallel", "arbitrary"),
        name="sb_attn",
    )(qkv, qkv, qkv, tri)


def _mix_ffn_kernel(h_ref, o_ref, wo_ref, g_ref, w1_ref, w3_ref, w2_ref, fg_ref, out_ref, a_ref, acc_ref,
                    *, final, tf):
    h1 = h_ref[...] + _dot(o_ref[...], wo_ref[...])
    a_ref[...] = _rmsnorm(h1, g_ref[...]).astype(BF16)
    acc_ref[...] = h1

    for c in range(w1_ref.shape[1] // tf):
        cols = slice(c * tf, (c + 1) * tf)
        a = a_ref[...]
        u = _dot(a, w1_ref[:, cols])
        v = _dot(a, w3_ref[:, cols])
        t = (u * jax.nn.sigmoid(u)) * v
        acc_ref[...] += _dot(t.astype(BF16), w2_ref[cols, :])
    h2 = acc_ref[...]
    out_ref[...] = _rmsnorm(h2, fg_ref[...]) if final else h2


def _mix_ffn(h, o, wo, g, w1, w3, w2, fg, tm, tf, final):
    r, d = h.shape
    single = pl.Buffered(1)

    def wspec(shape):
        nd = len(shape)
        return pl.BlockSpec(shape, lambda *_: (0,) * nd, pipeline_mode=single)

    return pl.pallas_call(
        functools.partial(_mix_ffn_kernel, final=final, tf=tf),
        grid=(r // tm,),
        in_specs=[
            pl.BlockSpec((tm, d), lambda i: (i, 0)),
            pl.BlockSpec((tm, d), lambda i: (i, 0)),
            wspec(wo.shape),
            _const_spec((1, d)),
            wspec(w1.shape),
            wspec(w3.shape),
            wspec(w2.shape),
            _const_spec((1, d)),
        ],
        out_specs=pl.BlockSpec((tm, d), lambda i: (i, 0)),
        out_shape=jax.ShapeDtypeStruct((r, d), F32),
        scratch_shapes=[pltpu.VMEM((tm, d), BF16), pltpu.VMEM((tm, d), F32)],
        compiler_params=_params("parallel"),
        name="mix_ffn",
    )(h, o, wo, g, w1, w3, w2, fg)


def _nsa_layout(d):
    kvw = N_GROUPS * HEAD_DIM
    off = {"q": 0, "kc": d, "vc": d + kvw, "ksl": d + 2 * kvw, "vsl": d + 3 * kvw, "kw": d + 4 * kvw,
           "vw": d + 5 * kvw, "gt": d + 6 * kvw}
    main = []
    for g in range(N_GROUPS):
        main.append(np.arange(HPG * HEAD_DIM) + g * HPG * HEAD_DIM)
        for name in ("ksl", "kw", "vsl", "vw"):
            main.append(off[name] + g * HEAD_DIM + np.arange(HEAD_DIM))
    main = np.concatenate(main)
    cv = off["kc"] + np.arange(2 * kvw)
    scale = np.where(main < d, HEAD_DIM ** -0.5 * LOG2E, 1.0).astype(np.float32)
    return main, scale, cv, off["gt"]


def _bf16_pieces(x):
    x = np.asarray(x, np.float32)
    out = []
    for _ in range(3):
        p = x.astype(BF16).astype(np.float32)
        out.append(p)
        x = (x - p).astype(np.float32)
    return out


def _nsa_tables(s, tk):
    n_slc = s // SLC_BLOCK
    n_cmp = s // CMP_STRIDE - CMP_BLOCK // CMP_STRIDE + 1
    nc = s // CMP_STRIDE
    pos = np.arange(s)
    kpos = np.zeros((s, LANES), np.float32)
    assert n_slc <= POS_LANE - HEAD_DIM
    kpos[pos, HEAD_DIM + pos // SLC_BLOCK] = 1.0
    for c in range(3):
        kpos[:, POS_LANE + c] = (pos // tk) * tk
        kpos[:, POS_LANE + 3 + c] = pos % tk
    cs = np.arange(n_cmp) * CMP_STRIDE
    ss = np.arange(n_slc) * SLC_BLOCK
    ov = np.clip(np.minimum(cs[:, None] + CMP_BLOCK, ss[None] + SLC_BLOCK) - np.maximum(cs[:, None], ss[None]), 0, None)
    jp = -(-n_slc // 16) * 16
    cmapt = np.zeros((jp, nc), np.float32)
    cmapt[:n_slc, :n_cmp] = (ov / CMP_STRIDE).T
    slopes = np.asarray(2.0 ** (-8.0 * (np.arange(N_HEADS) + 1) / N_HEADS), np.float32)
    hconst = np.zeros((N_GROUPS, 2 * HPG, LANES), np.float32)
    slopes = (slopes.astype(np.float64) * LOG2E).astype(np.float32)
    pieces = _bf16_pieces(slopes)
    for g in range(N_GROUPS):
        for hh in range(HPG):
            for c in range(3):
                hconst[g, hh, POS_LANE + c] = pieces[c][g * HPG + hh]
                hconst[g, hh, POS_LANE + 3 + c] = pieces[c][g * HPG + hh]
            hconst[g, HPG + hh, :] = slopes[g * HPG + hh]
    n_kt = s // tk
    nwt = min(WINDOW // tk + 1, n_kt)
    tt = np.arange(tk)[None, :]

    def dist(r, nt):
        return r * tk + tt - np.arange(nt * tk)[:, None]

    wbias = np.stack([np.where((dist(r, nwt) >= 0) & (dist(r, nwt) < WINDOW), 0.0, NEG) for r in range(nwt)])
    sbias = np.stack([np.where(dist(r, 2) >= 0, 0.0, NEG) for r in range(2)])
    return (jnp.asarray(kpos, BF16), jnp.asarray(cmapt, BF16), jnp.asarray(hconst, F32),
            jnp.asarray(wbias, F32), jnp.asarray(sbias, F32))


def _ffn_chunk(d_ff):
    for tf in (512, 256, 128):
        if d_ff % tf == 0:
            return tf
    return d_ff


def kernel(x, mix_norm_g, ffn_norm_g, final_norm_g, nsa_w_in, nsa_gate_b, nsa_cmp_pe, nsa_cmp_w1, nsa_cmp_w2,
           nsa_w_out, sb_w_in, sb_w_out, ffn_w1, ffn_w3, ffn_w2):
    b, s, d = x.shape
    depth = mix_norm_g.shape[0]
    d_ff = ffn_w1.shape[2]
    tm = min(512, s)
    tq = min(256, s)
    tf = _ffn_chunk(d_ff)

    main_idx, main_scale, cv_idx, gt_off = _nsa_layout(d)
    sb_scale = np.where(np.arange(3 * d) < d, HEAD_DIM ** -0.5 * LOG2E, 1.0).astype(np.float32)
    n_gate = N_BRANCH * N_HEADS
    kpos, cmapt, hconst, wbias, sbias = _nsa_tables(s, tq)
    tri = jnp.asarray(np.concatenate([-np.triu(np.ones((tq, tq), np.float32), 1),
                                      -np.ones((16, tq), np.float32)], axis=0), BF16)
    chunk = CMP_STRIDE * HEAD_DIM
    nc = s // CMP_STRIDE

    h = x
    for i in range(depth):
        j = i // 2
        gm = mix_norm_g[i].reshape(1, d)
        if i % 2 == 0:
            w_in = nsa_w_in[j]
            w_main = (w_in[:, main_idx] * main_scale).astype(BF16)
            w_cv = w_in[:, cv_idx].astype(BF16)
            w_gate = jnp.pad(w_in[:, gt_off:gt_off + n_gate], ((0, 0), (0, LANES - n_gate))).astype(BF16)
            gate_b = jnp.pad(nsa_gate_b[j], (0, LANES - n_gate)).reshape(1, LANES)
            main, cv, gates = _nsa_proj(h, gm, w_main, w_cv, w_gate, gate_b, tm)
            c = cv.reshape(b, 2, N_GROUPS // 2 * nc, 2 * chunk)
            pe = jnp.broadcast_to(nsa_cmp_pe[j].reshape(2, 2, CMP_STRIDE, 1, HEAD_DIM),
                                  (2, 2, CMP_STRIDE, 2, HEAD_DIM)).reshape(2, 2, 2 * chunk)
            w1 = nsa_cmp_w1[j].reshape(2, 2, CMP_STRIDE, 1, HEAD_DIM, -1)
            w1 = jnp.stack([jnp.concatenate([w1, jnp.zeros_like(w1)], axis=3),
                            jnp.concatenate([jnp.zeros_like(w1), w1], axis=3)], axis=2)
            w1 = w1.reshape(2, 2, 2, 2 * chunk, -1)
            w2pad = jnp.zeros((2, nsa_cmp_w2.shape[2], LANES), F32)
            w2pad = w2pad.at[0, :, :HEAD_DIM].set(nsa_cmp_w2[j, 0]).at[1, :, HEAD_DIM:].set(nsa_cmp_w2[j, 1])
            cmp = _nsa_compress(c, pe, w1.astype(BF16), w2pad.astype(BF16))
            o = _nsa_attn(main, cmp, gates, kpos, cmapt, hconst, wbias, sbias, tq)
            w_out = nsa_w_out[j]
        else:
            qkv = _sb_proj(h, gm, (sb_w_in[j] * sb_scale).astype(BF16), tm)
            o = _sb_attn(qkv, tri, tq)
            w_out = sb_w_out[j]
        final = i == depth - 1
        h = _mix_ffn(h.reshape(b * s, d), o.reshape(b * s, d), w_out.astype(BF16), ffn_norm_g[i].reshape(1, d),
                     ffn_w1[i].astype(BF16), ffn_w3[i].astype(BF16), ffn_w2[i].astype(BF16),
                     final_norm_g.reshape(1, d), tm, tf, final).reshape(b, s, d)
    return h
```

```python
import functools

import numpy as np
import jax
import jax.numpy as jnp
from jax import lax
from jax.experimental import pallas as pl
from jax.experimental.pallas import tpu as pltpu

F32 = jnp.float32
BF16 = jnp.bfloat16

RMS_EPS = 1e-6
NEG = -1e30
BIG = 1e30

N_HEADS = 16
HEAD_DIM = 64
N_GROUPS = 4
HPG = N_HEADS // N_GROUPS
N_BRANCH = 3
CMP_BLOCK = 32
CMP_STRIDE = 16
SLC_BLOCK = 64
N_SLC_BLOCKS = 8
WINDOW = 512

LANES = 128
POS_LANE = 96
LOG2E = 1.4426950408889634
VMEM_LIMIT = 56 * 1024 * 1024
SB_CHAINS = 8
NSA_CHAINS = 4
SB_LAG = 1
NSA_LAG = 2

_NT = (((1,), (1,)), ((), ()))


def _dot(a, b):
    return jnp.dot(a, b, preferred_element_type=F32)


def _dot_nt(a, b):
    return lax.dot_general(a, b, _NT, preferred_element_type=F32)


def _rmsnorm(x, g):
    ms = jnp.mean(x * x, axis=-1, keepdims=True)
    return (x * lax.rsqrt(ms + RMS_EPS)) * g


def _split_bf16(x):
    hi = x.astype(BF16)
    lo = (x - hi.astype(F32)).astype(BF16)
    return hi, lo


def _neg_abs(x):
    return lax.bitcast_convert_type(lax.bitcast_convert_type(x, jnp.uint32) | jnp.uint32(0x80000000), F32)


def _params(*sem):
    return pltpu.CompilerParams(dimension_semantics=sem, vmem_limit_bytes=VMEM_LIMIT)


def _const_spec(shape):
    nd = len(shape)
    return pl.BlockSpec(shape, lambda *_: (0,) * nd)


def _nsa_proj_kernel(x_ref, g_ref, wm_ref, wc_ref, wg_ref, gb_ref, main_ref, cv_ref, gate_ref):
    a = _rmsnorm(x_ref[...], g_ref[...]).astype(BF16)
    main_ref[...] = _dot(a, wm_ref[...]).astype(main_ref.dtype)
    cv = _dot(a, wc_ref[...])
    for kv in range(2):
        for gp in range(N_GROUPS // 2):
            c0 = (kv * N_GROUPS + 2 * gp) * HEAD_DIM
            cv_ref[kv, gp, :, :] = cv[:, c0:c0 + LANES]
    gate_ref[...] = jax.nn.sigmoid(_dot(a, wg_ref[...]) + gb_ref[...])


def _nsa_proj(h, g, w_main, w_cv, w_gate, gate_b, tm):
    b, s, d = h.shape
    nm, ng = w_main.shape[1], w_gate.shape[1]
    return pl.pallas_call(
        _nsa_proj_kernel,
        grid=(b, s // tm),
        in_specs=[
            pl.BlockSpec((None, tm, d), lambda bi, r: (bi, r, 0)),
            _const_spec((1, d)),
            _const_spec(w_main.shape),
            _const_spec(w_cv.shape),
            _const_spec(w_gate.shape),
            _const_spec((1, ng)),
        ],
        out_specs=[
            pl.BlockSpec((None, tm, nm), lambda bi, r: (bi, r, 0)),
            pl.BlockSpec((None, 2, N_GROUPS // 2, tm, LANES), lambda bi, r: (bi, 0, 0, r, 0)),
            pl.BlockSpec((None, tm, ng), lambda bi, r: (bi, r, 0)),
        ],
        out_shape=[
            jax.ShapeDtypeStruct((b, s, nm), BF16),
            jax.ShapeDtypeStruct((b, 2, N_GROUPS // 2, s, LANES), F32),
            jax.ShapeDtypeStruct((b, s, ng), F32),
        ],
        compiler_params=_params("parallel", "parallel"),
        name="nsa_proj",
    )(h, g, w_main, w_cv, w_gate, gate_b)


def _sb_proj_kernel(x_ref, g_ref, w_ref, o_ref):
    a = _rmsnorm(x_ref[...], g_ref[...]).astype(BF16)
    o_ref[...] = _dot(a, w_ref[...]).astype(o_ref.dtype)


def _sb_proj(h, g, w, tm):
    b, s, d = h.shape
    n = w.shape[1]
    return pl.pallas_call(
        _sb_proj_kernel,
        grid=(b, s // tm),
        in_specs=[
            pl.BlockSpec((None, tm, d), lambda bi, r: (bi, r, 0)),
            _const_spec((1, d)),
            _const_spec(w.shape),
        ],
        out_specs=pl.BlockSpec((None, tm, n), lambda bi, r: (bi, r, 0)),
        out_shape=jax.ShapeDtypeStruct((b, s, n), BF16),
        compiler_params=_params("parallel", "parallel"),
        name="sb_proj",
    )(h, g, w)


def _nsa_compress_kernel(c_ref, pe_ref, w1_ref, w2_ref, o_ref):
    rows = c_ref.shape[1]
    nc = rows // (N_GROUPS // 2)
    out = [jnp.zeros((rows, LANES), F32) for _ in range(2)]
    for kv in range(2):
        c = c_ref[kv]
        top = (c + pe_ref[kv, 0:1, :]).astype(BF16)
        bot = (c + pe_ref[kv, 1:2, :]).astype(BF16)
        for e in range(2):
            a_top = _dot(top, w1_ref[kv, 0, e])
            a_bot = _dot(bot, w1_ref[kv, 1, e])
            hid = jax.nn.gelu(a_top + pltpu.roll(a_bot, rows - 1, 0), approximate=True)
            out[e] = out[e] + _dot(hid.astype(BF16), w2_ref[kv])
    for gp in range(N_GROUPS // 2):
        for e in range(2):
            g = 2 * gp + e
            o_ref[g * nc:(g + 1) * nc, :] = out[e][gp * nc:(gp + 1) * nc]


def _nsa_compress(c, pe, w1, w2pad):
    b, _, rows, width = c.shape
    return pl.pallas_call(
        _nsa_compress_kernel,
        grid=(b,),
        in_specs=[
            pl.BlockSpec((None, 2, rows, width), lambda bi: (bi, 0, 0, 0)),
            _const_spec(pe.shape),
            _const_spec(w1.shape),
            _const_spec(w2pad.shape),
        ],
        out_specs=pl.BlockSpec((None, 2 * rows, LANES), lambda bi: (bi, 0, 0)),
        out_shape=jax.ShapeDtypeStruct((b, 2 * rows, LANES), F32),
        compiler_params=_params("parallel"),
        name="nsa_compress",
    )(c, pe, w1, w2pad)


def _nsa_attn_kernel(*refs, tq, n_slc, n_sel, n_chain):
    q_refs, kk_refs, vv_refs = refs[0:n_chain], refs[n_chain:2 * n_chain], refs[2 * n_chain:3 * n_chain]
    cmp_ref, gate_ref, kpos_ref, cmapt_ref, hconst_ref, wb_ref, sb_ref, o_ref = refs[3 * n_chain:3 * n_chain + 8]
    kaug_s_ref, kaug_w_ref, vts_ref, vtw_ref, qsel_ref, qwin_ref = refs[3 * n_chain + 8:]
    chains = range(n_chain)
    i = pl.program_id(2)
    t0 = i * tq
    tk = tq
    n_kt = kaug_s_ref.shape[1] // tk
    gw = HPG * HEAD_DIM

    @pl.when(i == 0)
    def _():
        top = lax.broadcasted_iota(jnp.int32, (LANES, tk), 0) < HEAD_DIM
        for c in chains:
            kk = kk_refs[c][...].astype(F32)
            first = lax.broadcasted_iota(jnp.int32, kk.shape, 1) < HEAD_DIM
            aux = kpos_ref[...].astype(F32)
            kaug_s_ref[c] = jnp.where(first, kk, aux).astype(BF16)
            kaug_w_ref[c] = jnp.where(first, pltpu.roll(kk, HEAD_DIM, 1), aux).astype(BF16)
            for j in range(n_kt):
                vt = vv_refs[c][j * tk:(j + 1) * tk, :].astype(F32).T
                vts_ref[c, j] = jnp.where(top, vt, 1.0).astype(BF16)
                vtw_ref[c, j] = jnp.where(top, 1.0, vt).astype(BF16)

    low = lax.broadcasted_iota(jnp.int32, (tq, LANES), 1) < HEAD_DIM

    qa = []
    for c in chains:
        q4 = q_refs[c][...].astype(F32)
        per_head = []
        for h in range(HPG):
            blk = q4[:, LANES * (h // 2):LANES * (h // 2 + 1)]
            per_head.append(blk if h % 2 == 0 else pltpu.roll(blk, HEAD_DIM, 1))
        qa.append(per_head)

    def q_operand(c, h, aux):
        return jnp.where(low, qa[c][h], aux).astype(BF16)

    def scores(kref, qref, c, j0, nt):
        kt = kref[c, pl.ds(pl.multiple_of(j0 * tk, tk), nt * tk), :]
        return _dot_nt(kt, qref[c])

    def pv(vt_ref, c, j0, nt, p):
        acc = _dot(vt_ref[c, j0], p[0:tk])
        for u in range(1, nt):
            acc = acc + _dot(vt_ref[c, j0 + u], p[u * tk:(u + 1) * tk])
        return acc

    def heads(bias):
        return jnp.concatenate([bias] * HPG, axis=1)

    for c in chains:
        for h in range(HPG):
            qwin_ref[c, h * tq:(h + 1) * tq, :] = q_operand(c, h, hconst_ref[c, h:h + 1, :])
    nwt = min(WINDOW // tk + 1, n_kt)
    jw = jnp.clip(i - (nwt - 1), 0, n_kt - nwt)
    wbias = wb_ref[i - jw]

    nc = cmp_ref.shape[0] // n_chain
    ck = [cmp_ref[c * nc:(c + 1) * nc, :] for c in chains]
    first_c = lax.broadcasted_iota(jnp.int32, (nc, LANES), 1) < HEAD_DIM
    sc_all = [_dot_nt(jnp.where(first_c, ck[c], 0.0).astype(BF16), qwin_ref[c]) for c in chains]
    s_w = [scores(kaug_w_ref, qwin_ref, c, jw, nwt) for c in chains]

    def window_head(c, h):
        s_h = s_w[c][:, h * tq:(h + 1) * tq] + wbias
        p_h = jnp.exp2(s_h - jnp.max(s_h, axis=0, keepdims=True)).astype(BF16)
        acc = pv(vtw_ref, c, jw, nwt, p_h)
        return acc[HEAD_DIM:] * (1.0 / acc[0:1])

    dist_c = ((t0 + lax.broadcasted_iota(jnp.int32, (nc, tq), 1))
              - (lax.broadcasted_iota(jnp.int32, (nc, tq), 0) * CMP_STRIDE + (CMP_BLOCK - 1)))
    valid_c = dist_c >= 0
    dist_cf = dist_c.astype(F32)
    psum, p_heads = [], []
    for c in chains:
        ps = jnp.zeros((nc, tq), F32)
        ph = []
        for h in range(HPG):
            slope = hconst_ref[c, HPG + h:HPG + h + 1, 0:1]
            s_c = jnp.where(valid_c, sc_all[c][:, h * tq:(h + 1) * tq] - slope * dist_cf, NEG)
            m_c = jnp.max(s_c, axis=0, keepdims=True)
            e_c = jnp.where(valid_c, jnp.exp2(s_c - m_c), 0.0)
            den = jnp.sum(e_c, axis=0, keepdims=True)
            p_c = e_c / jnp.maximum(den, 1e-30)
            ps = ps + p_c
            ph.append(p_c.astype(BF16))
        psum.append(ps)
        p_heads.append(ph)
    o_cmp_t = [_dot(ck[c].T.astype(BF16), jnp.concatenate(p_heads[c], axis=1))[HEAD_DIM:] for c in chains]

    cmt = cmapt_ref[...]
    jp = cmt.shape[0]
    imp_t = []
    for c in chains:
        p_hi, p_lo = _split_bf16(psum[c])
        imp_t.append(_dot(cmt, p_hi) + _dot(cmt, p_lo))
    jb = lax.broadcasted_iota(jnp.int32, (jp, tq), 0)
    cur = jnp.right_shift(t0 + lax.broadcasted_iota(jnp.int32, (jp, tq), 1), SLC_BLOCK.bit_length() - 1)
    causal_j = jb <= cur
    forced = (jb == 0) | (jb == cur) | (jb == cur - 1)
    val = [jnp.where(causal_j & forced, BIG, jnp.where(causal_j, imp_t[c], -BIG)) for c in chains]
    rank = [jnp.zeros((jp, tq), jnp.int32) for c in chains]
    o_win_heads = [[] for c in chains]
    for j in range(n_slc):
        if j % (n_slc // HPG) == 0:
            for c in chains:
                o_win_heads[c].append(window_head(c, j // (n_slc // HPG)))
        for c in chains:
            row = val[c][j:j + 1, :]
            beats = (row > val[c]) | ((row == val[c]) & (jb > j))
            rank[c] = rank[c] + beats.astype(jnp.int32)
    o_win_t = [jnp.concatenate(o_win_heads[c], axis=1) for c in chains]

    for c in chains:
        mb_t = jnp.where(causal_j & (rank[c] < n_sel), 0.0, NEG)
        mb = jnp.concatenate([jnp.zeros((HEAD_DIM, tq), F32), mb_t,
                              jnp.zeros((LANES - HEAD_DIM - jp, tq), F32)], axis=0).T
        for h in range(HPG):
            qsel_ref[c, h * tq:(h + 1) * tq, :] = q_operand(c, h, mb + hconst_ref[c, h:h + 1, :])
    pair = i // 2
    sbias = heads(sb_ref[i - 2 * pair])
    def sel_chunk(j0, carry):
        s, out = [None] * n_chain, [None] * n_chain

        def stage_scores(c):
            s[c] = scores(kaug_s_ref, qsel_ref, c, j0, 2)
            if carry is None:
                s[c] = s[c] + sbias

        def stage_pv(c):
            m_tile = jnp.max(s[c], axis=0, keepdims=True)
            if carry is None:
                out[c] = (m_tile, pv(vts_ref, c, j0, 2, jnp.exp2(s[c] - m_tile).astype(BF16)))
            else:
                m_new = jnp.maximum(carry[c][0], m_tile)
                p = jnp.exp2(s[c] - m_new).astype(BF16)
                out[c] = (m_new, jnp.exp2(carry[c][0] - m_new) * carry[c][1] + pv(vts_ref, c, j0, 2, p))

        for step in range(n_chain + NSA_LAG):
            if step < n_chain:
                stage_scores(step)
            if 0 <= step - NSA_LAG < n_chain:
                stage_pv(step - NSA_LAG)
        return tuple(out)

    res = lax.fori_loop(0, pair, lambda jj, carry: sel_chunk(2 * (pair - 1 - jj), carry), sel_chunk(2 * pair, None))

    gt_t = gate_ref[...].T
    for c in chains:
        acc_s = res[c][1]
        o_sel_t = acc_s[:HEAD_DIM] * (1.0 / acc_s[HEAD_DIM:HEAD_DIM + 1])
        rows = []
        for h in range(HPG):
            sl = slice(h * tq, (h + 1) * tq)
            g_cmp, g_sel, g_win = (gt_t[br * N_HEADS + c * HPG + h:br * N_HEADS + c * HPG + h + 1]
                                   for br in range(N_BRANCH))
            rows.append(g_cmp * o_cmp_t[c][:, sl] + g_sel * o_sel_t[:, sl] + g_win * o_win_t[c][:, sl])
        o_ref[:, c * gw:(c + 1) * gw] = jnp.concatenate(rows, axis=0).T.astype(o_ref.dtype)


def _nsa_attn(main, cmp, gates, kpos, cmapt, hconst, wbias, sbias, tq):
    b, s, _ = main.shape
    assert (s // tq) % 2 == 0 and NSA_CHAINS == N_GROUPS
    nch = NSA_CHAINS
    nc = cmp.shape[1] // N_GROUPS
    n_slc = s // SLC_BLOCK
    gw = HPG * HEAD_DIM
    per_g = (gw + 2 * LANES) // LANES

    def q_spec(c):
        return pl.BlockSpec((None, tq, gw), lambda bi, gp, i: (bi, i, (gp * nch + c) * per_g // 2))

    def kv_spec(c, off):
        return pl.BlockSpec((None, s, LANES), lambda bi, gp, i: (bi, 0, (gp * nch + c) * per_g + off))

    kern = functools.partial(_nsa_attn_kernel, tq=tq, n_slc=n_slc, n_sel=min(N_SLC_BLOCKS, n_slc), n_chain=nch)
    return pl.pallas_call(
        kern,
        grid=(b, N_GROUPS // nch, s // tq),
        in_specs=(
            [q_spec(c) for c in range(nch)]
            + [kv_spec(c, 2) for c in range(nch)]
            + [kv_spec(c, 3) for c in range(nch)]
            + [
                pl.BlockSpec((None, nch * nc, LANES), lambda bi, gp, i: (bi, gp, 0)),
                pl.BlockSpec((None, tq, LANES), lambda bi, gp, i: (bi, i, 0)),
                _const_spec(kpos.shape),
                _const_spec(cmapt.shape),
                pl.BlockSpec((nch, 2 * HPG, LANES), lambda bi, gp, i: (gp, 0, 0)),
                _const_spec(wbias.shape),
                _const_spec(sbias.shape),
            ]),
        out_specs=pl.BlockSpec((None, tq, nch * gw), lambda bi, gp, i: (bi, i, gp)),
        out_shape=jax.ShapeDtypeStruct((b, s, N_GROUPS * gw), BF16),
        scratch_shapes=[
            pltpu.VMEM((nch, s, LANES), BF16),
            pltpu.VMEM((nch, s, LANES), BF16),
            pltpu.VMEM((nch, s // tq, LANES, tq), BF16),
            pltpu.VMEM((nch, s // tq, LANES, tq), BF16),
            pltpu.VMEM((nch, HPG * tq, LANES), BF16),
            pltpu.VMEM((nch, HPG * tq, LANES), BF16),
        ],
        compiler_params=_params("parallel", "parallel", "arbitrary"),
        name="nsa_attn",
    )(*([main] * (3 * nch)), cmp, gates, kpos, cmapt, hconst, wbias, sbias)


def _sb_attn_kernel(q_ref, k_ref, v_ref, tri_ref, o_ref, vt_ref, acc_ref, *, tq):
    i = pl.program_id(2)
    tk = tq
    n_kt = k_ref.shape[0] // tk
    n_chain = q_ref.shape[1] // LANES

    @pl.when(i == 0)
    def _():
        for c in range(n_chain):
            for j in range(n_kt):
                vt_ref[c, j] = v_ref[j * tk:(j + 1) * tk, c * LANES:(c + 1) * LANES].astype(F32).T.astype(BF16)

    low = lax.broadcasted_iota(jnp.int32, (tq, LANES), 1) < HEAD_DIM
    before = (lax.broadcasted_iota(jnp.int32, (tk, 2 * tq), 0)
              < (lax.broadcasted_iota(jnp.int32, (tk, 2 * tq), 1) & (tq - 1)))
    tri = tri_ref[...]

    qst = []
    for c in range(n_chain):
        q2 = q_ref[:, c * LANES:(c + 1) * LANES].astype(F32)
        qst.append(jnp.concatenate([jnp.where(low, q2, 0.0), jnp.where(low, 0.0, q2)], axis=0).astype(BF16))

    chains = range(n_chain)

    def tiles(j, diag, carry):
        off = pl.multiple_of(j * tk, tk)
        z, sp, sums = [None] * n_chain, [None] * n_chain, [None] * n_chain

        def scores(c):
            z[c] = _dot_nt(k_ref[pl.ds(off, tk), c * LANES:(c + 1) * LANES], qst[c])
            sp[c] = jnp.maximum(z[c], 0.0) + jnp.log(1.0 + jnp.exp2(_neg_abs(z[c]))) * LOG2E
            if diag:
                sp[c] = jnp.where(before, sp[c], 0.0)

        def suffix(c):
            sums[c] = _dot(tri, sp[c].astype(BF16))

        def weights(c):
            if diag:
                a = jnp.where(before, jnp.exp2(z[c] - sp[c] + sums[c][:tk]), 0.0)
            else:
                a = jnp.exp2(z[c] - sp[c] + sums[c][:tk] + carry[c])
            pv = _dot(vt_ref[c, j], a.astype(BF16))
            if diag:
                acc_ref[c] = pv
            else:
                acc_ref[c] += pv

        for step in range(n_chain + 2 * SB_LAG):
            for d, stage in enumerate((scores, suffix, weights)):
                if 0 <= step - d * SB_LAG < n_chain:
                    stage(step - d * SB_LAG)
        if diag:
            return tuple(sums[c][tk:tk + 1] for c in chains)
        return tuple(carry[c] + sums[c][tk:tk + 1] for c in chains)

    res = lax.fori_loop(0, i, lambda jj, carry: tiles(i - 1 - jj, False, carry), tiles(i, True, None))
    for c in range(n_chain):
        acc = acc_ref[c]
        out_t = jnp.concatenate([acc[:HEAD_DIM, :tq], acc[HEAD_DIM:, tq:]], axis=0)
        o_ref[:, c * LANES:(c + 1) * LANES] = out_t.T.astype(o_ref.dtype)


def _sb_attn(qkv, tri, tq):
    b, s, n3 = qkv.shape
    w = LANES * SB_CHAINS
    nblk = n3 // 3 // w
    return pl.pallas_call(
        functools.partial(_sb_attn_kernel, tq=tq),
        grid=(b, nblk, s // tq),
        in_specs=[
            pl.BlockSpec((None, tq, w), lambda bi, p, i: (bi, i, p)),
            pl.BlockSpec((None, s, w), lambda bi, p, i: (bi, 0, nblk + p)),
            pl.BlockSpec((None, s, w), lambda bi, p, i: (bi, 0, 2 * nblk + p)),
            _const_spec(tri.shape),
        ],
        out_specs=pl.BlockSpec((None, tq, w), lambda bi, p, i: (bi, i, p)),
        out_shape=jax.ShapeDtypeStruct((b, s, nblk * w), BF16),
        scratch_shapes=[pltpu.VMEM((SB_CHAINS, s // tq, LANES, tq), BF16),
                        pltpu.VMEM((SB_CHAINS, LANES, 2 * tq), F32)],
        compiler_params=_params("parallel", "parallel", "arbitrary"),
        name="sb_attn",
    )(qkv, qkv, qkv, tri)


def _mix_ffn_kernel(h_ref, o_ref, wo_ref, g_ref, w1_ref, w3_ref, w2_ref, fg_ref, out_ref, a_ref, acc_ref,
                    *, final, tf):
    h1 = h_ref[...] + _dot(o_ref[...], wo_ref[...])
    a_ref[...] = _rmsnorm(h1, g_ref[...]).astype(BF16)
    acc_ref[...] = h1

    for c in range(w1_ref.shape[1] // tf):
        cols = slice(c * tf, (c + 1) * tf)
        a = a_ref[...]
        u = _dot(a, w1_ref[:, cols])
        v = _dot(a, w3_ref[:, cols])
        t = (u * jax.nn.sigmoid(u)) * v
        acc_ref[...] += _dot(t.astype(BF16), w2_ref[cols, :])
    h2 = acc_ref[...]
    out_ref[...] = _rmsnorm(h2, fg_ref[...]) if final else h2


def _mix_ffn(h, o, wo, g, w1, w3, w2, fg, tm, tf, final):
    r, d = h.shape
    single = pl.Buffered(1)

    def wspec(shape):
        nd = len(shape)
        return pl.BlockSpec(shape, lambda *_: (0,) * nd, pipeline_mode=single)

    return pl.pallas_call(
        functools.partial(_mix_ffn_kernel, final=final, tf=tf),
        grid=(r // tm,),
        in_specs=[
            pl.BlockSpec((tm, d), lambda i: (i, 0)),
            pl.BlockSpec((tm, d), lambda i: (i, 0)),
            wspec(wo.shape),
            _const_spec((1, d)),
            wspec(w1.shape),
            wspec(w3.shape),
            wspec(w2.shape),
            _const_spec((1, d)),
        ],
        out_specs=pl.BlockSpec((tm, d), lambda i: (i, 0)),
        out_shape=jax.ShapeDtypeStruct((r, d), F32),
        scratch_shapes=[pltpu.VMEM((tm, d), BF16), pltpu.VMEM((tm, d), F32)],
        compiler_params=_params("parallel"),
        name="mix_ffn",
    )(h, o, wo, g, w1, w3, w2, fg)


def _nsa_layout(d):
    kvw = N_GROUPS * HEAD_DIM
    off = {"q": 0, "kc": d, "vc": d + kvw, "ksl": d + 2 * kvw, "vsl": d + 3 * kvw, "kw": d + 4 * kvw,
           "vw": d + 5 * kvw, "gt": d + 6 * kvw}
    main = []
    for g in range(N_GROUPS):
        main.append(np.arange(HPG * HEAD_DIM) + g * HPG * HEAD_DIM)
        for name in ("ksl", "kw", "vsl", "vw"):
            main.append(off[name] + g * HEAD_DIM + np.arange(HEAD_DIM))
    main = np.concatenate(main)
    cv = off["kc"] + np.arange(2 * kvw)
    scale = np.where(main < d, HEAD_DIM ** -0.5 * LOG2E, 1.0).astype(np.float32)
    return main, scale, cv, off["gt"]


def _bf16_pieces(x):
    x = np.asarray(x, np.float32)
    out = []
    for _ in range(3):
        p = x.astype(BF16).astype(np.float32)
        out.append(p)
        x = (x - p).astype(np.float32)
    return out


def _nsa_tables(s, tk):
    n_slc = s // SLC_BLOCK
    n_cmp = s // CMP_STRIDE - CMP_BLOCK // CMP_STRIDE + 1
    nc = s // CMP_STRIDE
    pos = np.arange(s)
    kpos = np.zeros((s, LANES), np.float32)
    assert n_slc <= POS_LANE - HEAD_DIM
    kpos[pos, HEAD_DIM + pos // SLC_BLOCK] = 1.0
    for c in range(3):
        kpos[:, POS_LANE + c] = (pos // tk) * tk
        kpos[:, POS_LANE + 3 + c] = pos % tk
    cs = np.arange(n_cmp) * CMP_STRIDE
    ss = np.arange(n_slc) * SLC_BLOCK
    ov = np.clip(np.minimum(cs[:, None] + CMP_BLOCK, ss[None] + SLC_BLOCK) - np.maximum(cs[:, None], ss[None]), 0, None)
    jp = -(-n_slc // 16) * 16
    cmapt = np.zeros((jp, nc), np.float32)
    cmapt[:n_slc, :n_cmp] = (ov / CMP_STRIDE).T
    slopes = np.asarray(2.0 ** (-8.0 * (np.arange(N_HEADS) + 1) / N_HEADS), np.float32)
    hconst = np.zeros((N_GROUPS, 2 * HPG, LANES), np.float32)
    slopes = (slopes.astype(np.float64) * LOG2E).astype(np.float32)
    pieces = _bf16_pieces(slopes)
    for g in range(N_GROUPS):
        for hh in range(HPG):
            for c in range(3):
                hconst[g, hh, POS_LANE + c] = pieces[c][g * HPG + hh]
                hconst[g, hh, POS_LANE + 3 + c] = pieces[c][g * HPG + hh]
            hconst[g, HPG + hh, :] = slopes[g * HPG + hh]
    n_kt = s // tk
    nwt = min(WINDOW // tk + 1, n_kt)
    tt = np.arange(tk)[None, :]

    def dist(r, nt):
        return r * tk + tt - np.arange(nt * tk)[:, None]

    wbias = np.stack([np.where((dist(r, nwt) >= 0) & (dist(r, nwt) < WINDOW), 0.0, NEG) for r in range(nwt)])
    sbias = np.stack([np.where(dist(r, 2) >= 0, 0.0, NEG) for r in range(2)])
    return (jnp.asarray(kpos, BF16), jnp.asarray(cmapt, BF16), jnp.asarray(hconst, F32),
            jnp.asarray(wbias, F32), jnp.asarray(sbias, F32))


def _ffn_chunk(d_ff):
    for tf in (512, 256, 128):
        if d_ff % tf == 0:
            return tf
    return d_ff


def kernel(x, mix_norm_g, ffn_norm_g, final_norm_g, nsa_w_in, nsa_gate_b, nsa_cmp_pe, nsa_cmp_w1, nsa_cmp_w2,
           nsa_w_out, sb_w_in, sb_w_out, ffn_w1, ffn_w3, ffn_w2):
    b, s, d = x.shape
    depth = mix_norm_g.shape[0]
    d_ff = ffn_w1.shape[2]
    tm = min(512, s)
    tq = min(256, s)
    tf = _ffn_chunk(d_ff)

    main_idx, main_scale, cv_idx, gt_off = _nsa_layout(d)
    sb_scale = np.where(np.arange(3 * d) < d, HEAD_DIM ** -0.5 * LOG2E, 1.0).astype(np.float32)
    n_gate = N_BRANCH * N_HEADS
    kpos, cmapt, hconst, wbias, sbias = _nsa_tables(s, tq)
    tri = jnp.asarray(np.concatenate([-np.triu(np.ones((tq, tq), np.float32), 1),
                                      -np.ones((16, tq), np.float32)], axis=0), BF16)
    chunk = CMP_STRIDE * HEAD_DIM
    nc = s // CMP_STRIDE

    h = x
    for i in range(depth):
        j = i // 2
        gm = mix_norm_g[i].reshape(1, d)
        if i % 2 == 0:
            w_in = nsa_w_in[j]
            w_main = (w_in[:, main_idx] * main_scale).astype(BF16)
            w_cv = w_in[:, cv_idx].astype(BF16)
            w_gate = jnp.pad(w_in[:, gt_off:gt_off + n_gate], ((0, 0), (0, LANES - n_gate))).astype(BF16)
            gate_b = jnp.pad(nsa_gate_b[j], (0, LANES - n_gate)).reshape(1, LANES)
            main, cv, gates = _nsa_proj(h, gm, w_main, w_cv, w_gate, gate_b, tm)
            c = cv.reshape(b, 2, N_GROUPS // 2 * nc, 2 * chunk)
            pe = jnp.broadcast_to(nsa_cmp_pe[j].reshape(2, 2, CMP_STRIDE, 1, HEAD_DIM),
                                  (2, 2, CMP_STRIDE, 2, HEAD_DIM)).reshape(2, 2, 2 * chunk)
            w1 = nsa_cmp_w1[j].reshape(2, 2, CMP_STRIDE, 1, HEAD_DIM, -1)
            w1 = jnp.stack([jnp.concatenate([w1, jnp.zeros_like(w1)], axis=3),
                            jnp.concatenate([jnp.zeros_like(w1), w1], axis=3)], axis=2)
            w1 = w1.reshape(2, 2, 2, 2 * chunk, -1)
            w2pad = jnp.zeros((2, nsa_cmp_w2.shape[2], LANES), F32)
            w2pad = w2pad.at[0, :, :HEAD_DIM].set(nsa_cmp_w2[j, 0]).at[1, :, HEAD_DIM:].set(nsa_cmp_w2[j, 1])
            cmp = _nsa_compress(c, pe, w1.astype(BF16), w2pad.astype(BF16))
            o = _nsa_attn(main, cmp, gates, kpos, cmapt, hconst, wbias, sbias, tq)
            w_out = nsa_w_out[j]
        else:
            qkv = _sb_proj(h, gm, (sb_w_in[j] * sb_scale).astype(BF16), tm)
            o = _sb_attn(qkv, tri, tq)
            w_out = sb_w_out[j]
        final = i == depth - 1
        h = _mix_ffn(h.reshape(b * s, d), o.reshape(b * s, d), w_out.astype(BF16), ffn_norm_g[i].reshape(1, d),
                     ffn_w1[i].astype(BF16), ffn_w3[i].astype(BF16), ffn_w2[i].astype(BF16),
                     final_norm_g.reshape(1, d), tm, tf, final).reshape(b, s, d)
    return h
```

```python
import functools

import numpy as np
import jax
import jax.numpy as jnp
from jax import lax
from jax.experimental import pallas as pl
from jax.experimental.pallas import tpu as pltpu

F32 = jnp.float32
BF16 = jnp.bfloat16

RMS_EPS = 1e-6
NEG = -1e30
BIG = 1e30

N_HEADS = 16
HEAD_DIM = 64
N_GROUPS = 4
HPG = N_HEADS // N_GROUPS
N_BRANCH = 3
CMP_BLOCK = 32
CMP_STRIDE = 16
SLC_BLOCK = 64
N_SLC_BLOCKS = 8
WINDOW = 512

LANES = 128
POS_LANE = 96
LOG2E = 1.4426950408889634
VMEM_LIMIT = 56 * 1024 * 1024
SB_CHAINS = 8
NSA_CHAINS = 4
SB_LAG = SB_CHAINS
NSA_LAG = 2

_NT = (((1,), (1,)), ((), ()))


def _dot(a, b):
    return jnp.dot(a, b, preferred_element_type=F32)


def _dot_nt(a, b):
    return lax.dot_general(a, b, _NT, preferred_element_type=F32)


def _rmsnorm(x, g):
    ms = jnp.mean(x * x, axis=-1, keepdims=True)
    return (x * lax.rsqrt(ms + RMS_EPS)) * g


def _split_bf16(x):
    hi = x.astype(BF16)
    lo = (x - hi.astype(F32)).astype(BF16)
    return hi, lo


def _neg_abs(x):
    return lax.bitcast_convert_type(lax.bitcast_convert_type(x, jnp.uint32) | jnp.uint32(0x80000000), F32)


def _params(*sem):
    return pltpu.CompilerParams(dimension_semantics=sem, vmem_limit_bytes=VMEM_LIMIT)


def _const_spec(shape):
    nd = len(shape)
    return pl.BlockSpec(shape, lambda *_: (0,) * nd)


def _nsa_proj_kernel(x_ref, g_ref, wm_ref, wc_ref, wg_ref, gb_ref, main_ref, cv_ref, gate_ref):
    a = _rmsnorm(x_ref[...], g_ref[...]).astype(BF16)
    main_ref[...] = _dot(a, wm_ref[...]).astype(main_ref.dtype)
    cv = _dot(a, wc_ref[...])
    for kv in range(2):
        for gp in range(N_GROUPS // 2):
            c0 = (kv * N_GROUPS + 2 * gp) * HEAD_DIM
            cv_ref[kv, gp, :, :] = cv[:, c0:c0 + LANES]
    gate_ref[...] = jax.nn.sigmoid(_dot(a, wg_ref[...]) + gb_ref[...])


def _nsa_proj(h, g, w_main, w_cv, w_gate, gate_b, tm):
    b, s, d = h.shape
    nm, ng = w_main.shape[1], w_gate.shape[1]
    return pl.pallas_call(
        _nsa_proj_kernel,
        grid=(b, s // tm),
        in_specs=[
            pl.BlockSpec((None, tm, d), lambda bi, r: (bi, r, 0)),
            _const_spec((1, d)),
            _const_spec(w_main.shape),
            _const_spec(w_cv.shape),
            _const_spec(w_gate.shape),
            _const_spec((1, ng)),
        ],
        out_specs=[
            pl.BlockSpec((None, tm, nm), lambda bi, r: (bi, r, 0)),
            pl.BlockSpec((None, 2, N_GROUPS // 2, tm, LANES), lambda bi, r: (bi, 0, 0, r, 0)),
            pl.BlockSpec((None, tm, ng), lambda bi, r: (bi, r, 0)),
        ],
        out_shape=[
            jax.ShapeDtypeStruct((b, s, nm), BF16),
            jax.ShapeDtypeStruct((b, 2, N_GROUPS // 2, s, LANES), F32),
            jax.ShapeDtypeStruct((b, s, ng), F32),
        ],
        compiler_params=_params("parallel", "parallel"),
        name="nsa_proj",
    )(h, g, w_main, w_cv, w_gate, gate_b)


def _sb_proj_kernel(x_ref, g_ref, w_ref, o_ref):
    a = _rmsnorm(x_ref[...], g_ref[...]).astype(BF16)
    o_ref[...] = _dot(a, w_ref[...]).astype(o_ref.dtype)


def _sb_proj(h, g, w, tm):
    b, s, d = h.shape
    n = w.shape[1]
    return pl.pallas_call(
        _sb_proj_kernel,
        grid=(b, s // tm),
        in_specs=[
            pl.BlockSpec((None, tm, d), lambda bi, r: (bi, r, 0)),
            _const_spec((1, d)),
            _const_spec(w.shape),
        ],
        out_specs=pl.BlockSpec((None, tm, n), lambda bi, r: (bi, r, 0)),
        out_shape=jax.ShapeDtypeStruct((b, s, n), BF16),
        compiler_params=_params("parallel", "parallel"),
        name="sb_proj",
    )(h, g, w)


def _nsa_compress_kernel(c_ref, pe_ref, w1_ref, w2_ref, o_ref):
    rows = c_ref.shape[1]
    nc = rows // (N_GROUPS // 2)
    out = [jnp.zeros((rows, LANES), F32) for _ in range(2)]
    for kv in range(2):
        c = c_ref[kv]
        top = (c + pe_ref[kv, 0:1, :]).astype(BF16)
        bot = (c + pe_ref[kv, 1:2, :]).astype(BF16)
        for e in range(2):
            a_top = _dot(top, w1_ref[kv, 0, e])
            a_bot = _dot(bot, w1_ref[kv, 1, e])
            hid = jax.nn.gelu(a_top + pltpu.roll(a_bot, rows - 1, 0), approximate=True)
            out[e] = out[e] + _dot(hid.astype(BF16), w2_ref[kv])
    for gp in range(N_GROUPS // 2):
        for e in range(2):
            g = 2 * gp + e
            o_ref[g * nc:(g + 1) * nc, :] = out[e][gp * nc:(gp + 1) * nc]


def _nsa_compress(c, pe, w1, w2pad):
    b, _, rows, width = c.shape
    return pl.pallas_call(
        _nsa_compress_kernel,
        grid=(b,),
        in_specs=[
            pl.BlockSpec((None, 2, rows, width), lambda bi: (bi, 0, 0, 0)),
            _const_spec(pe.shape),
            _const_spec(w1.shape),
            _const_spec(w2pad.shape),
        ],
        out_specs=pl.BlockSpec((None, 2 * rows, LANES), lambda bi: (bi, 0, 0)),
        out_shape=jax.ShapeDtypeStruct((b, 2 * rows, LANES), F32),
        compiler_params=_params("parallel"),
        name="nsa_compress",
    )(c, pe, w1, w2pad)


def _nsa_attn_kernel(*refs, tq, n_slc, n_sel, n_chain):
    (q_ref, kk_ref, vv_ref, cmp_ref, gate_ref, kpos_ref, cmapt_ref, hconst_ref, hcol_ref, wb_ref, sb_ref,
     o_ref, kaug_s_ref, kaug_w_ref, vts_ref, vtw_ref, qsel_ref, qwin_ref, acc_ref) = refs
    chains = range(n_chain)
    i = pl.program_id(2)
    t0 = i * tq
    tk = tq
    n_kt = kaug_s_ref.shape[1] // tk
    gw = HPG * HEAD_DIM

    @pl.when(i == 0)
    def _():
        top = lax.broadcasted_iota(jnp.int32, (LANES, tk), 0) < HEAD_DIM
        for c in chains:
            kk = kk_ref[:, c * LANES:(c + 1) * LANES].astype(F32)
            first = lax.broadcasted_iota(jnp.int32, kk.shape, 1) < HEAD_DIM
            aux = kpos_ref[...].astype(F32)
            kaug_s_ref[c] = jnp.where(first, kk, aux).astype(BF16)
            kaug_w_ref[c] = jnp.where(first, pltpu.roll(kk, HEAD_DIM, 1), aux).astype(BF16)
            for j in range(n_kt):
                vt = vv_ref[j * tk:(j + 1) * tk, c * LANES:(c + 1) * LANES].astype(F32).T
                vts_ref[c, j] = jnp.where(top, vt, 1.0).astype(BF16)
                vtw_ref[c, j] = jnp.where(top, 1.0, vt).astype(BF16)

    q_t = [q_ref[:, c * gw:(c + 1) * gw].astype(F32).T for c in chains]

    def q_operand(c, h, aux_t):
        return jnp.concatenate([q_t[c][h * HEAD_DIM:(h + 1) * HEAD_DIM], aux_t], axis=0).astype(BF16)

    def aux_rows(c, h):
        return jnp.concatenate([hcol_ref[c, h]] * (tq // LANES), axis=1)

    def scores(kref, qref, c, j0, nt):
        kt = kref[c, pl.ds(pl.multiple_of(j0 * tk, tk), nt * tk), :]
        return _dot(kt, qref[c])

    def pv(vt_ref, c, j0, nt, p):
        acc = _dot(vt_ref[c, j0], p[0:tk])
        for u in range(1, nt):
            acc = acc + _dot(vt_ref[c, j0 + u], p[u * tk:(u + 1) * tk])
        return acc

    def heads(bias):
        return jnp.concatenate([bias] * HPG, axis=1)

    for c in chains:
        for h in range(HPG):
            qwin_ref[c, :, h * tq:(h + 1) * tq] = q_operand(c, h, aux_rows(c, h))
    nwt = min(WINDOW // tk + 1, n_kt)
    jw = jnp.clip(i - (nwt - 1), 0, n_kt - nwt)
    wbias = wb_ref[i - jw]

    nc = cmp_ref.shape[0] // n_chain
    ck = [cmp_ref[c * nc:(c + 1) * nc, :] for c in chains]
    first_c = lax.broadcasted_iota(jnp.int32, (nc, LANES), 1) < HEAD_DIM
    sc_all = [_dot(jnp.where(first_c, ck[c], 0.0).astype(BF16), qwin_ref[c]) for c in chains]
    s_w = [scores(kaug_w_ref, qwin_ref, c, jw, nwt) for c in chains]

    def window_head(c, h):
        s_h = s_w[c][:, h * tq:(h + 1) * tq] + wbias
        p_h = jnp.exp2(s_h - jnp.max(s_h, axis=0, keepdims=True)).astype(BF16)
        acc = pv(vtw_ref, c, jw, nwt, p_h)
        return acc[HEAD_DIM:] * (1.0 / acc[0:1])

    dist_c = ((t0 + lax.broadcasted_iota(jnp.int32, (nc, tq), 1))
              - (lax.broadcasted_iota(jnp.int32, (nc, tq), 0) * CMP_STRIDE + (CMP_BLOCK - 1)))
    valid_c = dist_c >= 0
    dist_cf = dist_c.astype(F32)
    psum, p_heads = [], []
    for c in chains:
        ps = jnp.zeros((nc, tq), F32)
        ph = []
        for h in range(HPG):
            slope = hconst_ref[c, h:h + 1, 0:1]
            s_c = jnp.where(valid_c, sc_all[c][:, h * tq:(h + 1) * tq] - slope * dist_cf, NEG)
            m_c = jnp.max(s_c, axis=0, keepdims=True)
            e_c = jnp.where(valid_c, jnp.exp2(s_c - m_c), 0.0)
            den = jnp.sum(e_c, axis=0, keepdims=True)
            p_c = e_c / jnp.maximum(den, 1e-30)
            ps = ps + p_c
            ph.append(p_c.astype(BF16))
        psum.append(ps)
        p_heads.append(ph)
    o_cmp_t = [_dot(ck[c].T.astype(BF16), jnp.concatenate(p_heads[c], axis=1))[HEAD_DIM:] for c in chains]

    cmt = cmapt_ref[...]
    jp = cmt.shape[0]
    imp_t = []
    for c in chains:
        p_hi, p_lo = _split_bf16(psum[c])
        imp_t.append(_dot(cmt, p_hi) + _dot(cmt, p_lo))
    jb = lax.broadcasted_iota(jnp.int32, (jp, tq), 0)
    cur = jnp.right_shift(t0 + lax.broadcasted_iota(jnp.int32, (jp, tq), 1), SLC_BLOCK.bit_length() - 1)
    causal_j = jb <= cur
    forced = (jb == 0) | (jb == cur) | (jb == cur - 1)
    val = [jnp.where(causal_j & forced, BIG, jnp.where(causal_j, imp_t[c], -BIG)) for c in chains]
    rank = [jnp.zeros((jp, tq), jnp.int32) for c in chains]
    o_win_heads = [[] for c in chains]
    for j in range(n_slc):
        if j % (n_slc // HPG) == 0:
            for c in chains:
                o_win_heads[c].append(window_head(c, j // (n_slc // HPG)))
        for c in chains:
            row = val[c][j:j + 1, :]
            beats = (row > val[c]) | ((row == val[c]) & (jb > j))
            rank[c] = rank[c] + beats.astype(jnp.int32)
    o_win_t = [jnp.concatenate(o_win_heads[c], axis=1) for c in chains]

    for c in chains:
        mb_t = jnp.where(causal_j & (rank[c] < n_sel), 0.0, NEG)
        mb_t = jnp.concatenate([mb_t, jnp.zeros((HEAD_DIM - jp, tq), F32)], axis=0)
        for h in range(HPG):
            qsel_ref[c, :, h * tq:(h + 1) * tq] = q_operand(c, h, mb_t + aux_rows(c, h))
    pair = i // 2
    sbias = heads(sb_ref[i - 2 * pair])
    def sel_chunk(j0, carry):
        s, out = [None] * n_chain, [None] * n_chain

        def stage_scores(c):
            s[c] = scores(kaug_s_ref, qsel_ref, c, j0, 2)
            if carry is None:
                s[c] = s[c] + sbias

        def stage_pv(c):
            m_tile = jnp.max(s[c], axis=0, keepdims=True)
            if carry is None:
                out[c] = m_tile
                acc_ref[c] = pv(vts_ref, c, j0, 2, jnp.exp2(s[c] - m_tile).astype(BF16))
            else:
                out[c] = jnp.maximum(carry[c], m_tile)
                p = jnp.exp2(s[c] - out[c]).astype(BF16)
                acc_ref[c] = jnp.exp2(carry[c] - out[c]) * acc_ref[c] + pv(vts_ref, c, j0, 2, p)

        for step in range(n_chain + NSA_LAG):
            if step < n_chain:
                stage_scores(step)
            if 0 <= step - NSA_LAG < n_chain:
                stage_pv(step - NSA_LAG)
        return tuple(out)

    lax.fori_loop(0, pair, lambda jj, carry: sel_chunk(2 * (pair - 1 - jj), carry), sel_chunk(2 * pair, None))

    gt_t = gate_ref[...].T
    for c in chains:
        acc_s = acc_ref[c]
        o_sel_t = acc_s[:HEAD_DIM] * (1.0 / acc_s[HEAD_DIM:HEAD_DIM + 1])
        rows = []
        for h in range(HPG):
            sl = slice(h * tq, (h + 1) * tq)
            g_cmp, g_sel, g_win = (gt_t[br * N_HEADS + c * HPG + h:br * N_HEADS + c * HPG + h + 1]
                                   for br in range(N_BRANCH))
            rows.append(g_cmp * o_cmp_t[c][:, sl] + g_sel * o_sel_t[:, sl] + g_win * o_win_t[c][:, sl])
        o_ref[:, c * gw:(c + 1) * gw] = jnp.concatenate(rows, axis=0).T.astype(o_ref.dtype)


def _nsa_attn(main, cmp, gates, kpos, cmapt, hconst, hcol, wbias, sbias, tq):
    b, s, _ = main.shape
    assert (s // tq) % 2 == 0 and NSA_CHAINS == N_GROUPS
    nch = NSA_CHAINS
    nc = cmp.shape[1] // N_GROUPS
    n_slc = s // SLC_BLOCK
    gw = HPG * HEAD_DIM
    qw, kvw = N_GROUPS * gw, N_GROUPS * LANES
    assert qw % kvw == 0

    kern =functools.partial(_nsa_attn_kernel, tq=tq, n_slc=n_slc, n_sel=min(N_SLC_BLOCKS, n_slc), n_chain=nch)
    return pl.pallas_call(
        kern,
        grid=(b, N_GROUPS // nch, s // tq),
        in_specs=(
            [
                pl.BlockSpec((None, tq, qw), lambda bi, gp, i: (bi, i, 0)),
                pl.BlockSpec((None, s, kvw), lambda bi, gp, i: (bi, 0, qw // kvw)),
                pl.BlockSpec((None, s, kvw), lambda bi, gp, i: (bi, 0, qw // kvw + 1)),
            ]
            + [
                pl.BlockSpec((None, nch * nc, LANES), lambda bi, gp, i: (bi, gp, 0)),
                pl.BlockSpec((None, tq, LANES), lambda bi, gp, i: (bi, i, 0)),
                _const_spec(kpos.shape),
                _const_spec(cmapt.shape),
                pl.BlockSpec((nch, 2 * HPG, LANES), lambda bi, gp, i: (gp, 0, 0)),
                pl.BlockSpec((nch, HPG, HEAD_DIM, LANES), lambda bi, gp, i: (gp, 0, 0, 0)),
                _const_spec(wbias.shape),
                _const_spec(sbias.shape),
            ]),
        out_specs=pl.BlockSpec((None, tq, nch * gw), lambda bi, gp, i: (bi, i, gp)),
        out_shape=jax.ShapeDtypeStruct((b, s, N_GROUPS * gw), BF16),
        scratch_shapes=[
            pltpu.VMEM((nch, s, LANES), BF16),
            pltpu.VMEM((nch, s, LANES), BF16),
            pltpu.VMEM((nch, s // tq, LANES, tq), BF16),
            pltpu.VMEM((nch, s // tq, LANES, tq), BF16),
            pltpu.VMEM((nch, LANES, HPG * tq), BF16),
            pltpu.VMEM((nch, LANES, HPG * tq), BF16),
            pltpu.VMEM((nch, LANES, HPG * tq), F32),
        ],
        compiler_params=_params("parallel", "parallel", "arbitrary"),
        name="nsa_attn",
    )(main, main, main, cmp, gates, kpos, cmapt, hconst, hcol, wbias, sbias)


def _sb_attn_kernel(q_ref, k_ref, v_ref, tri_ref, o_ref, vt_ref, acc_ref, *, tq):
    i = pl.program_id(2)
    tk = tq
    n_kt = k_ref.shape[0] // tk
    n_chain = q_ref.shape[1] // LANES

    @pl.when(i == 0)
    def _():
        for c in range(n_chain):
            for j in range(n_kt):
                vt_ref[c, j] = v_ref[j * tk:(j + 1) * tk, c * LANES:(c + 1) * LANES].astype(F32).T.astype(BF16)

    low = lax.broadcasted_iota(jnp.int32, (tq, LANES), 1) < HEAD_DIM
    before = (lax.broadcasted_iota(jnp.int32, (tk, 2 * tq), 0)
              < (lax.broadcasted_iota(jnp.int32, (tk, 2 * tq), 1) & (tq - 1)))
    tri = tri_ref[...]

    qst = []
    for c in range(n_chain):
        q2 = q_ref[:, c * LANES:(c + 1) * LANES].astype(F32)
        qst.append(jnp.concatenate([jnp.where(low, q2, 0.0), jnp.where(low, 0.0, q2)], axis=0).astype(BF16))

    chains = range(n_chain)

    def tiles(j, diag, carry):
        off = pl.multiple_of(j * tk, tk)
        z, sp, sums = [None] * n_chain, [None] * n_chain, [None] * n_chain

        def scores(c):
            z[c] = _dot_nt(k_ref[pl.ds(off, tk), c * LANES:(c + 1) * LANES], qst[c])
            sp[c] = jnp.maximum(z[c], 0.0) + jnp.log(1.0 + jnp.exp2(_neg_abs(z[c]))) * LOG2E
            if diag:
                sp[c] = jnp.where(before, sp[c], 0.0)

        def suffix(c):
            sums[c] = _dot(tri, sp[c].astype(BF16))

        def weights(c):
            if diag:
                a = jnp.where(before, jnp.exp2(z[c] - sp[c] + sums[c][:tk]), 0.0)
            else:
                a = jnp.exp2(z[c] - sp[c] + sums[c][:tk] + carry[c])
            pv = _dot(vt_ref[c, j], a.astype(BF16))
            if diag:
                acc_ref[c] = pv
            else:
                acc_ref[c] += pv

        for step in range(n_chain + 2 * SB_LAG):
            for d, stage in enumerate((scores, suffix, weights)):
                if 0 <= step - d * SB_LAG < n_chain:
                    stage(step - d * SB_LAG)
        if diag:
            return tuple(sums[c][tk:tk + 1] for c in chains)
        return tuple(carry[c] + sums[c][tk:tk + 1] for c in chains)

    res = lax.fori_loop(0, i, lambda jj, carry: tiles(i - 1 - jj, False, carry), tiles(i, True, None))
    for c in range(n_chain):
        acc = acc_ref[c]
        out_t = jnp.concatenate([acc[:HEAD_DIM, :tq], acc[HEAD_DIM:, tq:]], axis=0)
        o_ref[:, c * LANES:(c + 1) * LANES] = out_t.T.astype(o_ref.dtype)


def _sb_attn(qkv, tri, tq):
    b, s, n3 = qkv.shape
    w = LANES * SB_CHAINS
    nblk = n3 // 3 // w
    return pl.pallas_call(
        functools.partial(_sb_attn_kernel, tq=tq),
        grid=(b, nblk, s // tq),
        in_specs=[
            pl.BlockSpec((None, tq, w), lambda bi, p, i: (bi, i, p)),
            pl.BlockSpec((None, s, w), lambda bi, p, i: (bi, 0, nblk + p)),
            pl.BlockSpec((None, s, w), lambda bi, p, i: (bi, 0, 2 * nblk + p)),
            _const_spec(tri.shape),
        ],
        out_specs=pl.BlockSpec((None, tq, w), lambda bi, p, i: (bi, i, p)),
        out_shape=jax.ShapeDtypeStruct((b, s, nblk * w), BF16),
        scratch_shapes=[pltpu.VMEM((SB_CHAINS, s // tq, LANES, tq), BF16),
                        pltpu.VMEM((SB_CHAINS, LANES, 2 * tq), F32)],
        compiler_params=_params("parallel", "parallel", "arbitrary"),
        name="sb_attn",
    )(qkv, qkv, qkv, tri)


def _mix_ffn_kernel(h_ref, o_ref, wo_ref, g_ref, w1_ref, w3_ref, w2_ref, fg_ref, out_ref, a_ref, acc_ref,
                    *, final, tf):
    h1 = h_ref[...] + _dot(o_ref[...], wo_ref[...])
    a_ref[...] = _rmsnorm(h1, g_ref[...]).astype(BF16)
    acc_ref[...] = h1

    for c in range(w1_ref.shape[1] // tf):
        cols = slice(c * tf, (c + 1) * tf)
        a = a_ref[...]
        u = _dot(a, w1_ref[:, cols])
        v = _dot(a, w3_ref[:, cols])
        t = (u * jax.nn.sigmoid(u)) * v
        acc_ref[...] += _dot(t.astype(BF16), w2_ref[cols, :])
    h2 = acc_ref[...]
    out_ref[...] = _rmsnorm(h2, fg_ref[...]) if final else h2


def _mix_ffn(h, o, wo, g, w1, w3, w2, fg, tm, tf, final):
    r, d = h.shape
    single = pl.Buffered(1)

    def wspec(shape):
        nd = len(shape)
        return pl.BlockSpec(shape, lambda *_: (0,) * nd, pipeline_mode=single)

    return pl.pallas_call(
        functools.partial(_mix_ffn_kernel, final=final, tf=tf),
        grid=(r // tm,),
        in_specs=[
            pl.BlockSpec((tm, d), lambda i: (i, 0)),
            pl.BlockSpec((tm, d), lambda i: (i, 0)),
            wspec(wo.shape),
            _const_spec((1, d)),
            wspec(w1.shape),
            wspec(w3.shape),
            wspec(w2.shape),
            _const_spec((1, d)),
        ],
        out_specs=pl.BlockSpec((tm, d), lambda i: (i, 0)),
        out_shape=jax.ShapeDtypeStruct((r, d), F32),
        scratch_shapes=[pltpu.VMEM((tm, d), BF16), pltpu.VMEM((tm, d), F32)],
        compiler_params=_params("parallel"),
        name="mix_ffn",
    )(h, o, wo, g, w1, w3, w2, fg)


def _nsa_layout(d):
    kvw = N_GROUPS * HEAD_DIM
    off = {"q": 0, "kc": d, "vc": d + kvw, "ksl": d + 2 * kvw, "vsl": d + 3 * kvw, "kw": d + 4 * kvw,
           "vw": d + 5 * kvw, "gt": d + 6 * kvw}
    main = [np.arange(d)]
    for pair in (("ksl", "kw"), ("vsl", "vw")):
        for g in range(N_GROUPS):
            for name in pair:
                main.append(off[name] + g * HEAD_DIM + np.arange(HEAD_DIM))
    main = np.concatenate(main)
    cv = off["kc"] + np.arange(2 * kvw)
    scale = np.where(main < d, HEAD_DIM ** -0.5 * LOG2E, 1.0).astype(np.float32)
    return main, scale, cv, off["gt"]


def _bf16_pieces(x):
    x = np.asarray(x, np.float32)
    out = []
    for _ in range(3):
        p = x.astype(BF16).astype(np.float32)
        out.append(p)
        x = (x - p).astype(np.float32)
    return out


def _nsa_tables(s, tk):
    n_slc = s // SLC_BLOCK
    n_cmp = s // CMP_STRIDE - CMP_BLOCK // CMP_STRIDE + 1
    nc = s // CMP_STRIDE
    pos = np.arange(s)
    kpos = np.zeros((s, LANES), np.float32)
    assert n_slc <= POS_LANE - HEAD_DIM
    kpos[pos, HEAD_DIM + pos // SLC_BLOCK] = 1.0
    for c in range(3):
        kpos[:, POS_LANE + c] = (pos // tk) * tk
        kpos[:, POS_LANE + 3 + c] = pos % tk
    cs = np.arange(n_cmp) * CMP_STRIDE
    ss = np.arange(n_slc) * SLC_BLOCK
    ov = np.clip(np.minimum(cs[:, None] + CMP_BLOCK, ss[None] + SLC_BLOCK) - np.maximum(cs[:, None], ss[None]), 0, None)
    jp = -(-n_slc // 16) * 16
    cmapt = np.zeros((jp, nc), np.float32)
    cmapt[:n_slc, :n_cmp] = (ov / CMP_STRIDE).T
    slopes = np.asarray(2.0 ** (-8.0 * (np.arange(N_HEADS) + 1) / N_HEADS), np.float32)
    slopes = (slopes.astype(np.float64) * LOG2E).astype(np.float32)
    pieces = _bf16_pieces(slopes)
    hconst = np.zeros((N_GROUPS, 2 * HPG, LANES), np.float32)
    hcol = np.zeros((N_GROUPS, HPG, HEAD_DIM, LANES), np.float32)
    for g in range(N_GROUPS):
        for hh in range(HPG):
            for c in range(3):
                hcol[g, hh, POS_LANE - HEAD_DIM + c, :] = pieces[c][g * HPG + hh]
                hcol[g, hh, POS_LANE - HEAD_DIM + 3 + c, :] = pieces[c][g * HPG + hh]
            hconst[g, hh, :] = slopes[g * HPG + hh]
    n_kt = s // tk
    nwt = min(WINDOW // tk + 1, n_kt)
    tt = np.arange(tk)[None, :]

    def dist(r, nt):
        return r * tk + tt - np.arange(nt * tk)[:, None]

    wbias = np.stack([np.where((dist(r, nwt) >= 0) & (dist(r, nwt) < WINDOW), 0.0, NEG) for r in range(nwt)])
    sbias = np.stack([np.where(dist(r, 2) >= 0, 0.0, NEG) for r in range(2)])
    return (jnp.asarray(kpos, BF16), jnp.asarray(cmapt, BF16), jnp.asarray(hconst, F32), jnp.asarray(hcol, F32),
            jnp.asarray(wbias, F32), jnp.asarray(sbias, F32))


def _ffn_chunk(d_ff):
    for tf in (512, 256, 128):
        if d_ff % tf == 0:
            return tf
    return d_ff


def kernel(x, mix_norm_g, ffn_norm_g, final_norm_g, nsa_w_in, nsa_gate_b, nsa_cmp_pe, nsa_cmp_w1, nsa_cmp_w2,
           nsa_w_out, sb_w_in, sb_w_out, ffn_w1, ffn_w3, ffn_w2):
    b, s, d = x.shape
    depth = mix_norm_g.shape[0]
    d_ff = ffn_w1.shape[2]
    tm = min(512, s)
    tq = min(256, s)
    tf = _ffn_chunk(d_ff)

    main_idx, main_scale, cv_idx, gt_off = _nsa_layout(d)
    sb_scale = np.where(np.arange(3 * d) < d, HEAD_DIM ** -0.5 * LOG2E, 1.0).astype(np.float32)
    n_gate = N_BRANCH * N_HEADS
    kpos, cmapt, hconst, hcol, wbias, sbias = _nsa_tables(s, tq)
    tri = jnp.asarray(np.concatenate([-np.triu(np.ones((tq, tq), np.float32), 1),
                                      -np.ones((16, tq), np.float32)], axis=0), BF16)
    chunk = CMP_STRIDE * HEAD_DIM
    nc = s // CMP_STRIDE

    h = x
    for i in range(depth):
        j = i // 2
        gm = mix_norm_g[i].reshape(1, d)
        if i % 2 == 0:
            w_in = nsa_w_in[j]
            w_main = (w_in[:, main_idx] * main_scale).astype(BF16)
            w_cv = w_in[:, cv_idx].astype(BF16)
            w_gate = jnp.pad(w_in[:, gt_off:gt_off + n_gate], ((0, 0), (0, LANES - n_gate))).astype(BF16)
            gate_b = jnp.pad(nsa_gate_b[j], (0, LANES - n_gate)).reshape(1, LANES)
            main, cv, gates = _nsa_proj(h, gm, w_main, w_cv, w_gate, gate_b, tm)
            c = cv.reshape(b, 2, N_GROUPS // 2 * nc, 2 * chunk)
            pe = jnp.broadcast_to(nsa_cmp_pe[j].reshape(2, 2, CMP_STRIDE, 1, HEAD_DIM),
                                  (2, 2, CMP_STRIDE, 2, HEAD_DIM)).reshape(2, 2, 2 * chunk)
            w1 = nsa_cmp_w1[j].reshape(2, 2, CMP_STRIDE, 1, HEAD_DIM, -1)
            w1 = jnp.stack([jnp.concatenate([w1, jnp.zeros_like(w1)], axis=3),
                            jnp.concatenate([jnp.zeros_like(w1), w1], axis=3)], axis=2)
            w1 = w1.reshape(2, 2, 2, 2 * chunk, -1)
            w2pad = jnp.zeros((2, nsa_cmp_w2.shape[2], LANES), F32)
            w2pad = w2pad.at[0, :, :HEAD_DIM].set(nsa_cmp_w2[j, 0]).at[1, :, HEAD_DIM:].set(nsa_cmp_w2[j, 1])
            cmp = _nsa_compress(c, pe, w1.astype(BF16), w2pad.astype(BF16))
            o = _nsa_attn(main, cmp, gates, kpos, cmapt, hconst, hcol, wbias, sbias, tq)
            w_out = nsa_w_out[j]
        else:
            qkv = _sb_proj(h, gm, (sb_w_in[j] * sb_scale).astype(BF16), tm)
            o = _sb_attn(qkv, tri, tq)
            w_out = sb_w_out[j]
        final = i == depth - 1
        h = _mix_ffn(h.reshape(b * s, d), o.reshape(b * s, d), w_out.astype(BF16), ffn_norm_g[i].reshape(1, d),
                     ffn_w1[i].astype(BF16), ffn_w3[i].astype(BF16), ffn_w2[i].astype(BF16),
                     final_norm_g.reshape(1, d), tm, tf, final).reshape(b, s, d)
    return h
```

```python
import functools

import numpy as np
import jax
import jax.numpy as jnp
from jax import lax
from jax.experimental import pallas as pl
from jax.experimental.pallas import tpu as pltpu

F32 = jnp.float32
BF16 = jnp.bfloat16

RMS_EPS = 1e-6
NEG = -1e30
BIG = 1e30

N_HEADS = 16
HEAD_DIM = 64
N_GROUPS = 4
HPG = N_HEADS // N_GROUPS
N_BRANCH = 3
CMP_BLOCK = 32
CMP_STRIDE = 16
SLC_BLOCK = 64
N_SLC_BLOCKS = 8
WINDOW = 512

LANES = 128
POS_LANE = 96
LOG2E = 1.4426950408889634
VMEM_LIMIT = 56 * 1024 * 1024
SB_CHAINS = 8
NSA_CHAINS = 4
SB_LAG = SB_CHAINS
NSA_LAG = 2

_NT = (((1,), (1,)), ((), ()))


def _dot(a, b):
    return jnp.dot(a, b, preferred_element_type=F32)


def _dot_nt(a, b):
    return lax.dot_general(a, b, _NT, preferred_element_type=F32)


def _rmsnorm(x, g):
    ms = jnp.mean(x * x, axis=-1, keepdims=True)
    return (x * lax.rsqrt(ms + RMS_EPS)) * g


def _split_bf16(x):
    hi = x.astype(BF16)
    lo = (x - hi.astype(F32)).astype(BF16)
    return hi, lo


def _neg_abs(x):
    return lax.bitcast_convert_type(lax.bitcast_convert_type(x, jnp.uint32) | jnp.uint32(0x80000000), F32)


def _params(*sem):
    return pltpu.CompilerParams(dimension_semantics=sem, vmem_limit_bytes=VMEM_LIMIT)


def _const_spec(shape):
    nd = len(shape)
    return pl.BlockSpec(shape, lambda *_: (0,) * nd)


def _nsa_proj_kernel(x_ref, g_ref, wm_ref, wc_ref, wg_ref, gb_ref, main_ref, cv_ref, gate_ref, cv_tmp_ref):
    a = _rmsnorm(x_ref[...], g_ref[...]).astype(BF16)
    main_ref[...] = _dot(a, wm_ref[...]).astype(main_ref.dtype)
    cv = _dot(a, wc_ref[...])
    n_chunks = cv.shape[0] // CMP_STRIDE
    for kv in range(2):
        for gp in range(N_GROUPS // 2):
            c0 = (kv * N_GROUPS + 2 * gp) * HEAD_DIM
            cv_tmp_ref[kv, gp] = cv[:, c0:c0 + LANES]
            for l in range(CMP_STRIDE):
                cv_ref[kv, gp, :, l * LANES:(l + 1) * LANES] = cv_tmp_ref[kv, gp,
                                                                          pl.ds(l, n_chunks, stride=CMP_STRIDE), :]
    gate_ref[...] = jax.nn.sigmoid(_dot(a, wg_ref[...]) + gb_ref[...])


def _nsa_proj(h, g, w_main, w_cv, w_gate, gate_b, tm):
    b, s, d = h.shape
    nm, ng = w_main.shape[1], w_gate.shape[1]
    return pl.pallas_call(
        _nsa_proj_kernel,
        grid=(b, s // tm),
        in_specs=[
            pl.BlockSpec((None, tm, d), lambda bi, r: (bi, r, 0)),
            _const_spec((1, d)),
            _const_spec(w_main.shape),
            _const_spec(w_cv.shape),
            _const_spec(w_gate.shape),
            _const_spec((1, ng)),
        ],
        out_specs=[
            pl.BlockSpec((None, tm, nm), lambda bi, r: (bi, r, 0)),
            pl.BlockSpec((None, 2, N_GROUPS // 2, tm // CMP_STRIDE, CMP_STRIDE * LANES),
                         lambda bi, r: (bi, 0, 0, r, 0)),
            pl.BlockSpec((None, tm, ng), lambda bi, r: (bi, r, 0)),
        ],
        out_shape=[
            jax.ShapeDtypeStruct((b, s, nm), BF16),
            jax.ShapeDtypeStruct((b, 2, N_GROUPS // 2, s // CMP_STRIDE, CMP_STRIDE * LANES), F32),
            jax.ShapeDtypeStruct((b, s, ng), F32),
        ],
        scratch_shapes=[pltpu.VMEM((2, N_GROUPS // 2, tm, LANES), F32)],
        compiler_params=_params("parallel", "parallel"),
        name="nsa_proj",
    )(h, g, w_main, w_cv, w_gate, gate_b)


def _sb_proj_kernel(x_ref, g_ref, w_ref, o_ref):
    a = _rmsnorm(x_ref[...], g_ref[...]).astype(BF16)
    o_ref[...] = _dot(a, w_ref[...]).astype(o_ref.dtype)


def _sb_proj(h, g, w, tm):
    b, s, d = h.shape
    n = w.shape[1]
    return pl.pallas_call(
        _sb_proj_kernel,
        grid=(b, s // tm),
        in_specs=[
            pl.BlockSpec((None, tm, d), lambda bi, r: (bi, r, 0)),
            _const_spec((1, d)),
            _const_spec(w.shape),
        ],
        out_specs=pl.BlockSpec((None, tm, n), lambda bi, r: (bi, r, 0)),
        out_shape=jax.ShapeDtypeStruct((b, s, n), BF16),
        compiler_params=_params("parallel", "parallel"),
        name="sb_proj",
    )(h, g, w)


def _nsa_compress_kernel(c_ref, pe_ref, w1_ref, w2_ref, o_ref):
    rows = c_ref.shape[1]
    nc = rows // (N_GROUPS // 2)
    out = [jnp.zeros((rows, LANES), F32) for _ in range(2)]
    for kv in range(2):
        c = c_ref[kv]
        top = (c + pe_ref[kv, 0:1, :]).astype(BF16)
        bot = (c + pe_ref[kv, 1:2, :]).astype(BF16)
        for e in range(2):
            a_top = _dot(top, w1_ref[kv, 0, e])
            a_bot = _dot(bot, w1_ref[kv, 1, e])
            hid = jax.nn.gelu(a_top + pltpu.roll(a_bot, rows - 1, 0), approximate=True)
            out[e] = out[e] + _dot(hid.astype(BF16), w2_ref[kv])
    for gp in range(N_GROUPS // 2):
        for e in range(2):
            g = 2 * gp + e
            o_ref[g * nc:(g + 1) * nc, :] = out[e][gp * nc:(gp + 1) * nc]


def _nsa_compress(c, pe, w1, w2pad):
    b, _, rows, width = c.shape
    return pl.pallas_call(
        _nsa_compress_kernel,
        grid=(b,),
        in_specs=[
            pl.BlockSpec((None, 2, rows, width), lambda bi: (bi, 0, 0, 0)),
            _const_spec(pe.shape),
            _const_spec(w1.shape),
            _const_spec(w2pad.shape),
        ],
        out_specs=pl.BlockSpec((None, 2 * rows, LANES), lambda bi: (bi, 0, 0)),
        out_shape=jax.ShapeDtypeStruct((b, 2 * rows, LANES), F32),
        compiler_params=_params("parallel"),
        name="nsa_compress",
    )(c, pe, w1, w2pad)


def _nsa_attn_kernel(*refs, tq, n_slc, n_sel, n_chain):
    (q_ref, kk_ref, vv_ref, cmp_ref, gate_ref, kpos_ref, cmapt_ref, hconst_ref, hcol_ref, wb_ref, sb_ref,
     o_ref, kaug_s_ref, kaug_w_ref, vts_ref, vtw_ref, qsel_ref, qwin_ref, acc_ref) = refs
    chains = range(n_chain)
    i = pl.program_id(2)
    t0 = i * tq
    tk = tq
    n_kt = kaug_s_ref.shape[1] // tk
    gw = HPG * HEAD_DIM

    @pl.when(i == 0)
    def _():
        top = lax.broadcasted_iota(jnp.int32, (LANES, tk), 0) < HEAD_DIM
        for c in chains:
            kk = kk_ref[:, c * LANES:(c + 1) * LANES].astype(F32)
            first = lax.broadcasted_iota(jnp.int32, kk.shape, 1) < HEAD_DIM
            aux = kpos_ref[...].astype(F32)
            kaug_s_ref[c] = jnp.where(first, kk, aux).astype(BF16)
            kaug_w_ref[c] = jnp.where(first, pltpu.roll(kk, HEAD_DIM, 1), aux).astype(BF16)
            for j in range(n_kt):
                vt = vv_ref[j * tk:(j + 1) * tk, c * LANES:(c + 1) * LANES].astype(F32).T
                vts_ref[c, j] = jnp.where(top, vt, 1.0).astype(BF16)
                vtw_ref[c, j] = jnp.where(top, 1.0, vt).astype(BF16)

    q_t = [q_ref[:, c * gw:(c + 1) * gw].astype(F32).T for c in chains]

    def q_operand(c, h, aux_t):
        return jnp.concatenate([q_t[c][h * HEAD_DIM:(h + 1) * HEAD_DIM], aux_t], axis=0).astype(BF16)

    def aux_rows(c, h):
        return jnp.concatenate([hcol_ref[c, h]] * (tq // LANES), axis=1)

    def scores(kref, qref, c, j0, nt):
        kt = kref[c, pl.ds(pl.multiple_of(j0 * tk, tk), nt * tk), :]
        return _dot(kt, qref[c])

    def pv(vt_ref, c, j0, nt, p):
        acc = _dot(vt_ref[c, j0], p[0:tk])
        for u in range(1, nt):
            acc = acc + _dot(vt_ref[c, j0 + u], p[u * tk:(u + 1) * tk])
        return acc

    def heads(bias):
        return jnp.concatenate([bias] * HPG, axis=1)

    for c in chains:
        for h in range(HPG):
            qwin_ref[c, :, h * tq:(h + 1) * tq] = q_operand(c, h, aux_rows(c, h))
    nwt = min(WINDOW // tk + 1, n_kt)
    jw = jnp.clip(i - (nwt - 1), 0, n_kt - nwt)
    wbias = wb_ref[i - jw]

    nc = cmp_ref.shape[0] // n_chain
    ck = [cmp_ref[c * nc:(c + 1) * nc, :] for c in chains]
    first_c = lax.broadcasted_iota(jnp.int32, (nc, LANES), 1) < HEAD_DIM
    sc_all = [_dot(jnp.where(first_c, ck[c], 0.0).astype(BF16), qwin_ref[c]) for c in chains]
    s_w = [scores(kaug_w_ref, qwin_ref, c, jw, nwt) for c in chains]

    def window_head(c, h):
        s_h = s_w[c][:, h * tq:(h + 1) * tq] + wbias
        p_h = jnp.exp2(s_h - jnp.max(s_h, axis=0, keepdims=True)).astype(BF16)
        acc = pv(vtw_ref, c, jw, nwt, p_h)
        return acc[HEAD_DIM:] * (1.0 / acc[0:1])

    dist_c = ((t0 + lax.broadcasted_iota(jnp.int32, (nc, tq), 1))
              - (lax.broadcasted_iota(jnp.int32, (nc, tq), 0) * CMP_STRIDE + (CMP_BLOCK - 1)))
    valid_c = dist_c >= 0
    dist_cf = dist_c.astype(F32)
    psum, p_heads = [], []
    for c in chains:
        ps = jnp.zeros((nc, tq), F32)
        ph = []
        for h in range(HPG):
            slope = hconst_ref[c, h:h + 1, 0:1]
            s_c = jnp.where(valid_c, sc_all[c][:, h * tq:(h + 1) * tq] - slope * dist_cf, NEG)
            m_c = jnp.max(s_c, axis=0, keepdims=True)
            e_c = jnp.where(valid_c, jnp.exp2(s_c - m_c), 0.0)
            den = jnp.sum(e_c, axis=0, keepdims=True)
            p_c = e_c / jnp.maximum(den, 1e-30)
            ps = ps + p_c
            ph.append(p_c.astype(BF16))
        psum.append(ps)
        p_heads.append(ph)
    o_cmp_t = [_dot(ck[c].T.astype(BF16), jnp.concatenate(p_heads[c], axis=1))[HEAD_DIM:] for c in chains]

    cmt = cmapt_ref[...]
    jp = cmt.shape[0]
    imp_t = []
    for c in chains:
        p_hi, p_lo = _split_bf16(psum[c])
        imp_t.append(_dot(cmt, p_hi) + _dot(cmt, p_lo))
    jb = lax.broadcasted_iota(jnp.int32, (jp, tq), 0)
    cur = jnp.right_shift(t0 + lax.broadcasted_iota(jnp.int32, (jp, tq), 1), SLC_BLOCK.bit_length() - 1)
    causal_j = jb <= cur
    forced = (jb == 0) | (jb == cur) | (jb == cur - 1)
    val = [jnp.where(causal_j & forced, BIG, jnp.where(causal_j, imp_t[c], -BIG)) for c in chains]
    rank = [jnp.zeros((jp, tq), jnp.int32) for c in chains]
    o_win_heads = [[] for c in chains]
    for j in range(n_slc):
        if j % (n_slc // HPG) == 0:
            for c in chains:
                o_win_heads[c].append(window_head(c, j // (n_slc // HPG)))
        for c in chains:
            row = val[c][j:j + 1, :]
            beats = (row > val[c]) | ((row == val[c]) & (jb > j))
            rank[c] = rank[c] + beats.astype(jnp.int32)
    o_win_t = [jnp.concatenate(o_win_heads[c], axis=1) for c in chains]

    for c in chains:
        mb_t = jnp.where(causal_j & (rank[c] < n_sel), 0.0, NEG)
        mb_t = jnp.concatenate([mb_t, jnp.zeros((HEAD_DIM - jp, tq), F32)], axis=0)
        for h in range(HPG):
            qsel_ref[c, :, h * tq:(h + 1) * tq] = q_operand(c, h, mb_t + aux_rows(c, h))
    pair = i // 2
    sbias = heads(sb_ref[i - 2 * pair])
    def sel_chunk(j0, carry):
        s, out = [None] * n_chain, [None] * n_chain

        def stage_scores(c):
            s[c] = scores(kaug_s_ref, qsel_ref, c, j0, 2)
            if carry is None:
                s[c] = s[c] + sbias

        def stage_pv(c):
            m_tile = jnp.max(s[c], axis=0, keepdims=True)
            if carry is None:
                out[c] = m_tile
                acc_ref[c] = pv(vts_ref, c, j0, 2, jnp.exp2(s[c] - m_tile).astype(BF16))
            else:
                out[c] = jnp.maximum(carry[c], m_tile)
                p = jnp.exp2(s[c] - out[c]).astype(BF16)
                acc_ref[c] = jnp.exp2(carry[c] - out[c]) * acc_ref[c] + pv(vts_ref, c, j0, 2, p)

        for step in range(n_chain + NSA_LAG):
            if step < n_chain:
                stage_scores(step)
            if 0 <= step - NSA_LAG < n_chain:
                stage_pv(step - NSA_LAG)
        return tuple(out)

    lax.fori_loop(0, pair, lambda jj, carry: sel_chunk(2 * (pair - 1 - jj), carry), sel_chunk(2 * pair, None))

    gt_t = gate_ref[...].T
    for c in chains:
        acc_s = acc_ref[c]
        o_sel_t = acc_s[:HEAD_DIM] * (1.0 / acc_s[HEAD_DIM:HEAD_DIM + 1])
        rows = []
        for h in range(HPG):
            sl = slice(h * tq, (h + 1) * tq)
            g_cmp, g_sel, g_win = (gt_t[br * N_HEADS + c * HPG + h:br * N_HEADS + c * HPG + h + 1]
                                   for br in range(N_BRANCH))
            rows.append(g_cmp * o_cmp_t[c][:, sl] + g_sel * o_sel_t[:, sl] + g_win * o_win_t[c][:, sl])
        o_ref[:, c * gw:(c + 1) * gw] = jnp.concatenate(rows, axis=0).T.astype(o_ref.dtype)


def _nsa_attn(main, cmp, gates, kpos, cmapt, hconst, hcol, wbias, sbias, tq):
    b, s, _ = main.shape
    assert (s // tq) % 2 == 0 and NSA_CHAINS == N_GROUPS
    nch = NSA_CHAINS
    nc = cmp.shape[1] // N_GROUPS
    n_slc = s // SLC_BLOCK
    gw = HPG * HEAD_DIM
    qw, kvw = N_GROUPS * gw, N_GROUPS * LANES
    assert qw % kvw == 0

    kern =functools.partial(_nsa_attn_kernel, tq=tq, n_slc=n_slc, n_sel=min(N_SLC_BLOCKS, n_slc), n_chain=nch)
    return pl.pallas_call(
        kern,
        grid=(b, N_GROUPS // nch, s // tq),
        in_specs=(
            [
                pl.BlockSpec((None, tq, qw), lambda bi, gp, i: (bi, i, 0)),
                pl.BlockSpec((None, s, kvw), lambda bi, gp, i: (bi, 0, qw // kvw)),
                pl.BlockSpec((None, s, kvw), lambda bi, gp, i: (bi, 0, qw // kvw + 1)),
            ]
            + [
                pl.BlockSpec((None, nch * nc, LANES), lambda bi, gp, i: (bi, gp, 0)),
                pl.BlockSpec((None, tq, LANES), lambda bi, gp, i: (bi, i, 0)),
                _const_spec(kpos.shape),
                _const_spec(cmapt.shape),
                pl.BlockSpec((nch, 2 * HPG, LANES), lambda bi, gp, i: (gp, 0, 0)),
                pl.BlockSpec((nch, HPG, HEAD_DIM, LANES), lambda bi, gp, i: (gp, 0, 0, 0)),
                _const_spec(wbias.shape),
                _const_spec(sbias.shape),
            ]),
        out_specs=pl.BlockSpec((None, tq, nch * gw), lambda bi, gp, i: (bi, i, gp)),
        out_shape=jax.ShapeDtypeStruct((b, s, N_GROUPS * gw), BF16),
        scratch_shapes=[
            pltpu.VMEM((nch, s, LANES), BF16),
            pltpu.VMEM((nch, s, LANES), BF16),
            pltpu.VMEM((nch, s // tq, LANES, tq), BF16),
            pltpu.VMEM((nch, s // tq, LANES, tq), BF16),
            pltpu.VMEM((nch, LANES, HPG * tq), BF16),
            pltpu.VMEM((nch, LANES, HPG * tq), BF16),
            pltpu.VMEM((nch, LANES, HPG * tq), F32),
        ],
        compiler_params=_params("parallel", "parallel", "arbitrary"),
        name="nsa_attn",
    )(main, main, main, cmp, gates, kpos, cmapt, hconst, hcol, wbias, sbias)


def _sb_attn_kernel(q_ref, k_ref, v_ref, tri_ref, o_ref, vt_ref, acc_ref, *, tq):
    i = pl.program_id(2)
    tk = tq
    n_kt = k_ref.shape[0] // tk
    n_chain = q_ref.shape[1] // LANES

    @pl.when(i == 0)
    def _():
        for c in range(n_chain):
            for j in range(n_kt):
                vt_ref[c, j] = v_ref[j * tk:(j + 1) * tk, c * LANES:(c + 1) * LANES].astype(F32).T.astype(BF16)

    low = lax.broadcasted_iota(jnp.int32, (tq, LANES), 1) < HEAD_DIM
    before = (lax.broadcasted_iota(jnp.int32, (tk, 2 * tq), 0)
              < (lax.broadcasted_iota(jnp.int32, (tk, 2 * tq), 1) & (tq - 1)))
    tri = tri_ref[...]

    qst = []
    for c in range(n_chain):
        q2 = q_ref[:, c * LANES:(c + 1) * LANES].astype(F32)
        qst.append(jnp.concatenate([jnp.where(low, q2, 0.0), jnp.where(low, 0.0, q2)], axis=0).astype(BF16))

    chains = range(n_chain)

    def tiles(j, diag, carry):
        off = pl.multiple_of(j * tk, tk)
        z, sp, sums = [None] * n_chain, [None] * n_chain, [None] * n_chain

        def scores(c):
            z[c] = _dot_nt(k_ref[pl.ds(off, tk), c * LANES:(c + 1) * LANES], qst[c])
            sp[c] = jnp.maximum(z[c], 0.0) + jnp.log(1.0 + jnp.exp2(_neg_abs(z[c]))) * LOG2E
            if diag:
                sp[c] = jnp.where(before, sp[c], 0.0)

        def suffix(c):
            sums[c] = _dot(tri, sp[c].astype(BF16))

        def weights(c):
            if diag:
                a = jnp.where(before, jnp.exp2(z[c] - sp[c] + sums[c][:tk]), 0.0)
            else:
                a = jnp.exp2(z[c] - sp[c] + sums[c][:tk] + carry[c])
            pv = _dot(vt_ref[c, j], a.astype(BF16))
            if diag:
                acc_ref[c] = pv
            else:
                acc_ref[c] += pv

        for step in range(n_chain + 2 * SB_LAG):
            for d, stage in enumerate((scores, suffix, weights)):
                if 0 <= step - d * SB_LAG < n_chain:
                    stage(step - d * SB_LAG)
        if diag:
            return tuple(sums[c][tk:tk + 1] for c in chains)
        return tuple(carry[c] + sums[c][tk:tk + 1] for c in chains)

    res = lax.fori_loop(0, i, lambda jj, carry: tiles(i - 1 - jj, False, carry), tiles(i, True, None))
    for c in range(n_chain):
        acc = acc_ref[c]
        out_t = jnp.concatenate([acc[:HEAD_DIM, :tq], acc[HEAD_DIM:, tq:]], axis=0)
        o_ref[:, c * LANES:(c + 1) * LANES] = out_t.T.astype(o_ref.dtype)


def _sb_attn(qkv, tri, tq):
    b, s, n3 = qkv.shape
    w = LANES * SB_CHAINS
    nblk = n3 // 3 // w
    return pl.pallas_call(
        functools.partial(_sb_attn_kernel, tq=tq),
        grid=(b, nblk, s // tq),
        in_specs=[
            pl.BlockSpec((None, tq, w), lambda bi, p, i: (bi, i, p)),
            pl.BlockSpec((None, s, w), lambda bi, p, i: (bi, 0, nblk + p)),
            pl.BlockSpec((None, s, w), lambda bi, p, i: (bi, 0, 2 * nblk + p)),
            _const_spec(tri.shape),
        ],
        out_specs=pl.BlockSpec((None, tq, w), lambda bi, p, i: (bi, i, p)),
        out_shape=jax.ShapeDtypeStruct((b, s, nblk * w), BF16),
        scratch_shapes=[pltpu.VMEM((SB_CHAINS, s // tq, LANES, tq), BF16),
                        pltpu.VMEM((SB_CHAINS, LANES, 2 * tq), F32)],
        compiler_params=_params("parallel", "parallel", "arbitrary"),
        name="sb_attn",
    )(qkv, qkv, qkv, tri)


def _mix_ffn_kernel(h_ref, o_ref, wo_ref, g_ref, w1_ref, w3_ref, w2_ref, fg_ref, out_ref, a_ref, acc_ref,
                    *, final, tf):
    h1 = h_ref[...] + _dot(o_ref[...], wo_ref[...])
    a_ref[...] = _rmsnorm(h1, g_ref[...]).astype(BF16)
    acc_ref[...] = h1

    for c in range(w1_ref.shape[1] // tf):
        cols = slice(c * tf, (c + 1) * tf)
        a = a_ref[...]
        u = _dot(a, w1_ref[:, cols])
        v = _dot(a, w3_ref[:, cols])
        t = (u * jax.nn.sigmoid(u)) * v
        acc_ref[...] += _dot(t.astype(BF16), w2_ref[cols, :])
    h2 = acc_ref[...]
    out_ref[...] = _rmsnorm(h2, fg_ref[...]) if final else h2


def _mix_ffn(h, o, wo, g, w1, w3, w2, fg, tm, tf, final):
    r, d = h.shape
    single = pl.Buffered(1)

    def wspec(shape):
        nd = len(shape)
        return pl.BlockSpec(shape, lambda *_: (0,) * nd, pipeline_mode=single)

    return pl.pallas_call(
        functools.partial(_mix_ffn_kernel, final=final, tf=tf),
        grid=(r // tm,),
        in_specs=[
            pl.BlockSpec((tm, d), lambda i: (i, 0)),
            pl.BlockSpec((tm, d), lambda i: (i, 0)),
            wspec(wo.shape),
            _const_spec((1, d)),
            wspec(w1.shape),
            wspec(w3.shape),
            wspec(w2.shape),
            _const_spec((1, d)),
        ],
        out_specs=pl.BlockSpec((tm, d), lambda i: (i, 0)),
        out_shape=jax.ShapeDtypeStruct((r, d), F32),
        scratch_shapes=[pltpu.VMEM((tm, d), BF16), pltpu.VMEM((tm, d), F32)],
        compiler_params=_params("parallel"),
        name="mix_ffn",
    )(h, o, wo, g, w1, w3, w2, fg)


def _nsa_layout(d):
    kvw = N_GROUPS * HEAD_DIM
    off = {"q": 0, "kc": d, "vc": d + kvw, "ksl": d + 2 * kvw, "vsl": d + 3 * kvw, "kw": d + 4 * kvw,
           "vw": d + 5 * kvw, "gt": d + 6 * kvw}
    main = [np.arange(d)]
    for pair in (("ksl", "kw"), ("vsl", "vw")):
        for g in range(N_GROUPS):
            for name in pair:
                main.append(off[name] + g * HEAD_DIM + np.arange(HEAD_DIM))
    main = np.concatenate(main)
    cv = off["kc"] + np.arange(2 * kvw)
    scale = np.where(main < d, HEAD_DIM ** -0.5 * LOG2E, 1.0).astype(np.float32)
    return main, scale, cv, off["gt"]


def _bf16_pieces(x):
    x = np.asarray(x, np.float32)
    out = []
    for _ in range(3):
        p = x.astype(BF16).astype(np.float32)
        out.append(p)
        x = (x - p).astype(np.float32)
    return out


def _nsa_tables(s, tk):
    n_slc = s // SLC_BLOCK
    n_cmp = s // CMP_STRIDE - CMP_BLOCK // CMP_STRIDE + 1
    nc = s // CMP_STRIDE
    pos = np.arange(s)
    kpos = np.zeros((s, LANES), np.float32)
    assert n_slc <= POS_LANE - HEAD_DIM
    kpos[pos, HEAD_DIM + pos // SLC_BLOCK] = 1.0
    for c in range(3):
        kpos[:, POS_LANE + c] = (pos // tk) * tk
        kpos[:, POS_LANE + 3 + c] = pos % tk
    cs = np.arange(n_cmp) * CMP_STRIDE
    ss = np.arange(n_slc) * SLC_BLOCK
    ov = np.clip(np.minimum(cs[:, None] + CMP_BLOCK, ss[None] + SLC_BLOCK) - np.maximum(cs[:, None], ss[None]), 0, None)
    jp = -(-n_slc // 16) * 16
    cmapt = np.zeros((jp, nc), np.float32)
    cmapt[:n_slc, :n_cmp] = (ov / CMP_STRIDE).T
    slopes = np.asarray(2.0 ** (-8.0 * (np.arange(N_HEADS) + 1) / N_HEADS), np.float32)
    slopes = (slopes.astype(np.float64) * LOG2E).astype(np.float32)
    pieces = _bf16_pieces(slopes)
    hconst = np.zeros((N_GROUPS, 2 * HPG, LANES), np.float32)
    hcol = np.zeros((N_GROUPS, HPG, HEAD_DIM, LANES), np.float32)
    for g in range(N_GROUPS):
        for hh in range(HPG):
            for c in range(3):
                hcol[g, hh, POS_LANE - HEAD_DIM + c, :] = pieces[c][g * HPG + hh]
                hcol[g, hh, POS_LANE - HEAD_DIM + 3 + c, :] = pieces[c][g * HPG + hh]
            hconst[g, hh, :] = slopes[g * HPG + hh]
    n_kt = s // tk
    nwt = min(WINDOW // tk + 1, n_kt)
    tt = np.arange(tk)[None, :]

    def dist(r, nt):
        return r * tk + tt - np.arange(nt * tk)[:, None]

    wbias = np.stack([np.where((dist(r, nwt) >= 0) & (dist(r, nwt) < WINDOW), 0.0, NEG) for r in range(nwt)])
    sbias = np.stack([np.where(dist(r, 2) >= 0, 0.0, NEG) for r in range(2)])
    return (jnp.asarray(kpos, BF16), jnp.asarray(cmapt, BF16), jnp.asarray(hconst, F32), jnp.asarray(hcol, F32),
            jnp.asarray(wbias, F32), jnp.asarray(sbias, F32))


def _ffn_chunk(d_ff):
    for tf in (512, 256, 128):
        if d_ff % tf == 0:
            return tf
    return d_ff


def kernel(x, mix_norm_g, ffn_norm_g, final_norm_g, nsa_w_in, nsa_gate_b, nsa_cmp_pe, nsa_cmp_w1, nsa_cmp_w2,
           nsa_w_out, sb_w_in, sb_w_out, ffn_w1, ffn_w3, ffn_w2):
    b, s, d = x.shape
    depth = mix_norm_g.shape[0]
    d_ff = ffn_w1.shape[2]
    tm = min(512, s)
    tm_proj = min(1024, s)
    tq = min(256, s)
    tf = _ffn_chunk(d_ff)

    main_idx, main_scale, cv_idx, gt_off = _nsa_layout(d)
    sb_scale = np.where(np.arange(3 * d) < d, HEAD_DIM ** -0.5 * LOG2E, 1.0).astype(np.float32)
    n_gate = N_BRANCH * N_HEADS
    kpos, cmapt, hconst, hcol, wbias, sbias = _nsa_tables(s, tq)
    tri = jnp.asarray(np.concatenate([-np.triu(np.ones((tq, tq), np.float32), 1),
                                      -np.ones((16, tq), np.float32)], axis=0), BF16)
    chunk = CMP_STRIDE * HEAD_DIM
    nc = s // CMP_STRIDE

    h = x
    for i in range(depth):
        j = i // 2
        gm = mix_norm_g[i].reshape(1, d)
        if i % 2 == 0:
            w_in = nsa_w_in[j]
            w_main = (w_in[:, main_idx] * main_scale).astype(BF16)
            w_cv = w_in[:, cv_idx].astype(BF16)
            w_gate = jnp.pad(w_in[:, gt_off:gt_off + n_gate], ((0, 0), (0, LANES - n_gate))).astype(BF16)
            gate_b = jnp.pad(nsa_gate_b[j], (0, LANES - n_gate)).reshape(1, LANES)
            main, cv, gates = _nsa_proj(h, gm, w_main, w_cv, w_gate, gate_b, tm_proj)
            c = cv.reshape(b, 2, N_GROUPS // 2 * nc, 2 * chunk)
            pe = jnp.broadcast_to(nsa_cmp_pe[j].reshape(2, 2, CMP_STRIDE, 1, HEAD_DIM),
                                  (2, 2, CMP_STRIDE, 2, HEAD_DIM)).reshape(2, 2, 2 * chunk)
            w1 = nsa_cmp_w1[j].reshape(2, 2, CMP_STRIDE, 1, HEAD_DIM, -1)
            w1 = jnp.stack([jnp.concatenate([w1, jnp.zeros_like(w1)], axis=3),
                            jnp.concatenate([jnp.zeros_like(w1), w1], axis=3)], axis=2)
            w1 = w1.reshape(2, 2, 2, 2 * chunk, -1)
            w2pad = jnp.zeros((2, nsa_cmp_w2.shape[2], LANES), F32)
            w2pad = w2pad.at[0, :, :HEAD_DIM].set(nsa_cmp_w2[j, 0]).at[1, :, HEAD_DIM:].set(nsa_cmp_w2[j, 1])
            cmp = _nsa_compress(c, pe, w1.astype(BF16), w2pad.astype(BF16))
            o = _nsa_attn(main, cmp, gates, kpos, cmapt, hconst, hcol, wbias, sbias, tq)
            w_out = nsa_w_out[j]
        else:
            qkv = _sb_proj(h, gm, (sb_w_in[j] * sb_scale).astype(BF16), tm_proj)
            o = _sb_attn(qkv, tri, tq)
            w_out = sb_w_out[j]
        final = i == depth - 1
        h = _mix_ffn(h.reshape(b * s, d), o.reshape(b * s, d), w_out.astype(BF16), ffn_norm_g[i].reshape(1, d),
                     ffn_w1[i].astype(BF16), ffn_w3[i].astype(BF16), ffn_w2[i].astype(BF16),
                     final_norm_g.reshape(1, d), tm, tf, final).reshape(b, s, d)
    return h
```

```python
import functools

import numpy as np
import jax
import jax.numpy as jnp
from jax import lax
from jax.experimental import pallas as pl
from jax.experimental.pallas import tpu as pltpu

F32 = jnp.float32
BF16 = jnp.bfloat16

RMS_EPS = 1e-6
NEG = -1e30
BIG = 1e30

N_HEADS = 16
HEAD_DIM = 64
N_GROUPS = 4
HPG = N_HEADS // N_GROUPS
N_BRANCH = 3
CMP_BLOCK = 32
CMP_STRIDE = 16
SLC_BLOCK = 64
N_SLC_BLOCKS = 8
WINDOW = 512

LANES = 128
POS_LANE = 96
LOG2E = 1.4426950408889634
VMEM_LIMIT = 56 * 1024 * 1024
SB_CHAINS = 8
NSA_CHAINS = 4
SB_LAG = SB_CHAINS
NSA_LAG = 2

_NT = (((1,), (1,)), ((), ()))


def _dot(a, b):
    return jnp.dot(a, b, preferred_element_type=F32)


def _dot_nt(a, b):
    return lax.dot_general(a, b, _NT, preferred_element_type=F32)


def _rmsnorm(x, g):
    ms = jnp.mean(x * x, axis=-1, keepdims=True)
    return (x * lax.rsqrt(ms + RMS_EPS)) * g


def _split_bf16(x):
    hi = x.astype(BF16)
    lo = (x - hi.astype(F32)).astype(BF16)
    return hi, lo


def _neg_abs(x):
    return lax.bitcast_convert_type(lax.bitcast_convert_type(x, jnp.uint32) | jnp.uint32(0x80000000), F32)


def _params(*sem):
    return pltpu.CompilerParams(dimension_semantics=sem, vmem_limit_bytes=VMEM_LIMIT)


def _const_spec(shape):
    nd = len(shape)
    return pl.BlockSpec(shape, lambda *_: (0,) * nd)


def _nsa_proj_kernel(x_ref, g_ref, wm_ref, wc_ref, wg_ref, gb_ref, main_ref, cv_ref, gate_ref, cv_tmp_ref):
    a = _rmsnorm(x_ref[...], g_ref[...]).astype(BF16)
    main_ref[...] = _dot(a, wm_ref[...]).astype(main_ref.dtype)
    cv = _dot(a, wc_ref[...])
    n_chunks = cv.shape[0] // CMP_STRIDE
    for kv in range(2):
        for gp in range(N_GROUPS // 2):
            c0 = (kv * N_GROUPS + 2 * gp) * HEAD_DIM
            cv_tmp_ref[kv, gp] = cv[:, c0:c0 + LANES]
            for l in range(CMP_STRIDE):
                cv_ref[kv, gp, :, l * LANES:(l + 1) * LANES] = cv_tmp_ref[kv, gp,
                                                                          pl.ds(l, n_chunks, stride=CMP_STRIDE), :]
    gate_ref[...] = jax.nn.sigmoid(_dot(a, wg_ref[...]) + gb_ref[...])


def _nsa_proj(h, g, w_main, w_cv, w_gate, gate_b, tm):
    b, s, d = h.shape
    nm, ng = w_main.shape[1], w_gate.shape[1]
    return pl.pallas_call(
        _nsa_proj_kernel,
        grid=(b, s // tm),
        in_specs=[
            pl.BlockSpec((None, tm, d), lambda bi, r: (bi, r, 0)),
            _const_spec((1, d)),
            _const_spec(w_main.shape),
            _const_spec(w_cv.shape),
            _const_spec(w_gate.shape),
            _const_spec((1, ng)),
        ],
        out_specs=[
            pl.BlockSpec((None, tm, nm), lambda bi, r: (bi, r, 0)),
            pl.BlockSpec((None, 2, N_GROUPS // 2, tm // CMP_STRIDE, CMP_STRIDE * LANES),
                         lambda bi, r: (bi, 0, 0, r, 0)),
            pl.BlockSpec((None, tm, ng), lambda bi, r: (bi, r, 0)),
        ],
        out_shape=[
            jax.ShapeDtypeStruct((b, s, nm), BF16),
            jax.ShapeDtypeStruct((b, 2, N_GROUPS // 2, s // CMP_STRIDE, CMP_STRIDE * LANES), F32),
            jax.ShapeDtypeStruct((b, s, ng), F32),
        ],
        scratch_shapes=[pltpu.VMEM((2, N_GROUPS // 2, tm, LANES), F32)],
        compiler_params=_params("parallel", "parallel"),
        name="nsa_proj",
    )(h, g, w_main, w_cv, w_gate, gate_b)


def _sb_proj_kernel(x_ref, g_ref, w_ref, o_ref):
    a = _rmsnorm(x_ref[...], g_ref[...]).astype(BF16)
    o_ref[...] = _dot(a, w_ref[...]).astype(o_ref.dtype)


def _sb_proj(h, g, w, tm):
    b, s, d = h.shape
    n = w.shape[1]
    return pl.pallas_call(
        _sb_proj_kernel,
        grid=(b, s // tm),
        in_specs=[
            pl.BlockSpec((None, tm, d), lambda bi, r: (bi, r, 0)),
            _const_spec((1, d)),
            _const_spec(w.shape),
        ],
        out_specs=pl.BlockSpec((None, tm, n), lambda bi, r: (bi, r, 0)),
        out_shape=jax.ShapeDtypeStruct((b, s, n), BF16),
        compiler_params=_params("parallel", "parallel"),
        name="sb_proj",
    )(h, g, w)


def _nsa_compress_kernel(c_ref, pe_ref, w1_ref, w2_ref, o_ref):
    rows = c_ref.shape[1]
    nc = rows // (N_GROUPS // 2)
    out = [jnp.zeros((rows, LANES), F32) for _ in range(2)]
    for kv in range(2):
        c = c_ref[kv]
        top = (c + pe_ref[kv, 0:1, :]).astype(BF16)
        bot = (c + pe_ref[kv, 1:2, :]).astype(BF16)
        for e in range(2):
            a_top = _dot(top, w1_ref[kv, 0, e])
            a_bot = _dot(bot, w1_ref[kv, 1, e])
            hid = jax.nn.gelu(a_top + pltpu.roll(a_bot, rows - 1, 0), approximate=True)
            out[e] = out[e] + _dot(hid.astype(BF16), w2_ref[kv])
    for gp in range(N_GROUPS // 2):
        for e in range(2):
            g = 2 * gp + e
            o_ref[g * nc:(g + 1) * nc, :] = out[e][gp * nc:(gp + 1) * nc]


def _nsa_compress(c, pe, w1, w2pad):
    b, _, rows, width = c.shape
    return pl.pallas_call(
        _nsa_compress_kernel,
        grid=(b,),
        in_specs=[
            pl.BlockSpec((None, 2, rows, width), lambda bi: (bi, 0, 0, 0)),
            _const_spec(pe.shape),
            _const_spec(w1.shape),
            _const_spec(w2pad.shape),
        ],
        out_specs=pl.BlockSpec((None, 2 * rows, LANES), lambda bi: (bi, 0, 0)),
        out_shape=jax.ShapeDtypeStruct((b, 2 * rows, LANES), F32),
        compiler_params=_params("parallel"),
        name="nsa_compress",
    )(c, pe, w1, w2pad)


def _nsa_attn_kernel(*refs, tq, n_slc, n_sel, n_chain):
    (q_ref, kk_ref, vv_ref, cmp_ref, gate_ref, kpos_ref, cmapt_ref, hconst_ref, hcol_ref, wb_ref, sb_ref,
     o_ref, kaug_s_ref, kaug_w_ref, vts_ref, vtw_ref, qsel_ref, qwin_ref, acc_ref) = refs
    chains = range(n_chain)
    i = pl.program_id(2)
    t0 = i * tq
    tk = tq
    n_kt = kaug_s_ref.shape[1] // tk
    gw = HPG * HEAD_DIM

    @pl.when(i == 0)
    def _():
        top = lax.broadcasted_iota(jnp.int32, (LANES, tk), 0) < HEAD_DIM
        for c in chains:
            kk = kk_ref[:, c * LANES:(c + 1) * LANES].astype(F32)
            first = lax.broadcasted_iota(jnp.int32, kk.shape, 1) < HEAD_DIM
            aux = kpos_ref[...].astype(F32)
            kaug_s_ref[c] = jnp.where(first, kk, aux).astype(BF16)
            kaug_w_ref[c] = jnp.where(first, pltpu.roll(kk, HEAD_DIM, 1), aux).astype(BF16)
            for j in range(n_kt):
                vt = vv_ref[j * tk:(j + 1) * tk, c * LANES:(c + 1) * LANES].astype(F32).T
                vts_ref[c, j] = jnp.where(top, vt, 1.0).astype(BF16)
                vtw_ref[c, j] = jnp.where(top, 1.0, vt).astype(BF16)

    q_t = [q_ref[:, c * gw:(c + 1) * gw].astype(F32).T for c in chains]

    def q_operand(c, h, aux_t):
        return jnp.concatenate([q_t[c][h * HEAD_DIM:(h + 1) * HEAD_DIM], aux_t], axis=0).astype(BF16)

    def aux_rows(c, h):
        return jnp.concatenate([hcol_ref[c, h]] * (tq // LANES), axis=1)

    def scores(kref, qref, c, j0, nt):
        kt = kref[c, pl.ds(pl.multiple_of(j0 * tk, tk), nt * tk), :]
        return _dot(kt, qref[c])

    def pv(vt_ref, c, j0, nt, p):
        acc = _dot(vt_ref[c, j0], p[0:tk])
        for u in range(1, nt):
            acc = acc + _dot(vt_ref[c, j0 + u], p[u * tk:(u + 1) * tk])
        return acc

    def heads(bias):
        return jnp.concatenate([bias] * HPG, axis=1)

    for c in chains:
        for h in range(HPG):
            qwin_ref[c, :, h * tq:(h + 1) * tq] = q_operand(c, h, aux_rows(c, h))
    nwt = min(WINDOW // tk + 1, n_kt)
    jw = jnp.clip(i - (nwt - 1), 0, n_kt - nwt)
    wbias = wb_ref[i - jw]

    nc = cmp_ref.shape[0] // n_chain
    ck = [cmp_ref[c * nc:(c + 1) * nc, :] for c in chains]
    first_c = lax.broadcasted_iota(jnp.int32, (nc, LANES), 1) < HEAD_DIM
    sc_all = [_dot(jnp.where(first_c, ck[c], 0.0).astype(BF16), qwin_ref[c]) for c in chains]
    s_w = [scores(kaug_w_ref, qwin_ref, c, jw, nwt) for c in chains]

    def window_head(c, h):
        s_h = s_w[c][:, h * tq:(h + 1) * tq] + wbias
        p_h = jnp.exp2(s_h - jnp.max(s_h, axis=0, keepdims=True)).astype(BF16)
        acc = pv(vtw_ref, c, jw, nwt, p_h)
        return acc[HEAD_DIM:] * (1.0 / acc[0:1])

    dist_c = ((t0 + lax.broadcasted_iota(jnp.int32, (nc, tq), 1))
              - (lax.broadcasted_iota(jnp.int32, (nc, tq), 0) * CMP_STRIDE + (CMP_BLOCK - 1)))
    valid_c = dist_c >= 0
    dist_cf = dist_c.astype(F32)
    psum, p_heads = [], []
    for c in chains:
        ps = jnp.zeros((nc, tq), F32)
        ph = []
        for h in range(HPG):
            slope = hconst_ref[c, h:h + 1, 0:1]
            s_c = jnp.where(valid_c, sc_all[c][:, h * tq:(h + 1) * tq] - slope * dist_cf, NEG)
            m_c = jnp.max(s_c, axis=0, keepdims=True)
            e_c = jnp.where(valid_c, jnp.exp2(s_c - m_c), 0.0)
            den = jnp.sum(e_c, axis=0, keepdims=True)
            p_c = e_c / jnp.maximum(den, 1e-30)
            ps = ps + p_c
            ph.append(p_c.astype(BF16))
        psum.append(ps)
        p_heads.append(ph)
    o_cmp_t = [_dot(ck[c].T.astype(BF16), jnp.concatenate(p_heads[c], axis=1))[HEAD_DIM:] for c in chains]

    cmt = cmapt_ref[...]
    jp = cmt.shape[0]
    imp_t = []
    for c in chains:
        p_hi, p_lo = _split_bf16(psum[c])
        imp_t.append(_dot(cmt, p_hi) + _dot(cmt, p_lo))
    jb = lax.broadcasted_iota(jnp.int32, (jp, tq), 0)
    cur = jnp.right_shift(t0 + lax.broadcasted_iota(jnp.int32, (jp, tq), 1), SLC_BLOCK.bit_length() - 1)
    causal_j = jb <= cur
    forced = (jb == 0) | (jb == cur) | (jb == cur - 1)
    val = [jnp.where(causal_j & forced, BIG, jnp.where(causal_j, imp_t[c], -BIG)) for c in chains]
    rank = [jnp.zeros((jp, tq), jnp.int32) for c in chains]
    o_win_heads = [[] for c in chains]
    for j in range(n_slc):
        if j % (n_slc // HPG) == 0:
            for c in chains:
                o_win_heads[c].append(window_head(c, j // (n_slc // HPG)))
        for c in chains:
            row = val[c][j:j + 1, :]
            beats = (row > val[c]) | ((row == val[c]) & (jb > j))
            rank[c] = rank[c] + beats.astype(jnp.int32)
    o_win_t = [jnp.concatenate(o_win_heads[c], axis=1) for c in chains]

    for c in chains:
        mb_t = jnp.where(causal_j & (rank[c] < n_sel), 0.0, NEG)
        mb_t = jnp.concatenate([mb_t, jnp.zeros((HEAD_DIM - jp, tq), F32)], axis=0)
        for h in range(HPG):
            qsel_ref[c, :, h * tq:(h + 1) * tq] = q_operand(c, h, mb_t + aux_rows(c, h))
    pair = i // 2
    sbias = heads(sb_ref[i - 2 * pair])

    def sel_chunk(j0, carry):
        s, out = [None] * n_chain, [None] * n_chain

        def stage_scores(c):
            s[c] = scores(kaug_s_ref, qsel_ref, c, j0, 2)
            if carry is None:
                s[c] = s[c] + sbias

        def stage_pv(c):
            m_tile = jnp.max(s[c], axis=0, keepdims=True)
            if carry is None:
                out[c] = m_tile
                acc_ref[c] = pv(vts_ref, c, j0, 2, jnp.exp2(s[c] - m_tile).astype(BF16))
            else:
                out[c] = jnp.maximum(carry[c], m_tile)
                p = jnp.exp2(s[c] - out[c]).astype(BF16)
                acc_ref[c] = jnp.exp2(carry[c] - out[c]) * acc_ref[c] + pv(vts_ref, c, j0, 2, p)

        for step in range(n_chain + NSA_LAG):
            if step < n_chain:
                stage_scores(step)
            if 0 <= step - NSA_LAG < n_chain:
                stage_pv(step - NSA_LAG)
        return tuple(out)

    lax.fori_loop(0, pair, lambda jj, carry: sel_chunk(2 * (pair - 1 - jj), carry), sel_chunk(2 * pair, None))

    gt_t = gate_ref[...].T
    for c in chains:
        acc_s = acc_ref[c]
        o_sel_t = acc_s[:HEAD_DIM] * (1.0 / acc_s[HEAD_DIM:HEAD_DIM + 1])
        rows = []
        for h in range(HPG):
            sl = slice(h * tq, (h + 1) * tq)
            g_cmp, g_sel, g_win = (gt_t[br * N_HEADS + c * HPG + h:br * N_HEADS + c * HPG + h + 1]
                                   for br in range(N_BRANCH))
            rows.append(g_cmp * o_cmp_t[c][:, sl] + g_sel * o_sel_t[:, sl] + g_win * o_win_t[c][:, sl])
        o_ref[:, c * gw:(c + 1) * gw] = jnp.concatenate(rows, axis=0).T.astype(o_ref.dtype)


def _nsa_attn(main, cmp, gates, kpos, cmapt, hconst, hcol, wbias, sbias, tq):
    b, s, _ = main.shape
    assert (s // tq) % 2 == 0 and NSA_CHAINS == N_GROUPS
    nch = NSA_CHAINS
    nc = cmp.shape[1] // N_GROUPS
    n_slc = s // SLC_BLOCK
    gw = HPG * HEAD_DIM
    qw, kvw = N_GROUPS * gw, N_GROUPS * LANES
    assert qw % kvw == 0

    kern =functools.partial(_nsa_attn_kernel, tq=tq, n_slc=n_slc, n_sel=min(N_SLC_BLOCKS, n_slc), n_chain=nch)
    return pl.pallas_call(
        kern,
        grid=(b, N_GROUPS // nch, s // tq),
        in_specs=(
            [
                pl.BlockSpec((None, tq, qw), lambda bi, gp, i: (bi, i, 0)),
                pl.BlockSpec((None, s, kvw), lambda bi, gp, i: (bi, 0, qw // kvw)),
                pl.BlockSpec((None, s, kvw), lambda bi, gp, i: (bi, 0, qw // kvw + 1)),
            ]
            + [
                pl.BlockSpec((None, nch * nc, LANES), lambda bi, gp, i: (bi, gp, 0)),
                pl.BlockSpec((None, tq, LANES), lambda bi, gp, i: (bi, i, 0)),
                _const_spec(kpos.shape),
                _const_spec(cmapt.shape),
                pl.BlockSpec((nch, 2 * HPG, LANES), lambda bi, gp, i: (gp, 0, 0)),
                pl.BlockSpec((nch, HPG, HEAD_DIM, LANES), lambda bi, gp, i: (gp, 0, 0, 0)),
                _const_spec(wbias.shape),
                _const_spec(sbias.shape),
            ]),
        out_specs=pl.BlockSpec((None, tq, nch * gw), lambda bi, gp, i: (bi, i, gp)),
        out_shape=jax.ShapeDtypeStruct((b, s, N_GROUPS * gw), BF16),
        scratch_shapes=[
            pltpu.VMEM((nch, s, LANES), BF16),
            pltpu.VMEM((nch, s, LANES), BF16),
            pltpu.VMEM((nch, s // tq, LANES, tq), BF16),
            pltpu.VMEM((nch, s // tq, LANES, tq), BF16),
            pltpu.VMEM((nch, LANES, HPG * tq), BF16),
            pltpu.VMEM((nch, LANES, HPG * tq), BF16),
            pltpu.VMEM((nch, LANES, HPG * tq), F32),
        ],
        compiler_params=_params("parallel", "parallel", "arbitrary"),
        name="nsa_attn",
    )(main, main, main, cmp, gates, kpos, cmapt, hconst, hcol, wbias, sbias)


def _sb_attn_kernel(q_ref, k_ref, v_ref, tri_ref, o_ref, vt_ref, acc_ref, *, tq):
    i = pl.program_id(2)
    tk = tq
    n_kt = k_ref.shape[0] // tk
    n_chain = q_ref.shape[1] // LANES

    @pl.when(i == 0)
    def _():
        for c in range(n_chain):
            for j in range(n_kt):
                vt_ref[c, j] = v_ref[j * tk:(j + 1) * tk, c * LANES:(c + 1) * LANES].astype(F32).T.astype(BF16)

    low = lax.broadcasted_iota(jnp.int32, (tq, LANES), 1) < HEAD_DIM
    before = (lax.broadcasted_iota(jnp.int32, (tk, 2 * tq), 0)
              < (lax.broadcasted_iota(jnp.int32, (tk, 2 * tq), 1) & (tq - 1)))
    tri = tri_ref[...]

    qst = []
    for c in range(n_chain):
        q2 = q_ref[:, c * LANES:(c + 1) * LANES].astype(F32)
        qst.append(jnp.concatenate([jnp.where(low, q2, 0.0), jnp.where(low, 0.0, q2)], axis=0).astype(BF16))

    chains = range(n_chain)

    def tiles(j, diag, carry):
        off = pl.multiple_of(j * tk, tk)
        z, sp, sums = [None] * n_chain, [None] * n_chain, [None] * n_chain

        def scores(c):
            z[c] = _dot_nt(k_ref[pl.ds(off, tk), c * LANES:(c + 1) * LANES], qst[c])
            sp[c] = jnp.maximum(z[c], 0.0) + jnp.log(1.0 + jnp.exp2(_neg_abs(z[c]))) * LOG2E
            if diag:
                sp[c] = jnp.where(before, sp[c], 0.0)

        def suffix(c):
            sums[c] = _dot(tri, sp[c].astype(BF16))

        def weights(c):
            if diag:
                a = jnp.where(before, jnp.exp2(z[c] - sp[c] + sums[c][:tk]), 0.0)
            else:
                a = jnp.exp2(z[c] - sp[c] + sums[c][:tk] + carry[c])
            pv = _dot(vt_ref[c, j], a.astype(BF16))
            if diag:
                acc_ref[c] = pv
            else:
                acc_ref[c] += pv

        for step in range(n_chain + 2 * SB_LAG):
            for d, stage in enumerate((scores, suffix, weights)):
                if 0 <= step - d * SB_LAG < n_chain:
                    stage(step - d * SB_LAG)
        if diag:
            return tuple(sums[c][tk:tk + 1] for c in chains)
        return tuple(carry[c] + sums[c][tk:tk + 1] for c in chains)

    res = lax.fori_loop(0, i, lambda jj, carry: tiles(i - 1 - jj, False, carry), tiles(i, True, None))
    for c in range(n_chain):
        acc = acc_ref[c]
        out_t = jnp.concatenate([acc[:HEAD_DIM, :tq], acc[HEAD_DIM:, tq:]], axis=0)
        o_ref[:, c * LANES:(c + 1) * LANES] = out_t.T.astype(o_ref.dtype)


def _sb_attn(qkv, tri, tq):
    b, s, n3 = qkv.shape
    w = LANES * SB_CHAINS
    nblk = n3 // 3 // w
    return pl.pallas_call(
        functools.partial(_sb_attn_kernel, tq=tq),
        grid=(b, nblk, s // tq),
        in_specs=[
            pl.BlockSpec((None, tq, w), lambda bi, p, i: (bi, i, p)),
            pl.BlockSpec((None, s, w), lambda bi, p, i: (bi, 0, nblk + p)),
            pl.BlockSpec((None, s, w), lambda bi, p, i: (bi, 0, 2 * nblk + p)),
            _const_spec(tri.shape),
        ],
        out_specs=pl.BlockSpec((None, tq, w), lambda bi, p, i: (bi, i, p)),
        out_shape=jax.ShapeDtypeStruct((b, s, nblk * w), BF16),
        scratch_shapes=[pltpu.VMEM((SB_CHAINS, s // tq, LANES, tq), BF16),
                        pltpu.VMEM((SB_CHAINS, LANES, 2 * tq), F32)],
        compiler_params=_params("parallel", "parallel", "arbitrary"),
        name="sb_attn",
    )(qkv, qkv, qkv, tri)


def _mix_ffn_kernel(h_ref, o_ref, wo_ref, g_ref, w1_ref, w3_ref, w2_ref, fg_ref, out_ref, a_ref, acc_ref,
                    *, final, tf):
    h1 = h_ref[...] + _dot(o_ref[...], wo_ref[...])
    a_ref[...] = _rmsnorm(h1, g_ref[...]).astype(BF16)
    acc_ref[...] = h1

    for c in range(w1_ref.shape[1] // tf):
        cols = slice(c * tf, (c + 1) * tf)
        a = a_ref[...]
        u = _dot(a, w1_ref[:, cols])
        v = _dot(a, w3_ref[:, cols])
        t = (u * jax.nn.sigmoid(u)) * v
        acc_ref[...] += _dot(t.astype(BF16), w2_ref[cols, :])
    h2 = acc_ref[...]
    out_ref[...] = _rmsnorm(h2, fg_ref[...]) if final else h2


def _mix_ffn(h, o, wo, g, w1, w3, w2, fg, tm, tf, final):
    r, d = h.shape
    single = pl.Buffered(1)

    def wspec(shape):
        nd = len(shape)
        return pl.BlockSpec(shape, lambda *_: (0,) * nd, pipeline_mode=single)

    return pl.pallas_call(
        functools.partial(_mix_ffn_kernel, final=final, tf=tf),
        grid=(r // tm,),
        in_specs=[
            pl.BlockSpec((tm, d), lambda i: (i, 0)),
            pl.BlockSpec((tm, d), lambda i: (i, 0)),
            wspec(wo.shape),
            _const_spec((1, d)),
            wspec(w1.shape),
            wspec(w3.shape),
            wspec(w2.shape),
            _const_spec((1, d)),
        ],
        out_specs=pl.BlockSpec((tm, d), lambda i: (i, 0)),
        out_shape=jax.ShapeDtypeStruct((r, d), F32),
        scratch_shapes=[pltpu.VMEM((tm, d), BF16), pltpu.VMEM((tm, d), F32)],
        compiler_params=_params("parallel"),
        name="mix_ffn",
    )(h, o, wo, g, w1, w3, w2, fg)


def _nsa_layout(d):
    kvw = N_GROUPS * HEAD_DIM
    off = {"q": 0, "kc": d, "vc": d + kvw, "ksl": d + 2 * kvw, "vsl": d + 3 * kvw, "kw": d + 4 * kvw,
           "vw": d + 5 * kvw, "gt": d + 6 * kvw}
    main = [np.arange(d)]
    for pair in (("ksl", "kw"), ("vsl", "vw")):
        for g in range(N_GROUPS):
            for name in pair:
                main.append(off[name] + g * HEAD_DIM + np.arange(HEAD_DIM))
    main = np.concatenate(main)
    cv = off["kc"] + np.arange(2 * kvw)
    scale = np.where(main < d, HEAD_DIM ** -0.5 * LOG2E, 1.0).astype(np.float32)
    return main, scale, cv, off["gt"]


def _bf16_pieces(x):
    x = np.asarray(x, np.float32)
    out = []
    for _ in range(3):
        p = x.astype(BF16).astype(np.float32)
        out.append(p)
        x = (x - p).astype(np.float32)
    return out


def _nsa_tables(s, tk):
    n_slc = s // SLC_BLOCK
    n_cmp = s // CMP_STRIDE - CMP_BLOCK // CMP_STRIDE + 1
    nc = s // CMP_STRIDE
    pos = np.arange(s)
    kpos = np.zeros((s, LANES), np.float32)
    assert n_slc <= POS_LANE - HEAD_DIM
    kpos[pos, HEAD_DIM + pos // SLC_BLOCK] = 1.0
    for c in range(3):
        kpos[:, POS_LANE + c] = (pos // tk) * tk
        kpos[:, POS_LANE + 3 + c] = pos % tk
    cs = np.arange(n_cmp) * CMP_STRIDE
    ss = np.arange(n_slc) * SLC_BLOCK
    ov = np.clip(np.minimum(cs[:, None] + CMP_BLOCK, ss[None] + SLC_BLOCK) - np.maximum(cs[:, None], ss[None]), 0, None)
    jp = -(-n_slc // 16) * 16
    cmapt = np.zeros((jp, nc), np.float32)
    cmapt[:n_slc, :n_cmp] = (ov / CMP_STRIDE).T
    slopes = np.asarray(2.0 ** (-8.0 * (np.arange(N_HEADS) + 1) / N_HEADS), np.float32)
    slopes = (slopes.astype(np.float64) * LOG2E).astype(np.float32)
    pieces = _bf16_pieces(slopes)
    hconst = np.zeros((N_GROUPS, 2 * HPG, LANES), np.float32)
    hcol = np.zeros((N_GROUPS, HPG, HEAD_DIM, LANES), np.float32)
    for g in range(N_GROUPS):
        for hh in range(HPG):
            for c in range(3):
                hcol[g, hh, POS_LANE - HEAD_DIM + c, :] = pieces[c][g * HPG + hh]
                hcol[g, hh, POS_LANE - HEAD_DIM + 3 + c, :] = pieces[c][g * HPG + hh]
            hconst[g, hh, :] = slopes[g * HPG + hh]
    n_kt = s // tk
    nwt = min(WINDOW // tk + 1, n_kt)
    tt = np.arange(tk)[None, :]

    def dist(r, nt):
        return r * tk + tt - np.arange(nt * tk)[:, None]

    wbias = np.stack([np.where((dist(r, nwt) >= 0) & (dist(r, nwt) < WINDOW), 0.0, NEG) for r in range(nwt)])
    sbias = np.stack([np.where(dist(r, 2) >= 0, 0.0, NEG) for r in range(2)])
    return (jnp.asarray(kpos, BF16), jnp.asarray(cmapt, BF16), jnp.asarray(hconst, F32), jnp.asarray(hcol, F32),
            jnp.asarray(wbias, F32), jnp.asarray(sbias, F32))


def _ffn_chunk(d_ff):
    for tf in (512, 256, 128):
        if d_ff % tf == 0:
            return tf
    return d_ff


def kernel(x, mix_norm_g, ffn_norm_g, final_norm_g, nsa_w_in, nsa_gate_b, nsa_cmp_pe, nsa_cmp_w1, nsa_cmp_w2,
           nsa_w_out, sb_w_in, sb_w_out, ffn_w1, ffn_w3, ffn_w2):
    b, s, d = x.shape
    depth = mix_norm_g.shape[0]
    d_ff = ffn_w1.shape[2]
    tm = min(512, s)
    tm_proj = min(1024, s)
    tq = min(256, s)
    tf = _ffn_chunk(d_ff)

    main_idx, main_scale, cv_idx, gt_off = _nsa_layout(d)
    sb_scale = np.where(np.arange(3 * d) < d, HEAD_DIM ** -0.5 * LOG2E, 1.0).astype(np.float32)
    n_gate = N_BRANCH * N_HEADS
    kpos, cmapt, hconst, hcol, wbias, sbias = _nsa_tables(s, tq)
    tri = jnp.asarray(np.concatenate([-np.triu(np.ones((tq, tq), np.float32), 1),
                                      -np.ones((16, tq), np.float32)], axis=0), BF16)
    chunk = CMP_STRIDE * HEAD_DIM
    nc = s // CMP_STRIDE

    h = x
    for i in range(depth):
        j = i // 2
        gm = mix_norm_g[i].reshape(1, d)
        if i % 2 == 0:
            w_in = nsa_w_in[j]
            w_main = (w_in[:, main_idx] * main_scale).astype(BF16)
            w_cv = w_in[:, cv_idx].astype(BF16)
            w_gate = jnp.pad(w_in[:, gt_off:gt_off + n_gate], ((0, 0), (0, LANES - n_gate))).astype(BF16)
            gate_b = jnp.pad(nsa_gate_b[j], (0, LANES - n_gate)).reshape(1, LANES)
            main, cv, gates = _nsa_proj(h, gm, w_main, w_cv, w_gate, gate_b, tm_proj)
            c = cv.reshape(b, 2, N_GROUPS // 2 * nc, 2 * chunk)
            pe = jnp.broadcast_to(nsa_cmp_pe[j].reshape(2, 2, CMP_STRIDE, 1, HEAD_DIM),
                                  (2, 2, CMP_STRIDE, 2, HEAD_DIM)).reshape(2, 2, 2 * chunk)
            w1 = nsa_cmp_w1[j].reshape(2, 2, CMP_STRIDE, 1, HEAD_DIM, -1)
            w1 = jnp.stack([jnp.concatenate([w1, jnp.zeros_like(w1)], axis=3),
                            jnp.concatenate([jnp.zeros_like(w1), w1], axis=3)], axis=2)
            w1 = w1.reshape(2, 2, 2, 2 * chunk, -1)
            w2pad = jnp.zeros((2, nsa_cmp_w2.shape[2], LANES), F32)
            w2pad = w2pad.at[0, :, :HEAD_DIM].set(nsa_cmp_w2[j, 0]).at[1, :, HEAD_DIM:].set(nsa_cmp_w2[j, 1])
            cmp = _nsa_compress(c, pe, w1.astype(BF16), w2pad.astype(BF16))
            o = _nsa_attn(main, cmp, gates, kpos, cmapt, hconst, hcol, wbias, sbias, tq)
            w_out = nsa_w_out[j]
        else:
            qkv = _sb_proj(h, gm, (sb_w_in[j] * sb_scale).astype(BF16), tm_proj)
            o = _sb_attn(qkv, tri, tq)
            w_out = sb_w_out[j]
        final = i == depth - 1
        h = _mix_ffn(h.reshape(b * s, d), o.reshape(b * s, d), w_out.astype(BF16), ffn_norm_g[i].reshape(1, d),
                     ffn_w1[i].astype(BF16), ffn_w3[i].astype(BF16), ffn_w2[i].astype(BF16),
                     final_norm_g.reshape(1, d), tm, tf, final).reshape(b, s, d)
    return h
```

```python
import functools

import numpy as np
import jax
import jax.numpy as jnp
from jax import lax
from jax.experimental import pallas as pl
from jax.experimental.pallas import tpu as pltpu

F32 = jnp.float32
BF16 = jnp.bfloat16

RMS_EPS = 1e-6
NEG = -1e30
BIG = 1e30

N_HEADS = 16
HEAD_DIM = 64
N_GROUPS = 4
HPG = N_HEADS // N_GROUPS
N_BRANCH = 3
CMP_BLOCK = 32
CMP_STRIDE = 16
SLC_BLOCK = 64
N_SLC_BLOCKS = 8
WINDOW = 512

LANES = 128
POS_LANE = 96
LOG2E = 1.4426950408889634
VMEM_LIMIT = 56 * 1024 * 1024
SB_CHAINS = 8
NSA_CHAINS = 4
SB_LAG = SB_CHAINS
SB_LAG_DIAG = 1
NSA_LAG = 2

_NT = (((1,), (1,)), ((), ()))


def _dot(a, b):
    return jnp.dot(a, b, preferred_element_type=F32)


def _dot_nt(a, b):
    return lax.dot_general(a, b, _NT, preferred_element_type=F32)


def _rmsnorm(x, g):
    ms = jnp.mean(x * x, axis=-1, keepdims=True)
    return (x * lax.rsqrt(ms + RMS_EPS)) * g


def _split_bf16(x):
    hi = x.astype(BF16)
    lo = (x - hi.astype(F32)).astype(BF16)
    return hi, lo


def _neg_abs(x):
    return lax.bitcast_convert_type(lax.bitcast_convert_type(x, jnp.uint32) | jnp.uint32(0x80000000), F32)


def _params(*sem):
    return pltpu.CompilerParams(dimension_semantics=sem, vmem_limit_bytes=VMEM_LIMIT)


def _const_spec(shape):
    nd = len(shape)
    return pl.BlockSpec(shape, lambda *_: (0,) * nd)


def _nsa_proj_kernel(x_ref, g_ref, wm_ref, wc_ref, wg_ref, gb_ref, main_ref, cv_ref, gate_ref, cv_tmp_ref):
    a = _rmsnorm(x_ref[...], g_ref[...]).astype(BF16)
    main_ref[...] = _dot(a, wm_ref[...]).astype(main_ref.dtype)
    cv = _dot(a, wc_ref[...])
    n_chunks = cv.shape[0] // CMP_STRIDE
    for kv in range(2):
        for gp in range(N_GROUPS // 2):
            c0 = (kv * N_GROUPS + 2 * gp) * HEAD_DIM
            cv_tmp_ref[kv, gp] = cv[:, c0:c0 + LANES]
            for l in range(CMP_STRIDE):
                cv_ref[kv, gp, :, l * LANES:(l + 1) * LANES] = cv_tmp_ref[kv, gp,
                                                                          pl.ds(l, n_chunks, stride=CMP_STRIDE), :]
    gate_ref[...] = jax.nn.sigmoid(_dot(a, wg_ref[...]) + gb_ref[...])


def _nsa_proj(h, g, w_main, w_cv, w_gate, gate_b, tm):
    b, s, d = h.shape
    nm, ng = w_main.shape[1], w_gate.shape[1]
    return pl.pallas_call(
        _nsa_proj_kernel,
        grid=(b, s // tm),
        in_specs=[
            pl.BlockSpec((None, tm, d), lambda bi, r: (bi, r, 0)),
            _const_spec((1, d)),
            _const_spec(w_main.shape),
            _const_spec(w_cv.shape),
            _const_spec(w_gate.shape),
            _const_spec((1, ng)),
        ],
        out_specs=[
            pl.BlockSpec((None, tm, nm), lambda bi, r: (bi, r, 0)),
            pl.BlockSpec((None, 2, N_GROUPS // 2, tm // CMP_STRIDE, CMP_STRIDE * LANES),
                         lambda bi, r: (bi, 0, 0, r, 0)),
            pl.BlockSpec((None, tm, ng), lambda bi, r: (bi, r, 0)),
        ],
        out_shape=[
            jax.ShapeDtypeStruct((b, s, nm), BF16),
            jax.ShapeDtypeStruct((b, 2, N_GROUPS // 2, s // CMP_STRIDE, CMP_STRIDE * LANES), F32),
            jax.ShapeDtypeStruct((b, s, ng), F32),
        ],
        scratch_shapes=[pltpu.VMEM((2, N_GROUPS // 2, tm, LANES), F32)],
        compiler_params=_params("parallel", "parallel"),
        name="nsa_proj",
    )(h, g, w_main, w_cv, w_gate, gate_b)


def _sb_proj_kernel(x_ref, g_ref, w_ref, o_ref):
    a = _rmsnorm(x_ref[...], g_ref[...]).astype(BF16)
    o_ref[...] = _dot(a, w_ref[...]).astype(o_ref.dtype)


def _sb_proj(h, g, w, tm):
    b, s, d = h.shape
    n = w.shape[1]
    return pl.pallas_call(
        _sb_proj_kernel,
        grid=(b, s // tm),
        in_specs=[
            pl.BlockSpec((None, tm, d), lambda bi, r: (bi, r, 0)),
            _const_spec((1, d)),
            _const_spec(w.shape),
        ],
        out_specs=pl.BlockSpec((None, tm, n), lambda bi, r: (bi, r, 0)),
        out_shape=jax.ShapeDtypeStruct((b, s, n), BF16),
        compiler_params=_params("parallel", "parallel"),
        name="sb_proj",
    )(h, g, w)


def _nsa_compress_kernel(c_ref, pe_ref, w1_ref, w2_ref, o_ref):
    rows = c_ref.shape[1]
    nc = rows // (N_GROUPS // 2)
    out = [jnp.zeros((rows, LANES), F32) for _ in range(2)]
    for kv in range(2):
        c = c_ref[kv]
        top = (c + pe_ref[kv, 0:1, :]).astype(BF16)
        bot = (c + pe_ref[kv, 1:2, :]).astype(BF16)
        for e in range(2):
            a_top = _dot(top, w1_ref[kv, 0, e])
            a_bot = _dot(bot, w1_ref[kv, 1, e])
            hid = jax.nn.gelu(a_top + pltpu.roll(a_bot, rows - 1, 0), approximate=True)
            out[e] = out[e] + _dot(hid.astype(BF16), w2_ref[kv])
    for gp in range(N_GROUPS // 2):
        for e in range(2):
            g = 2 * gp + e
            o_ref[g * nc:(g + 1) * nc, :] = out[e][gp * nc:(gp + 1) * nc]


def _nsa_compress(c, pe, w1, w2pad):
    b, _, rows, width = c.shape
    return pl.pallas_call(
        _nsa_compress_kernel,
        grid=(b,),
        in_specs=[
            pl.BlockSpec((None, 2, rows, width), lambda bi: (bi, 0, 0, 0)),
            _const_spec(pe.shape),
            _const_spec(w1.shape),
            _const_spec(w2pad.shape),
        ],
        out_specs=pl.BlockSpec((None, 2 * rows, LANES), lambda bi: (bi, 0, 0)),
        out_shape=jax.ShapeDtypeStruct((b, 2 * rows, LANES), F32),
        compiler_params=_params("parallel"),
        name="nsa_compress",
    )(c, pe, w1, w2pad)


def _nsa_attn_kernel(*refs, tq, n_slc, n_sel, n_chain):
    (q_ref, kk_ref, vv_ref, cmp_ref, gate_ref, kpos_ref, cmapt_ref, hconst_ref, hcol_ref, wb_ref, sb_ref,
     o_ref, kaug_s_ref, kaug_w_ref, vts_ref, vtw_ref, qsel_ref, qwin_ref, acc_ref) = refs
    chains = range(n_chain)
    i = pl.program_id(2)
    t0 = i * tq
    tk = tq
    n_kt = kaug_s_ref.shape[1] // tk
    gw = HPG * HEAD_DIM

    @pl.when(i == 0)
    def _():
        top = lax.broadcasted_iota(jnp.int32, (LANES, tk), 0) < HEAD_DIM
        for c in chains:
            kk = kk_ref[:, c * LANES:(c + 1) * LANES].astype(F32)
            first = lax.broadcasted_iota(jnp.int32, kk.shape, 1) < HEAD_DIM
            aux = kpos_ref[...].astype(F32)
            kaug_s_ref[c] = jnp.where(first, kk, aux).astype(BF16)
            kaug_w_ref[c] = jnp.where(first, pltpu.roll(kk, HEAD_DIM, 1), aux).astype(BF16)
            for j in range(n_kt):
                vt = vv_ref[j * tk:(j + 1) * tk, c * LANES:(c + 1) * LANES].astype(F32).T
                vts_ref[c, j] = jnp.where(top, vt, 1.0).astype(BF16)
                vtw_ref[c, j] = jnp.where(top, 1.0, vt).astype(BF16)

    q_t = [q_ref[:, c * gw:(c + 1) * gw].astype(F32).T for c in chains]

    def q_operand(c, h, aux_t):
        return jnp.concatenate([q_t[c][h * HEAD_DIM:(h + 1) * HEAD_DIM], aux_t], axis=0).astype(BF16)

    def aux_rows(c, h):
        return jnp.concatenate([hcol_ref[c, h]] * (tq // LANES), axis=1)

    def scores(kref, qref, c, j0, nt):
        kt = kref[c, pl.ds(pl.multiple_of(j0 * tk, tk), nt * tk), :]
        return _dot(kt, qref[c])

    def pv(vt_ref, c, j0, nt, p):
        acc = _dot(vt_ref[c, j0], p[0:tk])
        for u in range(1, nt):
            acc = acc + _dot(vt_ref[c, j0 + u], p[u * tk:(u + 1) * tk])
        return acc

    def heads(bias):
        return jnp.concatenate([bias] * HPG, axis=1)

    for c in chains:
        for h in range(HPG):
            qwin_ref[c, :, h * tq:(h + 1) * tq] = q_operand(c, h, aux_rows(c, h))
    nwt = min(WINDOW // tk + 1, n_kt)
    jw = jnp.clip(i - (nwt - 1), 0, n_kt - nwt)
    wbias = wb_ref[i - jw]

    nc = cmp_ref.shape[0] // n_chain
    ck = [cmp_ref[c * nc:(c + 1) * nc, :] for c in chains]
    first_c = lax.broadcasted_iota(jnp.int32, (nc, LANES), 1) < HEAD_DIM
    sc_all = [_dot(jnp.where(first_c, ck[c], 0.0).astype(BF16), qwin_ref[c]) for c in chains]
    s_w = [scores(kaug_w_ref, qwin_ref, c, jw, nwt) for c in chains]

    def window_head(c, h):
        s_h = s_w[c][:, h * tq:(h + 1) * tq] + wbias
        p_h = jnp.exp2(s_h - jnp.max(s_h, axis=0, keepdims=True)).astype(BF16)
        acc = pv(vtw_ref, c, jw, nwt, p_h)
        return acc[HEAD_DIM:] * (1.0 / acc[0:1])

    dist_c = ((t0 + lax.broadcasted_iota(jnp.int32, (nc, tq), 1))
              - (lax.broadcasted_iota(jnp.int32, (nc, tq), 0) * CMP_STRIDE + (CMP_BLOCK - 1)))
    valid_c = dist_c >= 0
    dist_cf = dist_c.astype(F32)
    psum, p_heads = [], []
    for c in chains:
        ps = jnp.zeros((nc, tq), F32)
        ph = []
        for h in range(HPG):
            slope = hconst_ref[c, h:h + 1, 0:1]
            s_c = jnp.where(valid_c, sc_all[c][:, h * tq:(h + 1) * tq] - slope * dist_cf, NEG)
            m_c = jnp.max(s_c, axis=0, keepdims=True)
            e_c = jnp.where(valid_c, jnp.exp2(s_c - m_c), 0.0)
            den = jnp.sum(e_c, axis=0, keepdims=True)
            p_c = e_c / jnp.maximum(den, 1e-30)
            ps = ps + p_c
            ph.append(p_c.astype(BF16))
        psum.append(ps)
        p_heads.append(ph)
    o_cmp_t = [_dot(ck[c].T.astype(BF16), jnp.concatenate(p_heads[c], axis=1))[HEAD_DIM:] for c in chains]

    cmt = cmapt_ref[...]
    jp = cmt.shape[0]
    imp_t = []
    for c in chains:
        p_hi, p_lo = _split_bf16(psum[c])
        imp_t.append(_dot(cmt, p_hi) + _dot(cmt, p_lo))
    jb = lax.broadcasted_iota(jnp.int32, (jp, tq), 0)
    cur = jnp.right_shift(t0 + lax.broadcasted_iota(jnp.int32, (jp, tq), 1), SLC_BLOCK.bit_length() - 1)
    causal_j = jb <= cur
    forced = (jb == 0) | (jb == cur) | (jb == cur - 1)
    val = [jnp.where(causal_j & forced, BIG, jnp.where(causal_j, imp_t[c], -BIG)) for c in chains]
    rank = [jnp.zeros((jp, tq), jnp.int32) for c in chains]
    o_win_heads = [[] for c in chains]
    for j in range(n_slc):
        if j % (n_slc // HPG) == 0:
            for c in chains:
                o_win_heads[c].append(window_head(c, j // (n_slc // HPG)))
        for c in chains:
            row = val[c][j:j + 1, :]
            beats = (row > val[c]) | ((row == val[c]) & (jb > j))
            rank[c] = rank[c] + beats.astype(jnp.int32)
    o_win_t = [jnp.concatenate(o_win_heads[c], axis=1) for c in chains]

    for c in chains:
        mb_t = jnp.where(causal_j & (rank[c] < n_sel), 0.0, NEG)
        mb_t = jnp.concatenate([mb_t, jnp.zeros((HEAD_DIM - jp, tq), F32)], axis=0)
        for h in range(HPG):
            qsel_ref[c, :, h * tq:(h + 1) * tq] = q_operand(c, h, mb_t + aux_rows(c, h))
    pair = i // 2
    sbias = heads(sb_ref[i - 2 * pair])

    def sel_chunk(j0, carry):
        s, out = [None] * n_chain, [None] * n_chain

        def stage_scores(c):
            s[c] = scores(kaug_s_ref, qsel_ref, c, j0, 2)
            if carry is None:
                s[c] = s[c] + sbias

        def stage_pv(c):
            m_tile = jnp.max(s[c], axis=0, keepdims=True)
            if carry is None:
                out[c] = m_tile
                acc_ref[c] = pv(vts_ref, c, j0, 2, jnp.exp2(s[c] - m_tile).astype(BF16))
            else:
                out[c] = jnp.maximum(carry[c], m_tile)
                p = jnp.exp2(s[c] - out[c]).astype(BF16)
                acc_ref[c] = jnp.exp2(carry[c] - out[c]) * acc_ref[c] + pv(vts_ref, c, j0, 2, p)

        for step in range(n_chain + NSA_LAG):
            if step < n_chain:
                stage_scores(step)
            if 0 <= step - NSA_LAG < n_chain:
                stage_pv(step - NSA_LAG)
        return tuple(out)

    lax.fori_loop(0, pair, lambda jj, carry: sel_chunk(2 * (pair - 1 - jj), carry), sel_chunk(2 * pair, None))

    gt_t = gate_ref[...].T
    for c in chains:
        acc_s = acc_ref[c]
        o_sel_t = acc_s[:HEAD_DIM] * (1.0 / acc_s[HEAD_DIM:HEAD_DIM + 1])
        rows = []
        for h in range(HPG):
            sl = slice(h * tq, (h + 1) * tq)
            g_cmp, g_sel, g_win = (gt_t[br * N_HEADS + c * HPG + h:br * N_HEADS + c * HPG + h + 1]
                                   for br in range(N_BRANCH))
            rows.append(g_cmp * o_cmp_t[c][:, sl] + g_sel * o_sel_t[:, sl] + g_win * o_win_t[c][:, sl])
        o_ref[:, c * gw:(c + 1) * gw] = jnp.concatenate(rows, axis=0).T.astype(o_ref.dtype)


def _nsa_attn(main, cmp, gates, kpos, cmapt, hconst, hcol, wbias, sbias, tq):
    b, s, _ = main.shape
    assert (s // tq) % 2 == 0 and NSA_CHAINS == N_GROUPS
    nch = NSA_CHAINS
    nc = cmp.shape[1] // N_GROUPS
    n_slc = s // SLC_BLOCK
    gw = HPG * HEAD_DIM
    qw, kvw = N_GROUPS * gw, N_GROUPS * LANES
    assert qw % kvw == 0

    kern =functools.partial(_nsa_attn_kernel, tq=tq, n_slc=n_slc, n_sel=min(N_SLC_BLOCKS, n_slc), n_chain=nch)
    return pl.pallas_call(
        kern,
        grid=(b, N_GROUPS // nch, s // tq),
        in_specs=(
            [
                pl.BlockSpec((None, tq, qw), lambda bi, gp, i: (bi, i, 0)),
                pl.BlockSpec((None, s, kvw), lambda bi, gp, i: (bi, 0, qw // kvw)),
                pl.BlockSpec((None, s, kvw), lambda bi, gp, i: (bi, 0, qw // kvw + 1)),
            ]
            + [
                pl.BlockSpec((None, nch * nc, LANES), lambda bi, gp, i: (bi, gp, 0)),
                pl.BlockSpec((None, tq, LANES), lambda bi, gp, i: (bi, i, 0)),
                _const_spec(kpos.shape),
                _const_spec(cmapt.shape),
                pl.BlockSpec((nch, 2 * HPG, LANES), lambda bi, gp, i: (gp, 0, 0)),
                pl.BlockSpec((nch, HPG, HEAD_DIM, LANES), lambda bi, gp, i: (gp, 0, 0, 0)),
                _const_spec(wbias.shape),
                _const_spec(sbias.shape),
            ]),
        out_specs=pl.BlockSpec((None, tq, nch * gw), lambda bi, gp, i: (bi, i, gp)),
        out_shape=jax.ShapeDtypeStruct((b, s, N_GROUPS * gw), BF16),
        scratch_shapes=[
            pltpu.VMEM((nch, s, LANES), BF16),
            pltpu.VMEM((nch, s, LANES), BF16),
            pltpu.VMEM((nch, s // tq, LANES, tq), BF16),
            pltpu.VMEM((nch, s // tq, LANES, tq), BF16),
            pltpu.VMEM((nch, LANES, HPG * tq), BF16),
            pltpu.VMEM((nch, LANES, HPG * tq), BF16),
            pltpu.VMEM((nch, LANES, HPG * tq), F32),
        ],
        compiler_params=_params("parallel", "parallel", "arbitrary"),
        name="nsa_attn",
    )(main, main, main, cmp, gates, kpos, cmapt, hconst, hcol, wbias, sbias)


def _sb_attn_kernel(q_ref, k_ref, v_ref, tri_ref, o_ref, vt_ref, acc_ref, *, tq):
    i = pl.program_id(2)
    tk = tq
    n_kt = k_ref.shape[0] // tk
    n_chain = q_ref.shape[1] // LANES

    @pl.when(i == 0)
    def _():
        for c in range(n_chain):
            for j in range(n_kt):
                vt_ref[c, j] = v_ref[j * tk:(j + 1) * tk, c * LANES:(c + 1) * LANES].astype(F32).T.astype(BF16)

    low = lax.broadcasted_iota(jnp.int32, (tq, LANES), 1) < HEAD_DIM
    before = (lax.broadcasted_iota(jnp.int32, (tk, 2 * tq), 0)
              < (lax.broadcasted_iota(jnp.int32, (tk, 2 * tq), 1) & (tq - 1)))
    tri = tri_ref[...]

    qst = []
    for c in range(n_chain):
        q2 = q_ref[:, c * LANES:(c + 1) * LANES].astype(F32)
        qst.append(jnp.concatenate([jnp.where(low, q2, 0.0), jnp.where(low, 0.0, q2)], axis=0).astype(BF16))

    chains = range(n_chain)

    def tiles(j, diag, carry):
        off = pl.multiple_of(j * tk, tk)
        z, sp, sums = [None] * n_chain, [None] * n_chain, [None] * n_chain

        def scores(c):
            z[c] = _dot_nt(k_ref[pl.ds(off, tk), c * LANES:(c + 1) * LANES], qst[c])
            sp[c] = jnp.maximum(z[c], 0.0) + jnp.log(1.0 + jnp.exp2(_neg_abs(z[c]))) * LOG2E
            if diag:
                sp[c] = jnp.where(before, sp[c], 0.0)

        def suffix(c):
            sums[c] = _dot(tri, sp[c].astype(BF16))

        def weights(c):
            if diag:
                a = jnp.where(before, jnp.exp2(z[c] - sp[c] + sums[c][:tk]), 0.0)
            else:
                a = jnp.exp2(z[c] - sp[c] + sums[c][:tk] + carry[c])
            pv = _dot(vt_ref[c, j], a.astype(BF16))
            if diag:
                acc_ref[c] = pv
            else:
                acc_ref[c] += pv

        lag = SB_LAG_DIAG if diag else SB_LAG
        for step in range(n_chain + 2 * lag):
            for d, stage in enumerate((scores, suffix, weights)):
                if 0 <= step - d * lag < n_chain:
                    stage(step - d * lag)
        if diag:
            return tuple(sums[c][tk:tk + 1] for c in chains)
        return tuple(carry[c] + sums[c][tk:tk + 1] for c in chains)

    res = lax.fori_loop(0, i, lambda jj, carry: tiles(i - 1 - jj, False, carry), tiles(i, True, None))
    for c in range(n_chain):
        acc = acc_ref[c]
        out_t = jnp.concatenate([acc[:HEAD_DIM, :tq], acc[HEAD_DIM:, tq:]], axis=0)
        o_ref[:, c * LANES:(c + 1) * LANES] = out_t.T.astype(o_ref.dtype)


def _sb_attn(qkv, tri, tq):
    b, s, n3 = qkv.shape
    w = LANES * SB_CHAINS
    nblk = n3 // 3 // w
    return pl.pallas_call(
        functools.partial(_sb_attn_kernel, tq=tq),
        grid=(b, nblk, s // tq),
        in_specs=[
            pl.BlockSpec((None, tq, w), lambda bi, p, i: (bi, i, p)),
            pl.BlockSpec((None, s, w), lambda bi, p, i: (bi, 0, nblk + p)),
            pl.BlockSpec((None, s, w), lambda bi, p, i: (bi, 0, 2 * nblk + p)),
            _const_spec(tri.shape),
        ],
        out_specs=pl.BlockSpec((None, tq, w), lambda bi, p, i: (bi, i, p)),
        out_shape=jax.ShapeDtypeStruct((b, s, nblk * w), BF16),
        scratch_shapes=[pltpu.VMEM((SB_CHAINS, s // tq, LANES, tq), BF16),
                        pltpu.VMEM((SB_CHAINS, LANES, 2 * tq), F32)],
        compiler_params=_params("parallel", "parallel", "arbitrary"),
        name="sb_attn",
    )(qkv, qkv, qkv, tri)


def _mix_ffn_kernel(h_ref, o_ref, wo_ref, g_ref, w1_ref, w3_ref, w2_ref, fg_ref, out_ref, a_ref, acc_ref,
                    *, final, tf):
    h1 = h_ref[...] + _dot(o_ref[...], wo_ref[...])
    a_ref[...] = _rmsnorm(h1, g_ref[...]).astype(BF16)
    acc_ref[...] = h1

    for c in range(w1_ref.shape[1] // tf):
        cols = slice(c * tf, (c + 1) * tf)
        a = a_ref[...]
        u = _dot(a, w1_ref[:, cols])
        v = _dot(a, w3_ref[:, cols])
        t = (u * jax.nn.sigmoid(u)) * v
        acc_ref[...] += _dot(t.astype(BF16), w2_ref[cols, :])
    h2 = acc_ref[...]
    out_ref[...] = _rmsnorm(h2, fg_ref[...]) if final else h2


def _mix_ffn(h, o, wo, g, w1, w3, w2, fg, tm, tf, final):
    r, d = h.shape
    single = pl.Buffered(1)

    def wspec(shape):
        nd = len(shape)
        return pl.BlockSpec(shape, lambda *_: (0,) * nd, pipeline_mode=single)

    return pl.pallas_call(
        functools.partial(_mix_ffn_kernel, final=final, tf=tf),
        grid=(r // tm,),
        in_specs=[
            pl.BlockSpec((tm, d), lambda i: (i, 0)),
            pl.BlockSpec((tm, d), lambda i: (i, 0)),
            wspec(wo.shape),
            _const_spec((1, d)),
            wspec(w1.shape),
            wspec(w3.shape),
            wspec(w2.shape),
            _const_spec((1, d)),
        ],
        out_specs=pl.BlockSpec((tm, d), lambda i: (i, 0)),
        out_shape=jax.ShapeDtypeStruct((r, d), F32),
        scratch_shapes=[pltpu.VMEM((tm, d), BF16), pltpu.VMEM((tm, d), F32)],
        compiler_params=_params("parallel"),
        name="mix_ffn",
    )(h, o, wo, g, w1, w3, w2, fg)


def _nsa_layout(d):
    kvw = N_GROUPS * HEAD_DIM
    off = {"q": 0, "kc": d, "vc": d + kvw, "ksl": d + 2 * kvw, "vsl": d + 3 * kvw, "kw": d + 4 * kvw,
           "vw": d + 5 * kvw, "gt": d + 6 * kvw}
    main = [np.arange(d)]
    for pair in (("ksl", "kw"), ("vsl", "vw")):
        for g in range(N_GROUPS):
            for name in pair:
                main.append(off[name] + g * HEAD_DIM + np.arange(HEAD_DIM))
    main = np.concatenate(main)
    cv = off["kc"] + np.arange(2 * kvw)
    scale = np.where(main < d, HEAD_DIM ** -0.5 * LOG2E, 1.0).astype(np.float32)
    return main, scale, cv, off["gt"]


def _bf16_pieces(x):
    x = np.asarray(x, np.float32)
    out = []
    for _ in range(3):
        p = x.astype(BF16).astype(np.float32)
        out.append(p)
        x = (x - p).astype(np.float32)
    return out


def _nsa_tables(s, tk):
    n_slc = s // SLC_BLOCK
    n_cmp = s // CMP_STRIDE - CMP_BLOCK // CMP_STRIDE + 1
    nc = s // CMP_STRIDE
    pos = np.arange(s)
    kpos = np.zeros((s, LANES), np.float32)
    assert n_slc <= POS_LANE - HEAD_DIM
    kpos[pos, HEAD_DIM + pos // SLC_BLOCK] = 1.0
    for c in range(3):
        kpos[:, POS_LANE + c] = (pos // tk) * tk
        kpos[:, POS_LANE + 3 + c] = pos % tk
    cs = np.arange(n_cmp) * CMP_STRIDE
    ss = np.arange(n_slc) * SLC_BLOCK
    ov = np.clip(np.minimum(cs[:, None] + CMP_BLOCK, ss[None] + SLC_BLOCK) - np.maximum(cs[:, None], ss[None]), 0, None)
    jp = -(-n_slc // 16) * 16
    cmapt = np.zeros((jp, nc), np.float32)
    cmapt[:n_slc, :n_cmp] = (ov / CMP_STRIDE).T
    slopes = np.asarray(2.0 ** (-8.0 * (np.arange(N_HEADS) + 1) / N_HEADS), np.float32)
    slopes = (slopes.astype(np.float64) * LOG2E).astype(np.float32)
    pieces = _bf16_pieces(slopes)
    hconst = np.zeros((N_GROUPS, 2 * HPG, LANES), np.float32)
    hcol = np.zeros((N_GROUPS, HPG, HEAD_DIM, LANES), np.float32)
    for g in range(N_GROUPS):
        for hh in range(HPG):
            for c in range(3):
                hcol[g, hh, POS_LANE - HEAD_DIM + c, :] = pieces[c][g * HPG + hh]
                hcol[g, hh, POS_LANE - HEAD_DIM + 3 + c, :] = pieces[c][g * HPG + hh]
            hconst[g, hh, :] = slopes[g * HPG + hh]
    n_kt = s // tk
    nwt = min(WINDOW // tk + 1, n_kt)
    tt = np.arange(tk)[None, :]

    def dist(r, nt):
        return r * tk + tt - np.arange(nt * tk)[:, None]

    wbias = np.stack([np.where((dist(r, nwt) >= 0) & (dist(r, nwt) < WINDOW), 0.0, NEG) for r in range(nwt)])
    sbias = np.stack([np.where(dist(r, 2) >= 0, 0.0, NEG) for r in range(2)])
    return (jnp.asarray(kpos, BF16), jnp.asarray(cmapt, BF16), jnp.asarray(hconst, F32), jnp.asarray(hcol, F32),
            jnp.asarray(wbias, F32), jnp.asarray(sbias, F32))


def _ffn_chunk(d_ff):
    for tf in (512, 256, 128):
        if d_ff % tf == 0:
            return tf
    return d_ff


def kernel(x, mix_norm_g, ffn_norm_g, final_norm_g, nsa_w_in, nsa_gate_b, nsa_cmp_pe, nsa_cmp_w1, nsa_cmp_w2,
           nsa_w_out, sb_w_in, sb_w_out, ffn_w1, ffn_w3, ffn_w2):
    b, s, d = x.shape
    depth = mix_norm_g.shape[0]
    d_ff = ffn_w1.shape[2]
    tm = min(512, s)
    tm_proj = min(1024, s)
    tq = min(256, s)
    tf = _ffn_chunk(d_ff)

    main_idx, main_scale, cv_idx, gt_off = _nsa_layout(d)
    sb_scale = np.where(np.arange(3 * d) < d, HEAD_DIM ** -0.5 * LOG2E, 1.0).astype(np.float32)
    n_gate = N_BRANCH * N_HEADS
    kpos, cmapt, hconst, hcol, wbias, sbias = _nsa_tables(s, tq)
    tri = jnp.asarray(np.concatenate([-np.triu(np.ones((tq, tq), np.float32), 1),
                                      -np.ones((16, tq), np.float32)], axis=0), BF16)
    chunk = CMP_STRIDE * HEAD_DIM
    nc = s // CMP_STRIDE

    h = x
    for i in range(depth):
        j = i // 2
        gm = mix_norm_g[i].reshape(1, d)
        if i % 2 == 0:
            w_in = nsa_w_in[j]
            w_main = (w_in[:, main_idx] * main_scale).astype(BF16)
            w_cv = w_in[:, cv_idx].astype(BF16)
            w_gate = jnp.pad(w_in[:, gt_off:gt_off + n_gate], ((0, 0), (0, LANES - n_gate))).astype(BF16)
            gate_b = jnp.pad(nsa_gate_b[j], (0, LANES - n_gate)).reshape(1, LANES)
            main, cv, gates = _nsa_proj(h, gm, w_main, w_cv, w_gate, gate_b, tm_proj)
            c = cv.reshape(b, 2, N_GROUPS // 2 * nc, 2 * chunk)
            pe = jnp.broadcast_to(nsa_cmp_pe[j].reshape(2, 2, CMP_STRIDE, 1, HEAD_DIM),
                                  (2, 2, CMP_STRIDE, 2, HEAD_DIM)).reshape(2, 2, 2 * chunk)
            w1 = nsa_cmp_w1[j].reshape(2, 2, CMP_STRIDE, 1, HEAD_DIM, -1)
            w1 = jnp.stack([jnp.concatenate([w1, jnp.zeros_like(w1)], axis=3),
                            jnp.concatenate([jnp.zeros_like(w1), w1], axis=3)], axis=2)
            w1 = w1.reshape(2, 2, 2, 2 * chunk, -1)
            w2pad = jnp.zeros((2, nsa_cmp_w2.shape[2], LANES), F32)
            w2pad = w2pad.at[0, :, :HEAD_DIM].set(nsa_cmp_w2[j, 0]).at[1, :, HEAD_DIM:].set(nsa_cmp_w2[j, 1])
            cmp = _nsa_compress(c, pe, w1.astype(BF16), w2pad.astype(BF16))
            o = _nsa_attn(main, cmp, gates, kpos, cmapt, hconst, hcol, wbias, sbias, tq)
            w_out = nsa_w_out[j]
        else:
            qkv = _sb_proj(h, gm, (sb_w_in[j] * sb_scale).astype(BF16), tm_proj)
            o = _sb_attn(qkv, tri, tq)
            w_out = sb_w_out[j]
        final = i == depth - 1
        h = _mix_ffn(h.reshape(b * s, d), o.reshape(b * s, d), w_out.astype(BF16), ffn_norm_g[i].reshape(1, d),
                     ffn_w1[i].astype(BF16), ffn_w3[i].astype(BF16), ffn_w2[i].astype(BF16),
                     final_norm_g.reshape(1, d), tm, tf, final).reshape(b, s, d)
    return h
```

```python
import functools

import numpy as np
import jax
import jax.numpy as jnp
from jax import lax
from jax.experimental import pallas as pl
from jax.experimental.pallas import tpu as pltpu

F32 = jnp.float32
BF16 = jnp.bfloat16

RMS_EPS = 1e-6
NEG = -1e30
BIG = 1e30

N_HEADS = 16
HEAD_DIM = 64
N_GROUPS = 4
HPG = N_HEADS // N_GROUPS
N_BRANCH = 3
CMP_BLOCK = 32
CMP_STRIDE = 16
SLC_BLOCK = 64
N_SLC_BLOCKS = 8
WINDOW = 512

LANES = 128
POS_LANE = 96
LOG2E = 1.4426950408889634
VMEM_LIMIT = 56 * 1024 * 1024
SB_CHAINS = 8
NSA_CHAINS = 4
SB_LAG = SB_CHAINS
NSA_LAG = 2

_NT = (((1,), (1,)), ((), ()))


def _dot(a, b):
    return jnp.dot(a, b, preferred_element_type=F32)


def _dot_nt(a, b):
    return lax.dot_general(a, b, _NT, preferred_element_type=F32)


def _rmsnorm(x, g):
    ms = jnp.mean(x * x, axis=-1, keepdims=True)
    return (x * lax.rsqrt(ms + RMS_EPS)) * g


def _split_bf16(x):
    hi = x.astype(BF16)
    lo = (x - hi.astype(F32)).astype(BF16)
    return hi, lo


def _neg_abs(x):
    return lax.bitcast_convert_type(lax.bitcast_convert_type(x, jnp.uint32) | jnp.uint32(0x80000000), F32)


def _params(*sem):
    return pltpu.CompilerParams(dimension_semantics=sem, vmem_limit_bytes=VMEM_LIMIT)


def _const_spec(shape):
    nd = len(shape)
    return pl.BlockSpec(shape, lambda *_: (0,) * nd)


def _nsa_proj_kernel(x_ref, g_ref, wm_ref, wc_ref, wg_ref, gb_ref, main_ref, cv_ref, gate_ref, cv_tmp_ref):
    a = _rmsnorm(x_ref[...], g_ref[...]).astype(BF16)
    main_ref[...] = _dot(a, wm_ref[...]).astype(main_ref.dtype)
    cv = _dot(a, wc_ref[...])
    n_chunks = cv.shape[0] // CMP_STRIDE
    for kv in range(2):
        for gp in range(N_GROUPS // 2):
            c0 = (kv * N_GROUPS + 2 * gp) * HEAD_DIM
            cv_tmp_ref[kv, gp] = cv[:, c0:c0 + LANES]
            for l in range(CMP_STRIDE):
                cv_ref[kv, gp, :, l * LANES:(l + 1) * LANES] = cv_tmp_ref[kv, gp,
                                                                          pl.ds(l, n_chunks, stride=CMP_STRIDE), :]
    gate_ref[...] = jax.nn.sigmoid(_dot(a, wg_ref[...]) + gb_ref[...])


def _nsa_proj(h, g, w_main, w_cv, w_gate, gate_b, tm):
    b, s, d = h.shape
    nm, ng = w_main.shape[1], w_gate.shape[1]
    return pl.pallas_call(
        _nsa_proj_kernel,
        grid=(b, s // tm),
        in_specs=[
            pl.BlockSpec((None, tm, d), lambda bi, r: (bi, r, 0)),
            _const_spec((1, d)),
            _const_spec(w_main.shape),
            _const_spec(w_cv.shape),
            _const_spec(w_gate.shape),
            _const_spec((1, ng)),
        ],
        out_specs=[
            pl.BlockSpec((None, tm, nm), lambda bi, r: (bi, r, 0)),
            pl.BlockSpec((None, 2, N_GROUPS // 2, tm // CMP_STRIDE, CMP_STRIDE * LANES),
                         lambda bi, r: (bi, 0, 0, r, 0)),
            pl.BlockSpec((None, tm, ng), lambda bi, r: (bi, r, 0)),
        ],
        out_shape=[
            jax.ShapeDtypeStruct((b, s, nm), BF16),
            jax.ShapeDtypeStruct((b, 2, N_GROUPS // 2, s // CMP_STRIDE, CMP_STRIDE * LANES), F32),
            jax.ShapeDtypeStruct((b, s, ng), F32),
        ],
        scratch_shapes=[pltpu.VMEM((2, N_GROUPS // 2, tm, LANES), F32)],
        compiler_params=_params("parallel", "parallel"),
        name="nsa_proj",
    )(h, g, w_main, w_cv, w_gate, gate_b)


def _sb_proj_kernel(x_ref, g_ref, w_ref, o_ref):
    a = _rmsnorm(x_ref[...], g_ref[...]).astype(BF16)
    o_ref[...] = _dot(a, w_ref[...]).astype(o_ref.dtype)


def _sb_proj(h, g, w, tm):
    b, s, d = h.shape
    n = w.shape[1]
    return pl.pallas_call(
        _sb_proj_kernel,
        grid=(b, s // tm),
        in_specs=[
            pl.BlockSpec((None, tm, d), lambda bi, r: (bi, r, 0)),
            _const_spec((1, d)),
            _const_spec(w.shape),
        ],
        out_specs=pl.BlockSpec((None, tm, n), lambda bi, r: (bi, r, 0)),
        out_shape=jax.ShapeDtypeStruct((b, s, n), BF16),
        compiler_params=_params("parallel", "parallel"),
        name="sb_proj",
    )(h, g, w)


def _nsa_compress_kernel(c_ref, pe_ref, w1_ref, w2_ref, o_ref):
    rows = c_ref.shape[1]
    nc = rows // (N_GROUPS // 2)
    out = [jnp.zeros((rows, LANES), F32) for _ in range(2)]
    for kv in range(2):
        c = c_ref[kv]
        top = (c + pe_ref[kv, 0:1, :]).astype(BF16)
        bot = (c + pe_ref[kv, 1:2, :]).astype(BF16)
        for e in range(2):
            a_top = _dot(top, w1_ref[kv, 0, e])
            a_bot = _dot(bot, w1_ref[kv, 1, e])
            hid = jax.nn.gelu(a_top + pltpu.roll(a_bot, rows - 1, 0), approximate=True)
            out[e] = out[e] + _dot(hid.astype(BF16), w2_ref[kv])
    for gp in range(N_GROUPS // 2):
        for e in range(2):
            g = 2 * gp + e
            o_ref[g * nc:(g + 1) * nc, :] = out[e][gp * nc:(gp + 1) * nc]


def _nsa_compress(c, pe, w1, w2pad):
    b, _, rows, width = c.shape
    return pl.pallas_call(
        _nsa_compress_kernel,
        grid=(b,),
        in_specs=[
            pl.BlockSpec((None, 2, rows, width), lambda bi: (bi, 0, 0, 0)),
            _const_spec(pe.shape),
            _const_spec(w1.shape),
            _const_spec(w2pad.shape),
        ],
        out_specs=pl.BlockSpec((None, 2 * rows, LANES), lambda bi: (bi, 0, 0)),
        out_shape=jax.ShapeDtypeStruct((b, 2 * rows, LANES), F32),
        compiler_params=_params("parallel"),
        name="nsa_compress",
    )(c, pe, w1, w2pad)


def _nsa_select_kernel(q_ref, cmp_ref, gate_ref, cmapt_ref, hconst_ref, ocmp_ref, mask_ref, *, tq, n_slc, n_sel):
    i = pl.program_id(1)
    t0 = i * tq
    gw = HPG * HEAD_DIM
    chains = range(N_GROUPS)
    nc = cmp_ref.shape[0] // N_GROUPS
    cmt = cmapt_ref[...]
    jp = cmt.shape[0]
    gt_t = gate_ref[...].T
    dist_c = ((t0 + lax.broadcasted_iota(jnp.int32, (nc, tq), 1))
              - (lax.broadcasted_iota(jnp.int32, (nc, tq), 0) * CMP_STRIDE + (CMP_BLOCK - 1)))
    valid_c = dist_c >= 0
    dist_cf = dist_c.astype(F32)
    jb = lax.broadcasted_iota(jnp.int32, (jp, tq), 0)
    cur = jnp.right_shift(t0 + lax.broadcasted_iota(jnp.int32, (jp, tq), 1), SLC_BLOCK.bit_length() - 1)
    causal_j = jb <= cur
    forced = (jb == 0) | (jb == cur) | (jb == cur - 1)

    q_t = [q_ref[:, c * gw:(c + 1) * gw].astype(F32).T.astype(BF16) for c in chains]
    ck = [cmp_ref[c * nc:(c + 1) * nc, :] for c in chains]
    sc = [[_dot(ck[c][:, :HEAD_DIM].astype(BF16), q_t[c][h * HEAD_DIM:(h + 1) * HEAD_DIM]) for h in range(HPG)]
          for c in chains]
    val = []
    for c in chains:
        ps = jnp.zeros((nc, tq), F32)
        ph = []
        for h in range(HPG):
            slope = hconst_ref[c, h:h + 1, 0:1]
            s_c = jnp.where(valid_c, sc[c][h] - slope * dist_cf, NEG)
            m_c = jnp.max(s_c, axis=0, keepdims=True)
            e_c = jnp.where(valid_c, jnp.exp2(s_c - m_c), 0.0)
            den = jnp.sum(e_c, axis=0, keepdims=True)
            p_c = e_c / jnp.maximum(den, 1e-30)
            ps = ps + p_c
            ph.append(p_c.astype(BF16))
        o_cmp_t = _dot(ck[c].T.astype(BF16), jnp.concatenate(ph, axis=1))[HEAD_DIM:]
        rows = [gt_t[c * HPG + h:c * HPG + h + 1] * o_cmp_t[:, h * tq:(h + 1) * tq] for h in range(HPG)]
        ocmp_ref[:, c * gw:(c + 1) * gw] = jnp.concatenate(rows, axis=0).T.astype(ocmp_ref.dtype)
        p_hi, p_lo = _split_bf16(ps)
        imp_t = _dot(cmt, p_hi) + _dot(cmt, p_lo)
        val.append(jnp.where(causal_j & forced, BIG, jnp.where(causal_j, imp_t, -BIG)))

    rank = [jnp.zeros((jp, tq), jnp.int32) for c in chains]
    for j in range(n_slc):
        for c in chains:
            row = val[c][j:j + 1, :]
            beats = (row > val[c]) | ((row == val[c]) & (jb > j))
            rank[c] = rank[c] + beats.astype(jnp.int32)
    for c in chains:
        mask_ref[c] = jnp.where(causal_j & (rank[c] < n_sel), 0.0, NEG)


def _nsa_select(main, cmp, gates, cmapt, hconst, tq):
    b, s, _ = main.shape
    nc = cmp.shape[1] // N_GROUPS
    n_slc = s // SLC_BLOCK
    qw = N_HEADS * HEAD_DIM
    jp = cmapt.shape[0]
    kern = functools.partial(_nsa_select_kernel, tq=tq, n_slc=n_slc, n_sel=min(N_SLC_BLOCKS, n_slc))
    return pl.pallas_call(
        kern,
        grid=(b, s // tq),
        in_specs=[
            pl.BlockSpec((None, tq, qw), lambda bi, i: (bi, i, 0)),
            pl.BlockSpec((None, N_GROUPS * nc, LANES), lambda bi, i: (bi, 0, 0)),
            pl.BlockSpec((None, tq, LANES), lambda bi, i: (bi, i, 0)),
            _const_spec(cmapt.shape),
            _const_spec(hconst.shape),
        ],
        out_specs=[
            pl.BlockSpec((None, tq, qw), lambda bi, i: (bi, i, 0)),
            pl.BlockSpec((None, N_GROUPS, jp, tq), lambda bi, i: (bi, 0, 0, i)),
        ],
        out_shape=[
            jax.ShapeDtypeStruct((b, s, qw), BF16),
            jax.ShapeDtypeStruct((b, N_GROUPS, jp, s), F32),
        ],
        compiler_params=_params("parallel", "arbitrary"),
        name="nsa_select",
    )(main, cmp, gates, cmapt, hconst)


def _nsa_attn_kernel(*refs, tq, n_slc, n_sel, n_chain):
    (q_ref, kk_ref, vv_ref, ocmp_ref, mask_ref, gate_ref, kpos_ref, hcol_ref, wb_ref, sb_ref,
     o_ref, kaug_s_ref, kaug_w_ref, vts_ref, vtw_ref, qsel_ref, qwin_ref, acc_ref) = refs
    chains = range(n_chain)
    i = pl.program_id(2)
    t0 = i * tq
    tk = tq
    n_kt = kaug_s_ref.shape[1] // tk
    gw = HPG * HEAD_DIM

    @pl.when(i == 0)
    def _():
        top = lax.broadcasted_iota(jnp.int32, (LANES, tk), 0) < HEAD_DIM
        for c in chains:
            kk = kk_ref[:, c * LANES:(c + 1) * LANES].astype(F32)
            first = lax.broadcasted_iota(jnp.int32, kk.shape, 1) < HEAD_DIM
            aux = kpos_ref[...].astype(F32)
            kaug_s_ref[c] = jnp.where(first, kk, aux).astype(BF16)
            kaug_w_ref[c] = jnp.where(first, pltpu.roll(kk, HEAD_DIM, 1), aux).astype(BF16)
            for j in range(n_kt):
                vt = vv_ref[j * tk:(j + 1) * tk, c * LANES:(c + 1) * LANES].astype(F32).T
                vts_ref[c, j] = jnp.where(top, vt, 1.0).astype(BF16)
                vtw_ref[c, j] = jnp.where(top, 1.0, vt).astype(BF16)

    q_t = [q_ref[:, c * gw:(c + 1) * gw].astype(F32).T for c in chains]

    def q_operand(c, h, aux_t):
        return jnp.concatenate([q_t[c][h * HEAD_DIM:(h + 1) * HEAD_DIM], aux_t], axis=0).astype(BF16)

    def aux_rows(c, h):
        return jnp.concatenate([hcol_ref[c, h]] * (tq // LANES), axis=1)

    def scores(kref, qref, c, j0, nt):
        kt = kref[c, pl.ds(pl.multiple_of(j0 * tk, tk), nt * tk), :]
        return _dot(kt, qref[c])

    def pv(vt_ref, c, j0, nt, p):
        acc = _dot(vt_ref[c, j0], p[0:tk])
        for u in range(1, nt):
            acc = acc + _dot(vt_ref[c, j0 + u], p[u * tk:(u + 1) * tk])
        return acc

    def heads(bias):
        return jnp.concatenate([bias] * HPG, axis=1)

    for c in chains:
        for h in range(HPG):
            qwin_ref[c, :, h * tq:(h + 1) * tq] = q_operand(c, h, aux_rows(c, h))
    nwt = min(WINDOW // tk + 1, n_kt)
    jw = jnp.clip(i - (nwt - 1), 0, n_kt - nwt)
    wbias = wb_ref[i - jw]

    jp = mask_ref.shape[1]
    for c in chains:
        mb_t = jnp.concatenate([mask_ref[c], jnp.zeros((HEAD_DIM - jp, tq), F32)], axis=0)
        for h in range(HPG):
            qsel_ref[c, :, h * tq:(h + 1) * tq] = q_operand(c, h, mb_t + aux_rows(c, h))

    s_w = [scores(kaug_w_ref, qwin_ref, c, jw, nwt) for c in chains]

    def window_head(c, h):
        s_h = s_w[c][:, h * tq:(h + 1) * tq] + wbias
        p_h = jnp.exp2(s_h - jnp.max(s_h, axis=0, keepdims=True)).astype(BF16)
        acc = pv(vtw_ref, c, jw, nwt, p_h)
        return acc[HEAD_DIM:] * (1.0 / acc[0:1])

    pair = i // 2
    sbias = heads(sb_ref[i - 2 * pair])

    def sel_chunk(j0, carry):
        s, out = [None] * n_chain, [None] * n_chain

        def stage_scores(c):
            s[c] = scores(kaug_s_ref, qsel_ref, c, j0, 2)
            if carry is None:
                s[c] = s[c] + sbias

        def stage_pv(c):
            m_tile = jnp.max(s[c], axis=0, keepdims=True)
            if carry is None:
                out[c] = m_tile
                acc_ref[c] = pv(vts_ref, c, j0, 2, jnp.exp2(s[c] - m_tile).astype(BF16))
            else:
                out[c] = jnp.maximum(carry[c], m_tile)
                p = jnp.exp2(s[c] - out[c]).astype(BF16)
                acc_ref[c] = jnp.exp2(carry[c] - out[c]) * acc_ref[c] + pv(vts_ref, c, j0, 2, p)

        if carry is None:
            for c in chains:
                stage_scores(c)
                for h in range(HPG):
                    o_win_heads[c].append(window_head(c, h))
            for c in chains:
                stage_pv(c)
            return tuple(out)
        for step in range(n_chain + NSA_LAG):
            if step < n_chain:
                stage_scores(step)
            if 0 <= step - NSA_LAG < n_chain:
                stage_pv(step - NSA_LAG)
        return tuple(out)

    o_win_heads = [[] for c in chains]
    lax.fori_loop(0, pair, lambda jj, carry: sel_chunk(2 * (pair - 1 - jj), carry), sel_chunk(2 * pair, None))

    gt_t = gate_ref[...].T
    for c in chains:
        acc_s = acc_ref[c]
        o_sel_t = acc_s[:HEAD_DIM] * (1.0 / acc_s[HEAD_DIM:HEAD_DIM + 1])
        o_win_t = jnp.concatenate(o_win_heads[c], axis=1)
        rows = []
        for h in range(HPG):
            sl = slice(h * tq, (h + 1) * tq)
            g_sel, g_win = (gt_t[br * N_HEADS + c * HPG + h:br * N_HEADS + c * HPG + h + 1] for br in (1, 2))
            rows.append(g_sel * o_sel_t[:, sl] + g_win * o_win_t[:, sl])
        gated = ocmp_ref[:, c * gw:(c + 1) * gw].astype(F32) + jnp.concatenate(rows, axis=0).T
        o_ref[:, c * gw:(c + 1) * gw] = gated.astype(o_ref.dtype)


def _nsa_attn(main, ocmp, mask, gates, kpos, hcol, wbias, sbias, tq):
    b, s, _ = main.shape
    assert (s // tq) % 2 == 0 and NSA_CHAINS == N_GROUPS
    nch = NSA_CHAINS
    n_slc = s // SLC_BLOCK
    jp = mask.shape[2]
    gw = HPG * HEAD_DIM
    qw, kvw = N_GROUPS * gw, N_GROUPS * LANES
    assert qw % kvw == 0

    kern =functools.partial(_nsa_attn_kernel, tq=tq, n_slc=n_slc, n_sel=min(N_SLC_BLOCKS, n_slc), n_chain=nch)
    return pl.pallas_call(
        kern,
        grid=(b, N_GROUPS // nch, s // tq),
        in_specs=(
            [
                pl.BlockSpec((None, tq, qw), lambda bi, gp, i: (bi, i, 0)),
                pl.BlockSpec((None, s, kvw), lambda bi, gp, i: (bi, 0, qw // kvw)),
                pl.BlockSpec((None, s, kvw), lambda bi, gp, i: (bi, 0, qw // kvw + 1)),
            ]
            + [
                pl.BlockSpec((None, tq, qw), lambda bi, gp, i: (bi, i, 0)),
                pl.BlockSpec((None, nch, jp, tq), lambda bi, gp, i: (bi, 0, 0, i)),
                pl.BlockSpec((None, tq, LANES), lambda bi, gp, i: (bi, i, 0)),
                _const_spec(kpos.shape),
                pl.BlockSpec((nch, HPG, HEAD_DIM, LANES), lambda bi, gp, i: (gp, 0, 0, 0)),
                _const_spec(wbias.shape),
                _const_spec(sbias.shape),
            ]),
        out_specs=pl.BlockSpec((None, tq, nch * gw), lambda bi, gp, i: (bi, i, gp)),
        out_shape=jax.ShapeDtypeStruct((b, s, N_GROUPS * gw), BF16),
        scratch_shapes=[
            pltpu.VMEM((nch, s, LANES), BF16),
            pltpu.VMEM((nch, s, LANES), BF16),
            pltpu.VMEM((nch, s // tq, LANES, tq), BF16),
            pltpu.VMEM((nch, s // tq, LANES, tq), BF16),
            pltpu.VMEM((nch, LANES, HPG * tq), BF16),
            pltpu.VMEM((nch, LANES, HPG * tq), BF16),
            pltpu.VMEM((nch, LANES, HPG * tq), F32),
        ],
        compiler_params=_params("parallel", "parallel", "arbitrary"),
        name="nsa_attn",
    )(main, main, main, ocmp, mask, gates, kpos, hcol, wbias, sbias)


def _sb_attn_kernel(q_ref, k_ref, v_ref, tri_ref, o_ref, vt_ref, acc_ref, *, tq):
    i = pl.program_id(2)
    tk = tq
    n_kt = k_ref.shape[0] // tk
    n_chain = q_ref.shape[1] // LANES

    @pl.when(i == 0)
    def _():
        for c in range(n_chain):
            for j in range(n_kt):
                vt_ref[c, j] = v_ref[j * tk:(j + 1) * tk, c * LANES:(c + 1) * LANES].astype(F32).T.astype(BF16)

    low = lax.broadcasted_iota(jnp.int32, (tq, LANES), 1) < HEAD_DIM
    before = (lax.broadcasted_iota(jnp.int32, (tk, 2 * tq), 0)
              < (lax.broadcasted_iota(jnp.int32, (tk, 2 * tq), 1) & (tq - 1)))
    tri = tri_ref[...]

    qst = []
    for c in range(n_chain):
        q2 = q_ref[:, c * LANES:(c + 1) * LANES].astype(F32)
        qst.append(jnp.concatenate([jnp.where(low, q2, 0.0), jnp.where(low, 0.0, q2)], axis=0).astype(BF16))

    chains = range(n_chain)

    def tiles(j, diag, carry):
        off = pl.multiple_of(j * tk, tk)
        z, sp, sums = [None] * n_chain, [None] * n_chain, [None] * n_chain

        def scores(c):
            z[c] = _dot_nt(k_ref[pl.ds(off, tk), c * LANES:(c + 1) * LANES], qst[c])
            sp[c] = jnp.maximum(z[c], 0.0) + jnp.log(1.0 + jnp.exp2(_neg_abs(z[c]))) * LOG2E
            if diag:
                sp[c] = jnp.where(before, sp[c], 0.0)

        def suffix(c):
            sums[c] = _dot(tri, sp[c].astype(BF16))

        def weights(c):
            if diag:
                a = jnp.where(before, jnp.exp2(z[c] - sp[c] + sums[c][:tk]), 0.0)
            else:
                a = jnp.exp2(z[c] - sp[c] + sums[c][:tk] + carry[c])
            pv = _dot(vt_ref[c, j], a.astype(BF16))
            if diag:
                acc_ref[c] = pv
            else:
                acc_ref[c] += pv

        for step in range(n_chain + 2 * SB_LAG):
            for d, stage in enumerate((scores, suffix, weights)):
                if 0 <= step - d * SB_LAG < n_chain:
                    stage(step - d * SB_LAG)
        if diag:
            return tuple(sums[c][tk:tk + 1] for c in chains)
        return tuple(carry[c] + sums[c][tk:tk + 1] for c in chains)

    res = lax.fori_loop(0, i, lambda jj, carry: tiles(i - 1 - jj, False, carry), tiles(i, True, None))
    for c in range(n_chain):
        acc = acc_ref[c]
        out_t = jnp.concatenate([acc[:HEAD_DIM, :tq], acc[HEAD_DIM:, tq:]], axis=0)
        o_ref[:, c * LANES:(c + 1) * LANES] = out_t.T.astype(o_ref.dtype)


def _sb_attn(qkv, tri, tq):
    b, s, n3 = qkv.shape
    w = LANES * SB_CHAINS
    nblk = n3 // 3 // w
    return pl.pallas_call(
        functools.partial(_sb_attn_kernel, tq=tq),
        grid=(b, nblk, s // tq),
        in_specs=[
            pl.BlockSpec((None, tq, w), lambda bi, p, i: (bi, i, p)),
            pl.BlockSpec((None, s, w), lambda bi, p, i: (bi, 0, nblk + p)),
            pl.BlockSpec((None, s, w), lambda bi, p, i: (bi, 0, 2 * nblk + p)),
            _const_spec(tri.shape),
        ],
        out_specs=pl.BlockSpec((None, tq, w), lambda bi, p, i: (bi, i, p)),
        out_shape=jax.ShapeDtypeStruct((b, s, nblk * w), BF16),
        scratch_shapes=[pltpu.VMEM((SB_CHAINS, s // tq, LANES, tq), BF16),
                        pltpu.VMEM((SB_CHAINS, LANES, 2 * tq), F32)],
        compiler_params=_params("parallel", "parallel", "arbitrary"),
        name="sb_attn",
    )(qkv, qkv, qkv, tri)


def _mix_ffn_kernel(h_ref, o_ref, wo_ref, g_ref, w1_ref, w3_ref, w2_ref, fg_ref, out_ref, a_ref, acc_ref,
                    *, final, tf):
    h1 = h_ref[...] + _dot(o_ref[...], wo_ref[...])
    a_ref[...] = _rmsnorm(h1, g_ref[...]).astype(BF16)
    acc_ref[...] = h1

    for c in range(w1_ref.shape[1] // tf):
        cols = slice(c * tf, (c + 1) * tf)
        a = a_ref[...]
        u = _dot(a, w1_ref[:, cols])
        v = _dot(a, w3_ref[:, cols])
        t = (u * jax.nn.sigmoid(u)) * v
        acc_ref[...] += _dot(t.astype(BF16), w2_ref[cols, :])
    h2 = acc_ref[...]
    out_ref[...] = _rmsnorm(h2, fg_ref[...]) if final else h2


def _mix_ffn(h, o, wo, g, w1, w3, w2, fg, tm, tf, final):
    r, d = h.shape
    single = pl.Buffered(1)

    def wspec(shape):
        nd = len(shape)
        return pl.BlockSpec(shape, lambda *_: (0,) * nd, pipeline_mode=single)

    return pl.pallas_call(
        functools.partial(_mix_ffn_kernel, final=final, tf=tf),
        grid=(r // tm,),
        in_specs=[
            pl.BlockSpec((tm, d), lambda i: (i, 0)),
            pl.BlockSpec((tm, d), lambda i: (i, 0)),
            wspec(wo.shape),
            _const_spec((1, d)),
            wspec(w1.shape),
            wspec(w3.shape),
            wspec(w2.shape),
            _const_spec((1, d)),
        ],
        out_specs=pl.BlockSpec((tm, d), lambda i: (i, 0)),
        out_shape=jax.ShapeDtypeStruct((r, d), F32),
        scratch_shapes=[pltpu.VMEM((tm, d), BF16), pltpu.VMEM((tm, d), F32)],
        compiler_params=_params("parallel"),
        name="mix_ffn",
    )(h, o, wo, g, w1, w3, w2, fg)


def _nsa_layout(d):
    kvw = N_GROUPS * HEAD_DIM
    off = {"q": 0, "kc": d, "vc": d + kvw, "ksl": d + 2 * kvw, "vsl": d + 3 * kvw, "kw": d + 4 * kvw,
           "vw": d + 5 * kvw, "gt": d + 6 * kvw}
    main = [np.arange(d)]
    for pair in (("ksl", "kw"), ("vsl", "vw")):
        for g in range(N_GROUPS):
            for name in pair:
                main.append(off[name] + g * HEAD_DIM + np.arange(HEAD_DIM))
    main = np.concatenate(main)
    cv = off["kc"] + np.arange(2 * kvw)
    scale = np.where(main < d, HEAD_DIM ** -0.5 * LOG2E, 1.0).astype(np.float32)
    return main, scale, cv, off["gt"]


def _bf16_pieces(x):
    x = np.asarray(x, np.float32)
    out = []
    for _ in range(3):
        p = x.astype(BF16).astype(np.float32)
        out.append(p)
        x = (x - p).astype(np.float32)
    return out


def _nsa_tables(s, tk):
    n_slc = s // SLC_BLOCK
    n_cmp = s // CMP_STRIDE - CMP_BLOCK // CMP_STRIDE + 1
    nc = s // CMP_STRIDE
    pos = np.arange(s)
    kpos = np.zeros((s, LANES), np.float32)
    assert n_slc <= POS_LANE - HEAD_DIM
    kpos[pos, HEAD_DIM + pos // SLC_BLOCK] = 1.0
    for c in range(3):
        kpos[:, POS_LANE + c] = (pos // tk) * tk
        kpos[:, POS_LANE + 3 + c] = pos % tk
    cs = np.arange(n_cmp) * CMP_STRIDE
    ss = np.arange(n_slc) * SLC_BLOCK
    ov = np.clip(np.minimum(cs[:, None] + CMP_BLOCK, ss[None] + SLC_BLOCK) - np.maximum(cs[:, None], ss[None]), 0, None)
    jp = -(-n_slc // 16) * 16
    cmapt = np.zeros((jp, nc), np.float32)
    cmapt[:n_slc, :n_cmp] = (ov / CMP_STRIDE).T
    slopes = np.asarray(2.0 ** (-8.0 * (np.arange(N_HEADS) + 1) / N_HEADS), np.float32)
    slopes = (slopes.astype(np.float64) * LOG2E).astype(np.float32)
    pieces = _bf16_pieces(slopes)
    hconst = np.zeros((N_GROUPS, 2 * HPG, LANES), np.float32)
    hcol = np.zeros((N_GROUPS, HPG, HEAD_DIM, LANES), np.float32)
    for g in range(N_GROUPS):
        for hh in range(HPG):
            for c in range(3):
                hcol[g, hh, POS_LANE - HEAD_DIM + c, :] = pieces[c][g * HPG + hh]
                hcol[g, hh, POS_LANE - HEAD_DIM + 3 + c, :] = pieces[c][g * HPG + hh]
            hconst[g, hh, :] = slopes[g * HPG + hh]
    n_kt = s // tk
    nwt = min(WINDOW // tk + 1, n_kt)
    tt = np.arange(tk)[None, :]

    def dist(r, nt):
        return r * tk + tt - np.arange(nt * tk)[:, None]

    wbias = np.stack([np.where((dist(r, nwt) >= 0) & (dist(r, nwt) < WINDOW), 0.0, NEG) for r in range(nwt)])
    sbias = np.stack([np.where(dist(r, 2) >= 0, 0.0, NEG) for r in range(2)])
    return (jnp.asarray(kpos, BF16), jnp.asarray(cmapt, BF16), jnp.asarray(hconst, F32), jnp.asarray(hcol, F32),
            jnp.asarray(wbias, F32), jnp.asarray(sbias, F32))


def _ffn_chunk(d_ff):
    for tf in (512, 256, 128):
        if d_ff % tf == 0:
            return tf
    return d_ff


def kernel(x, mix_norm_g, ffn_norm_g, final_norm_g, nsa_w_in, nsa_gate_b, nsa_cmp_pe, nsa_cmp_w1, nsa_cmp_w2,
           nsa_w_out, sb_w_in, sb_w_out, ffn_w1, ffn_w3, ffn_w2):
    b, s, d = x.shape
    depth = mix_norm_g.shape[0]
    d_ff = ffn_w1.shape[2]
    tm = min(512, s)
    tm_proj = min(1024, s)
    tq = min(256, s)
    tf = _ffn_chunk(d_ff)

    main_idx, main_scale, cv_idx, gt_off = _nsa_layout(d)
    sb_scale = np.where(np.arange(3 * d) < d, HEAD_DIM ** -0.5 * LOG2E, 1.0).astype(np.float32)
    n_gate = N_BRANCH * N_HEADS
    kpos, cmapt, hconst, hcol, wbias, sbias = _nsa_tables(s, tq)
    tri = jnp.asarray(np.concatenate([-np.triu(np.ones((tq, tq), np.float32), 1),
                                      -np.ones((16, tq), np.float32)], axis=0), BF16)
    chunk = CMP_STRIDE * HEAD_DIM
    nc = s // CMP_STRIDE

    h = x
    for i in range(depth):
        j = i // 2
        gm = mix_norm_g[i].reshape(1, d)
        if i % 2 == 0:
            w_in = nsa_w_in[j]
            w_main = (w_in[:, main_idx] * main_scale).astype(BF16)
            w_cv = w_in[:, cv_idx].astype(BF16)
            w_gate = jnp.pad(w_in[:, gt_off:gt_off + n_gate], ((0, 0), (0, LANES - n_gate))).astype(BF16)
            gate_b = jnp.pad(nsa_gate_b[j], (0, LANES - n_gate)).reshape(1, LANES)
            main, cv, gates = _nsa_proj(h, gm, w_main, w_cv, w_gate, gate_b, tm_proj)
            c = cv.reshape(b, 2, N_GROUPS // 2 * nc, 2 * chunk)
            pe = jnp.broadcast_to(nsa_cmp_pe[j].reshape(2, 2, CMP_STRIDE, 1, HEAD_DIM),
                                  (2, 2, CMP_STRIDE, 2, HEAD_DIM)).reshape(2, 2, 2 * chunk)
            w1 = nsa_cmp_w1[j].reshape(2, 2, CMP_STRIDE, 1, HEAD_DIM, -1)
            w1 = jnp.stack([jnp.concatenate([w1, jnp.zeros_like(w1)], axis=3),
                            jnp.concatenate([jnp.zeros_like(w1), w1], axis=3)], axis=2)
            w1 = w1.reshape(2, 2, 2, 2 * chunk, -1)
            w2pad = jnp.zeros((2, nsa_cmp_w2.shape[2], LANES), F32)
            w2pad = w2pad.at[0, :, :HEAD_DIM].set(nsa_cmp_w2[j, 0]).at[1, :, HEAD_DIM:].set(nsa_cmp_w2[j, 1])
            cmp = _nsa_compress(c, pe, w1.astype(BF16), w2pad.astype(BF16))
            ocmp, mask = _nsa_select(main, cmp, gates, cmapt, hconst, tq)
            o = _nsa_attn(main, ocmp, mask, gates, kpos, hcol, wbias, sbias, tq)
            w_out = nsa_w_out[j]
        else:
            qkv = _sb_proj(h, gm, (sb_w_in[j] * sb_scale).astype(BF16), tm_proj)
            o = _sb_attn(qkv, tri, tq)
            w_out = sb_w_out[j]
        final = i == depth - 1
        h = _mix_ffn(h.reshape(b * s, d), o.reshape(b * s, d), w_out.astype(BF16), ffn_norm_g[i].reshape(1, d),
                     ffn_w1[i].astype(BF16), ffn_w3[i].astype(BF16), ffn_w2[i].astype(BF16),
                     final_norm_g.reshape(1, d), tm, tf, final).reshape(b, s, d)
    return h
```

```python
import functools

import numpy as np
import jax
import jax.numpy as jnp
from jax import lax
from jax.experimental import pallas as pl
from jax.experimental.pallas import tpu as pltpu

F32 = jnp.float32
BF16 = jnp.bfloat16

RMS_EPS = 1e-6
NEG = -1e30
BIG = 1e30

N_HEADS = 16
HEAD_DIM = 64
N_GROUPS = 4
HPG = N_HEADS // N_GROUPS
N_BRANCH = 3
CMP_BLOCK = 32
CMP_STRIDE = 16
SLC_BLOCK = 64
N_SLC_BLOCKS = 8
WINDOW = 512

LANES = 128
POS_LANE = 96
LOG2E = 1.4426950408889634
VMEM_LIMIT = 56 * 1024 * 1024
SB_CHAINS = 8
NSA_CHAINS = 4
SB_LAG = SB_CHAINS
NSA_LAG = 2

_NT = (((1,), (1,)), ((), ()))


def _dot(a, b):
    return jnp.dot(a, b, preferred_element_type=F32)


def _dot_nt(a, b):
    return lax.dot_general(a, b, _NT, preferred_element_type=F32)


def _rmsnorm(x, g):
    ms = jnp.mean(x * x, axis=-1, keepdims=True)
    return (x * lax.rsqrt(ms + RMS_EPS)) * g


def _split_bf16(x):
    hi = x.astype(BF16)
    lo = (x - hi.astype(F32)).astype(BF16)
    return hi, lo


def _neg_abs(x):
    return lax.bitcast_convert_type(lax.bitcast_convert_type(x, jnp.uint32) | jnp.uint32(0x80000000), F32)


def _params(*sem):
    return pltpu.CompilerParams(dimension_semantics=sem, vmem_limit_bytes=VMEM_LIMIT)


def _const_spec(shape):
    nd = len(shape)
    return pl.BlockSpec(shape, lambda *_: (0,) * nd)


def _nsa_proj_kernel(x_ref, g_ref, wm_ref, wc_ref, wg_ref, gb_ref, main_ref, cv_ref, gate_ref, cv_tmp_ref):
    a = _rmsnorm(x_ref[...], g_ref[...]).astype(BF16)
    _nsa_proj_body(a, wm_ref, wc_ref, wg_ref, gb_ref, main_ref, cv_ref, gate_ref, cv_tmp_ref)


def _nsa_proj_body(a, wm_ref, wc_ref, wg_ref, gb_ref, main_ref, cv_ref, gate_ref, cv_tmp_ref):
    main_ref[...] = _dot(a, wm_ref[...]).astype(main_ref.dtype)
    cv = _dot(a, wc_ref[...])
    n_chunks = cv.shape[0] // CMP_STRIDE
    for kv in range(2):
        for gp in range(N_GROUPS // 2):
            c0 = (kv * N_GROUPS + 2 * gp) * HEAD_DIM
            cv_tmp_ref[kv, gp] = cv[:, c0:c0 + LANES]
            for l in range(CMP_STRIDE):
                cv_ref[kv, gp, :, l * LANES:(l + 1) * LANES] = cv_tmp_ref[kv, gp,
                                                                          pl.ds(l, n_chunks, stride=CMP_STRIDE), :]
    gate_ref[...] = jax.nn.sigmoid(_dot(a, wg_ref[...]) + gb_ref[...])


def _nsa_proj(h, g, w_main, w_cv, w_gate, gate_b, tm):
    b, s, d = h.shape
    nm, ng = w_main.shape[1], w_gate.shape[1]
    return pl.pallas_call(
        _nsa_proj_kernel,
        grid=(b, s // tm),
        in_specs=[
            pl.BlockSpec((None, tm, d), lambda bi, r: (bi, r, 0)),
            _const_spec((1, d)),
            _const_spec(w_main.shape),
            _const_spec(w_cv.shape),
            _const_spec(w_gate.shape),
            _const_spec((1, ng)),
        ],
        out_specs=[
            pl.BlockSpec((None, tm, nm), lambda bi, r: (bi, r, 0)),
            pl.BlockSpec((None, 2, N_GROUPS // 2, tm // CMP_STRIDE, CMP_STRIDE * LANES),
                         lambda bi, r: (bi, 0, 0, r, 0)),
            pl.BlockSpec((None, tm, ng), lambda bi, r: (bi, r, 0)),
        ],
        out_shape=[
            jax.ShapeDtypeStruct((b, s, nm), BF16),
            jax.ShapeDtypeStruct((b, 2, N_GROUPS // 2, s // CMP_STRIDE, CMP_STRIDE * LANES), F32),
            jax.ShapeDtypeStruct((b, s, ng), F32),
        ],
        scratch_shapes=[pltpu.VMEM((2, N_GROUPS // 2, tm, LANES), F32)],
        compiler_params=_params("parallel", "parallel"),
        name="nsa_proj",
    )(h, g, w_main, w_cv, w_gate, gate_b)


def _sb_proj_kernel(x_ref, g_ref, w_ref, o_ref):
    a = _rmsnorm(x_ref[...], g_ref[...]).astype(BF16)
    o_ref[...] = _dot(a, w_ref[...]).astype(o_ref.dtype)


def _sb_proj(h, g, w, tm):
    b, s, d = h.shape
    n = w.shape[1]
    return pl.pallas_call(
        _sb_proj_kernel,
        grid=(b, s // tm),
        in_specs=[
            pl.BlockSpec((None, tm, d), lambda bi, r: (bi, r, 0)),
            _const_spec((1, d)),
            _const_spec(w.shape),
        ],
        out_specs=pl.BlockSpec((None, tm, n), lambda bi, r: (bi, r, 0)),
        out_shape=jax.ShapeDtypeStruct((b, s, n), BF16),
        compiler_params=_params("parallel", "parallel"),
        name="sb_proj",
    )(h, g, w)


def _nsa_compress_kernel(c_ref, pe_ref, w1_ref, w2_ref, o_ref):
    rows = c_ref.shape[1]
    nc = rows // (N_GROUPS // 2)
    out = [jnp.zeros((rows, LANES), F32) for _ in range(2)]
    for kv in range(2):
        c = c_ref[kv]
        top = (c + pe_ref[kv, 0:1, :]).astype(BF16)
        bot = (c + pe_ref[kv, 1:2, :]).astype(BF16)
        for e in range(2):
            a_top = _dot(top, w1_ref[kv, 0, e])
            a_bot = _dot(bot, w1_ref[kv, 1, e])
            hid = jax.nn.gelu(a_top + pltpu.roll(a_bot, rows - 1, 0), approximate=True)
            out[e] = out[e] + _dot(hid.astype(BF16), w2_ref[kv])
    for gp in range(N_GROUPS // 2):
        for e in range(2):
            g = 2 * gp + e
            o_ref[g * nc:(g + 1) * nc, :] = out[e][gp * nc:(gp + 1) * nc]


def _nsa_compress(c, pe, w1, w2pad):
    b, _, rows, width = c.shape
    return pl.pallas_call(
        _nsa_compress_kernel,
        grid=(b,),
        in_specs=[
            pl.BlockSpec((None, 2, rows, width), lambda bi: (bi, 0, 0, 0)),
            _const_spec(pe.shape),
            _const_spec(w1.shape),
            _const_spec(w2pad.shape),
        ],
        out_specs=pl.BlockSpec((None, 2 * rows, LANES), lambda bi: (bi, 0, 0)),
        out_shape=jax.ShapeDtypeStruct((b, 2 * rows, LANES), F32),
        compiler_params=_params("parallel"),
        name="nsa_compress",
    )(c, pe, w1, w2pad)


def _nsa_attn_kernel(*refs, tq, n_slc, n_sel, n_chain):
    (q_ref, kk_ref, vv_ref, cmp_ref, gate_ref, kpos_ref, cmapt_ref, hconst_ref, hcol_ref, wb_ref, sb_ref,
     o_ref, kaug_s_ref, kaug_w_ref, vts_ref, vtw_ref, qsel_ref, qwin_ref, acc_ref) = refs
    chains = range(n_chain)
    i = pl.program_id(2)
    t0 = i * tq
    tk = tq
    n_kt = kaug_s_ref.shape[1] // tk
    gw = HPG * HEAD_DIM

    @pl.when(i == 0)
    def _():
        top = lax.broadcasted_iota(jnp.int32, (LANES, tk), 0) < HEAD_DIM
        for c in chains:
            kk = kk_ref[:, c * LANES:(c + 1) * LANES].astype(F32)
            first = lax.broadcasted_iota(jnp.int32, kk.shape, 1) < HEAD_DIM
            aux = kpos_ref[...].astype(F32)
            kaug_s_ref[c] = jnp.where(first, kk, aux).astype(BF16)
            kaug_w_ref[c] = jnp.where(first, pltpu.roll(kk, HEAD_DIM, 1), aux).astype(BF16)
            for j in range(n_kt):
                vt = vv_ref[j * tk:(j + 1) * tk, c * LANES:(c + 1) * LANES].astype(F32).T
                vts_ref[c, j] = jnp.where(top, vt, 1.0).astype(BF16)
                vtw_ref[c, j] = jnp.where(top, 1.0, vt).astype(BF16)

    q_t = [q_ref[:, c * gw:(c + 1) * gw].astype(F32).T for c in chains]

    def q_operand(c, h, aux_t):
        return jnp.concatenate([q_t[c][h * HEAD_DIM:(h + 1) * HEAD_DIM], aux_t], axis=0).astype(BF16)

    def aux_rows(c, h):
        return jnp.concatenate([hcol_ref[c, h]] * (tq // LANES), axis=1)

    def scores(kref, qref, c, j0, nt):
        kt = kref[c, pl.ds(pl.multiple_of(j0 * tk, tk), nt * tk), :]
        return _dot(kt, qref[c])

    def pv(vt_ref, c, j0, nt, p):
        acc = _dot(vt_ref[c, j0], p[0:tk])
        for u in range(1, nt):
            acc = acc + _dot(vt_ref[c, j0 + u], p[u * tk:(u + 1) * tk])
        return acc

    def heads(bias):
        return jnp.concatenate([bias] * HPG, axis=1)

    for c in chains:
        for h in range(HPG):
            qwin_ref[c, :, h * tq:(h + 1) * tq] = q_operand(c, h, aux_rows(c, h))
    nwt = min(WINDOW // tk + 1, n_kt)
    jw = jnp.clip(i - (nwt - 1), 0, n_kt - nwt)
    wbias = wb_ref[i - jw]

    nc = cmp_ref.shape[0] // n_chain
    ck = [cmp_ref[c * nc:(c + 1) * nc, :] for c in chains]
    first_c = lax.broadcasted_iota(jnp.int32, (nc, LANES), 1) < HEAD_DIM
    sc_all = [_dot(jnp.where(first_c, ck[c], 0.0).astype(BF16), qwin_ref[c]) for c in chains]
    s_w = [scores(kaug_w_ref, qwin_ref, c, jw, nwt) for c in chains]

    def window_head(c, h):
        s_h = s_w[c][:, h * tq:(h + 1) * tq] + wbias
        p_h = jnp.exp2(s_h - jnp.max(s_h, axis=0, keepdims=True)).astype(BF16)
        acc = pv(vtw_ref, c, jw, nwt, p_h)
        return acc[HEAD_DIM:] * (1.0 / acc[0:1])

    dist_c = ((t0 + lax.broadcasted_iota(jnp.int32, (nc, tq), 1))
              - (lax.broadcasted_iota(jnp.int32, (nc, tq), 0) * CMP_STRIDE + (CMP_BLOCK - 1)))
    valid_c = dist_c >= 0
    dist_cf = dist_c.astype(F32)
    psum, p_heads = [], []
    for c in chains:
        ps = jnp.zeros((nc, tq), F32)
        ph = []
        for h in range(HPG):
            slope = hconst_ref[c, h:h + 1, 0:1]
            s_c = jnp.where(valid_c, sc_all[c][:, h * tq:(h + 1) * tq] - slope * dist_cf, NEG)
            m_c = jnp.max(s_c, axis=0, keepdims=True)
            e_c = jnp.where(valid_c, jnp.exp2(s_c - m_c), 0.0)
            den = jnp.sum(e_c, axis=0, keepdims=True)
            p_c = e_c / jnp.maximum(den, 1e-30)
            ps = ps + p_c
            ph.append(p_c.astype(BF16))
        psum.append(ps)
        p_heads.append(ph)
    o_cmp_t = [_dot(ck[c].T.astype(BF16), jnp.concatenate(p_heads[c], axis=1))[HEAD_DIM:] for c in chains]

    cmt = cmapt_ref[...]
    jp = cmt.shape[0]
    imp_t = []
    for c in chains:
        p_hi, p_lo = _split_bf16(psum[c])
        imp_t.append(_dot(cmt, p_hi) + _dot(cmt, p_lo))
    jb = lax.broadcasted_iota(jnp.int32, (jp, tq), 0)
    cur = jnp.right_shift(t0 + lax.broadcasted_iota(jnp.int32, (jp, tq), 1), SLC_BLOCK.bit_length() - 1)
    causal_j = jb <= cur
    forced = (jb == 0) | (jb == cur) | (jb == cur - 1)
    val = [jnp.where(causal_j & forced, BIG, jnp.where(causal_j, imp_t[c], -BIG)) for c in chains]
    rank = [jnp.zeros((jp, tq), jnp.int32) for c in chains]
    o_win_heads = [[] for c in chains]
    for j in range(n_slc):
        if j % (n_slc // HPG) == 0:
            for c in chains:
                o_win_heads[c].append(window_head(c, j // (n_slc // HPG)))
        for c in chains:
            row = val[c][j:j + 1, :]
            beats = (row > val[c]) | ((row == val[c]) & (jb > j))
            rank[c] = rank[c] + beats.astype(jnp.int32)
    o_win_t = [jnp.concatenate(o_win_heads[c], axis=1) for c in chains]

    for c in chains:
        mb_t = jnp.where(causal_j & (rank[c] < n_sel), 0.0, NEG)
        mb_t = jnp.concatenate([mb_t, jnp.zeros((HEAD_DIM - jp, tq), F32)], axis=0)
        for h in range(HPG):
            qsel_ref[c, :, h * tq:(h + 1) * tq] = q_operand(c, h, mb_t + aux_rows(c, h))
    pair = i // 2
    sbias = heads(sb_ref[i - 2 * pair])

    def sel_chunk(j0, carry):
        s, out = [None] * n_chain, [None] * n_chain

        def stage_scores(c):
            s[c] = scores(kaug_s_ref, qsel_ref, c, j0, 2)
            if carry is None:
                s[c] = s[c] + sbias

        def stage_pv(c):
            m_tile = jnp.max(s[c], axis=0, keepdims=True)
            if carry is None:
                out[c] = m_tile
                acc_ref[c] = pv(vts_ref, c, j0, 2, jnp.exp2(s[c] - m_tile).astype(BF16))
            else:
                out[c] = jnp.maximum(carry[c], m_tile)
                p = jnp.exp2(s[c] - out[c]).astype(BF16)
                acc_ref[c] = jnp.exp2(carry[c] - out[c]) * acc_ref[c] + pv(vts_ref, c, j0, 2, p)

        for step in range(n_chain + NSA_LAG):
            if step < n_chain:
                stage_scores(step)
            if 0 <= step - NSA_LAG < n_chain:
                stage_pv(step - NSA_LAG)
        return tuple(out)

    lax.fori_loop(0, pair, lambda jj, carry: sel_chunk(2 * (pair - 1 - jj), carry), sel_chunk(2 * pair, None))

    gt_t = gate_ref[...].T
    for c in chains:
        acc_s = acc_ref[c]
        o_sel_t = acc_s[:HEAD_DIM] * (1.0 / acc_s[HEAD_DIM:HEAD_DIM + 1])
        rows = []
        for h in range(HPG):
            sl = slice(h * tq, (h + 1) * tq)
            g_cmp, g_sel, g_win = (gt_t[br * N_HEADS + c * HPG + h:br * N_HEADS + c * HPG + h + 1]
                                   for br in range(N_BRANCH))
            rows.append(g_cmp * o_cmp_t[c][:, sl] + g_sel * o_sel_t[:, sl] + g_win * o_win_t[c][:, sl])
        o_ref[:, c * gw:(c + 1) * gw] = jnp.concatenate(rows, axis=0).T.astype(o_ref.dtype)


def _nsa_attn(main, cmp, gates, kpos, cmapt, hconst, hcol, wbias, sbias, tq):
    b, s, _ = main.shape
    assert (s // tq) % 2 == 0 and NSA_CHAINS == N_GROUPS
    nch = NSA_CHAINS
    nc = cmp.shape[1] // N_GROUPS
    n_slc = s // SLC_BLOCK
    gw = HPG * HEAD_DIM
    qw, kvw = N_GROUPS * gw, N_GROUPS * LANES
    assert qw % kvw == 0

    kern =functools.partial(_nsa_attn_kernel, tq=tq, n_slc=n_slc, n_sel=min(N_SLC_BLOCKS, n_slc), n_chain=nch)
    return pl.pallas_call(
        kern,
        grid=(b, N_GROUPS // nch, s // tq),
        in_specs=(
            [
                pl.BlockSpec((None, tq, qw), lambda bi, gp, i: (bi, i, 0)),
                pl.BlockSpec((None, s, kvw), lambda bi, gp, i: (bi, 0, qw // kvw)),
                pl.BlockSpec((None, s, kvw), lambda bi, gp, i: (bi, 0, qw // kvw + 1)),
            ]
            + [
                pl.BlockSpec((None, nch * nc, LANES), lambda bi, gp, i: (bi, gp, 0)),
                pl.BlockSpec((None, tq, LANES), lambda bi, gp, i: (bi, i, 0)),
                _const_spec(kpos.shape),
                _const_spec(cmapt.shape),
                pl.BlockSpec((nch, 2 * HPG, LANES), lambda bi, gp, i: (gp, 0, 0)),
                pl.BlockSpec((nch, HPG, HEAD_DIM, LANES), lambda bi, gp, i: (gp, 0, 0, 0)),
                _const_spec(wbias.shape),
                _const_spec(sbias.shape),
            ]),
        out_specs=pl.BlockSpec((None, tq, nch * gw), lambda bi, gp, i: (bi, i, gp)),
        out_shape=jax.ShapeDtypeStruct((b, s, N_GROUPS * gw), BF16),
        scratch_shapes=[
            pltpu.VMEM((nch, s, LANES), BF16),
            pltpu.VMEM((nch, s, LANES), BF16),
            pltpu.VMEM((nch, s // tq, LANES, tq), BF16),
            pltpu.VMEM((nch, s // tq, LANES, tq), BF16),
            pltpu.VMEM((nch, LANES, HPG * tq), BF16),
            pltpu.VMEM((nch, LANES, HPG * tq), BF16),
            pltpu.VMEM((nch, LANES, HPG * tq), F32),
        ],
        compiler_params=_params("parallel", "parallel", "arbitrary"),
        name="nsa_attn",
    )(main, main, main, cmp, gates, kpos, cmapt, hconst, hcol, wbias, sbias)


def _sb_attn_kernel(q_ref, k_ref, v_ref, tri_ref, o_ref, vt_ref, acc_ref, *, tq):
    i = pl.program_id(2)
    tk = tq
    n_kt = k_ref.shape[0] // tk
    n_chain = q_ref.shape[1] // LANES

    @pl.when(i == 0)
    def _():
        for c in range(n_chain):
            for j in range(n_kt):
                vt_ref[c, j] = v_ref[j * tk:(j + 1) * tk, c * LANES:(c + 1) * LANES].astype(F32).T.astype(BF16)

    low = lax.broadcasted_iota(jnp.int32, (tq, LANES), 1) < HEAD_DIM
    before = (lax.broadcasted_iota(jnp.int32, (tk, 2 * tq), 0)
              < (lax.broadcasted_iota(jnp.int32, (tk, 2 * tq), 1) & (tq - 1)))
    tri = tri_ref[...]

    qst = []
    for c in range(n_chain):
        q2 = q_ref[:, c * LANES:(c + 1) * LANES].astype(F32)
        qst.append(jnp.concatenate([jnp.where(low, q2, 0.0), jnp.where(low, 0.0, q2)], axis=0).astype(BF16))

    chains = range(n_chain)

    def tiles(j, diag, carry):
        off = pl.multiple_of(j * tk, tk)
        z, sp, sums = [None] * n_chain, [None] * n_chain, [None] * n_chain

        def scores(c):
            z[c] = _dot_nt(k_ref[pl.ds(off, tk), c * LANES:(c + 1) * LANES], qst[c])
            sp[c] = jnp.maximum(z[c], 0.0) + jnp.log(1.0 + jnp.exp2(_neg_abs(z[c]))) * LOG2E
            if diag:
                sp[c] = jnp.where(before, sp[c], 0.0)

        def suffix(c):
            sums[c] = _dot(tri, sp[c].astype(BF16))

        def weights(c):
            if diag:
                a = jnp.where(before, jnp.exp2(z[c] - sp[c] + sums[c][:tk]), 0.0)
            else:
                a = jnp.exp2(z[c] - sp[c] + sums[c][:tk] + carry[c])
            pv = _dot(vt_ref[c, j], a.astype(BF16))
            if diag:
                acc_ref[c] = pv
            else:
                acc_ref[c] += pv

        for step in range(n_chain + 2 * SB_LAG):
            for d, stage in enumerate((scores, suffix, weights)):
                if 0 <= step - d * SB_LAG < n_chain:
                    stage(step - d * SB_LAG)
        if diag:
            return tuple(sums[c][tk:tk + 1] for c in chains)
        return tuple(carry[c] + sums[c][tk:tk + 1] for c in chains)

    res = lax.fori_loop(0, i, lambda jj, carry: tiles(i - 1 - jj, False, carry), tiles(i, True, None))
    for c in range(n_chain):
        acc = acc_ref[c]
        out_t = jnp.concatenate([acc[:HEAD_DIM, :tq], acc[HEAD_DIM:, tq:]], axis=0)
        o_ref[:, c * LANES:(c + 1) * LANES] = out_t.T.astype(o_ref.dtype)


def _sb_attn(qkv, tri, tq):
    b, s, n3 = qkv.shape
    w = LANES * SB_CHAINS
    nblk = n3 // 3 // w
    return pl.pallas_call(
        functools.partial(_sb_attn_kernel, tq=tq),
        grid=(b, nblk, s // tq),
        in_specs=[
            pl.BlockSpec((None, tq, w), lambda bi, p, i: (bi, i, p)),
            pl.BlockSpec((None, s, w), lambda bi, p, i: (bi, 0, nblk + p)),
            pl.BlockSpec((None, s, w), lambda bi, p, i: (bi, 0, 2 * nblk + p)),
            _const_spec(tri.shape),
        ],
        out_specs=pl.BlockSpec((None, tq, w), lambda bi, p, i: (bi, i, p)),
        out_shape=jax.ShapeDtypeStruct((b, s, nblk * w), BF16),
        scratch_shapes=[pltpu.VMEM((SB_CHAINS, s // tq, LANES, tq), BF16),
                        pltpu.VMEM((SB_CHAINS, LANES, 2 * tq), F32)],
        compiler_params=_params("parallel", "parallel", "arbitrary"),
        name="sb_attn",
    )(qkv, qkv, qkv, tri)


def _mix_ffn_kernel(*refs, final, tf, fuse_next):
    if fuse_next == "sb":
        (h_ref, o_ref, wo_ref, g_ref, w1_ref, w3_ref, w2_ref, fg_ref, gn_ref, wn_ref,
         out_ref, qkv_ref, a_ref, acc_ref) = refs
    elif fuse_next == "nsa":
        (h_ref, o_ref, wo_ref, g_ref, w1_ref, w3_ref, w2_ref, fg_ref, gn_ref, wm_ref, wc_ref, wg_ref, gb_ref,
         out_ref, main_ref, cv_ref, gate_ref, a_ref, acc_ref, cv_tmp_ref) = refs
    else:
        h_ref, o_ref, wo_ref, g_ref, w1_ref, w3_ref, w2_ref, fg_ref, out_ref, a_ref, acc_ref = refs
    h1 = h_ref[...] + _dot(o_ref[...], wo_ref[...])
    a_ref[...] = _rmsnorm(h1, g_ref[...]).astype(BF16)
    acc_ref[...] = h1

    for c in range(w1_ref.shape[1] // tf):
        cols = slice(c * tf, (c + 1) * tf)
        a = a_ref[...]
        u = _dot(a, w1_ref[:, cols])
        v = _dot(a, w3_ref[:, cols])
        t = (u * jax.nn.sigmoid(u)) * v
        acc_ref[...] += _dot(t.astype(BF16), w2_ref[cols, :])
    h2 = acc_ref[...]
    out_ref[...] = _rmsnorm(h2, fg_ref[...]) if final else h2
    if fuse_next == "sb":
        qkv_ref[...] = _dot(_rmsnorm(h2, gn_ref[...]).astype(BF16), wn_ref[...]).astype(qkv_ref.dtype)
    elif fuse_next == "nsa":
        _nsa_proj_body(_rmsnorm(h2, gn_ref[...]).astype(BF16), wm_ref, wc_ref, wg_ref, gb_ref,
                       main_ref, cv_ref, gate_ref, cv_tmp_ref)


def _mix_ffn(h, o, wo, g, w1, w3, w2, fg, tm, tf, final, next_proj=None, seq=None):
    r, d = h.shape
    single = pl.Buffered(1)

    def wspec(shape):
        nd = len(shape)
        return pl.BlockSpec(shape, lambda *_: (0,) * nd, pipeline_mode=single)

    in_specs = [
        pl.BlockSpec((tm, d), lambda i: (i, 0)),
        pl.BlockSpec((tm, d), lambda i: (i, 0)),
        wspec(wo.shape),
        _const_spec((1, d)),
        wspec(w1.shape),
        wspec(w3.shape),
        wspec(w2.shape),
        _const_spec((1, d)),
    ]
    out_specs = pl.BlockSpec((tm, d), lambda i: (i, 0))
    out_shape = jax.ShapeDtypeStruct((r, d), F32)
    args = (h, o, wo, g, w1, w3, w2, fg)
    scratch = [pltpu.VMEM((tm, d), BF16), pltpu.VMEM((tm, d), F32)]
    fuse_next = None
    if next_proj is not None and len(next_proj) == 2:
        fuse_next = "sb"
        gn, wn = next_proj
        in_specs += [_const_spec((1, d)), wspec(wn.shape)]
        out_specs = [out_specs, pl.BlockSpec((tm, wn.shape[1]), lambda i: (i, 0))]
        out_shape = [out_shape, jax.ShapeDtypeStruct((r, wn.shape[1]), BF16)]
        args += (gn, wn)
    elif next_proj is not None:
        fuse_next = "nsa"
        gn, wm, wc, wg, gb = next_proj
        per_seq = seq // tm
        in_specs += [_const_spec((1, d)), wspec(wm.shape), wspec(wc.shape), wspec(wg.shape), _const_spec(gb.shape)]
        out_specs = [out_specs,
                     pl.BlockSpec((tm, wm.shape[1]), lambda i: (i, 0)),
                     pl.BlockSpec((None, 2, N_GROUPS // 2, tm // CMP_STRIDE, CMP_STRIDE * LANES),
                                  lambda i: (i // per_seq, 0, 0, i % per_seq, 0)),
                     pl.BlockSpec((tm, wg.shape[1]), lambda i: (i, 0))]
        out_shape = [out_shape,
                     jax.ShapeDtypeStruct((r, wm.shape[1]), BF16),
                     jax.ShapeDtypeStruct((r // seq, 2, N_GROUPS // 2, seq // CMP_STRIDE, CMP_STRIDE * LANES), F32),
                     jax.ShapeDtypeStruct((r, wg.shape[1]), F32)]
        args += (gn, wm, wc, wg, gb)
        scratch.append(pltpu.VMEM((2, N_GROUPS // 2, tm, LANES), F32))
    return pl.pallas_call(
        functools.partial(_mix_ffn_kernel, final=final, tf=tf, fuse_next=fuse_next),
        grid=(r // tm,),
        in_specs=in_specs,
        out_specs=out_specs,
        out_shape=out_shape,
        scratch_shapes=scratch,
        compiler_params=_params("parallel"),
        name="mix_ffn",
    )(*args)


def _nsa_layout(d):
    kvw = N_GROUPS * HEAD_DIM
    off = {"q": 0, "kc": d, "vc": d + kvw, "ksl": d + 2 * kvw, "vsl": d + 3 * kvw, "kw": d + 4 * kvw,
           "vw": d + 5 * kvw, "gt": d + 6 * kvw}
    main = [np.arange(d)]
    for pair in (("ksl", "kw"), ("vsl", "vw")):
        for g in range(N_GROUPS):
            for name in pair:
                main.append(off[name] + g * HEAD_DIM + np.arange(HEAD_DIM))
    main = np.concatenate(main)
    cv = off["kc"] + np.arange(2 * kvw)
    scale = np.where(main < d, HEAD_DIM ** -0.5 * LOG2E, 1.0).astype(np.float32)
    return main, scale, cv, off["gt"]


def _bf16_pieces(x):
    x = np.asarray(x, np.float32)
    out = []
    for _ in range(3):
        p = x.astype(BF16).astype(np.float32)
        out.append(p)
        x = (x - p).astype(np.float32)
    return out


def _nsa_tables(s, tk):
    n_slc = s // SLC_BLOCK
    n_cmp = s // CMP_STRIDE - CMP_BLOCK // CMP_STRIDE + 1
    nc = s // CMP_STRIDE
    pos = np.arange(s)
    kpos = np.zeros((s, LANES), np.float32)
    assert n_slc <= POS_LANE - HEAD_DIM
    kpos[pos, HEAD_DIM + pos // SLC_BLOCK] = 1.0
    for c in range(3):
        kpos[:, POS_LANE + c] = (pos // tk) * tk
        kpos[:, POS_LANE + 3 + c] = pos % tk
    cs = np.arange(n_cmp) * CMP_STRIDE
    ss = np.arange(n_slc) * SLC_BLOCK
    ov = np.clip(np.minimum(cs[:, None] + CMP_BLOCK, ss[None] + SLC_BLOCK) - np.maximum(cs[:, None], ss[None]), 0, None)
    jp = -(-n_slc // 16) * 16
    cmapt = np.zeros((jp, nc), np.float32)
    cmapt[:n_slc, :n_cmp] = (ov / CMP_STRIDE).T
    slopes = np.asarray(2.0 ** (-8.0 * (np.arange(N_HEADS) + 1) / N_HEADS), np.float32)
    slopes = (slopes.astype(np.float64) * LOG2E).astype(np.float32)
    pieces = _bf16_pieces(slopes)
    hconst = np.zeros((N_GROUPS, 2 * HPG, LANES), np.float32)
    hcol = np.zeros((N_GROUPS, HPG, HEAD_DIM, LANES), np.float32)
    for g in range(N_GROUPS):
        for hh in range(HPG):
            for c in range(3):
                hcol[g, hh, POS_LANE - HEAD_DIM + c, :] = pieces[c][g * HPG + hh]
                hcol[g, hh, POS_LANE - HEAD_DIM + 3 + c, :] = pieces[c][g * HPG + hh]
            hconst[g, hh, :] = slopes[g * HPG + hh]
    n_kt = s // tk
    nwt = min(WINDOW // tk + 1, n_kt)
    tt = np.arange(tk)[None, :]

    def dist(r, nt):
        return r * tk + tt - np.arange(nt * tk)[:, None]

    wbias = np.stack([np.where((dist(r, nwt) >= 0) & (dist(r, nwt) < WINDOW), 0.0, NEG) for r in range(nwt)])
    sbias = np.stack([np.where(dist(r, 2) >= 0, 0.0, NEG) for r in range(2)])
    return (jnp.asarray(kpos, BF16), jnp.asarray(cmapt, BF16), jnp.asarray(hconst, F32), jnp.asarray(hcol, F32),
            jnp.asarray(wbias, F32), jnp.asarray(sbias, F32))


def _ffn_chunk(d_ff):
    for tf in (512, 256, 128):
        if d_ff % tf == 0:
            return tf
    return d_ff


def kernel(x, mix_norm_g, ffn_norm_g, final_norm_g, nsa_w_in, nsa_gate_b, nsa_cmp_pe, nsa_cmp_w1, nsa_cmp_w2,
           nsa_w_out, sb_w_in, sb_w_out, ffn_w1, ffn_w3, ffn_w2):
    b, s, d = x.shape
    depth = mix_norm_g.shape[0]
    d_ff = ffn_w1.shape[2]
    tm = min(512, s)
    tm_proj = min(1024, s)
    tq = min(256, s)
    tf = _ffn_chunk(d_ff)

    main_idx, main_scale, cv_idx, gt_off = _nsa_layout(d)
    sb_scale = np.where(np.arange(3 * d) < d, HEAD_DIM ** -0.5 * LOG2E, 1.0).astype(np.float32)
    n_gate = N_BRANCH * N_HEADS
    kpos, cmapt, hconst, hcol, wbias, sbias = _nsa_tables(s, tq)
    tri = jnp.asarray(np.concatenate([-np.triu(np.ones((tq, tq), np.float32), 1),
                                      -np.ones((16, tq), np.float32)], axis=0), BF16)
    chunk = CMP_STRIDE * HEAD_DIM
    nc = s // CMP_STRIDE

    def nsa_proj_weights(j):
        w_in = nsa_w_in[j]
        return ((w_in[:, main_idx] * main_scale).astype(BF16), w_in[:, cv_idx].astype(BF16),
                jnp.pad(w_in[:, gt_off:gt_off + n_gate], ((0, 0), (0, LANES - n_gate))).astype(BF16),
                jnp.pad(nsa_gate_b[j], (0, LANES - n_gate)).reshape(1, LANES))

    h, proj = x, None
    for i in range(depth):
        j = i // 2
        gm = mix_norm_g[i].reshape(1, d)
        if i % 2 == 0:
            main, cv, gates = proj if proj is not None else _nsa_proj(h, gm, *nsa_proj_weights(j), tm_proj)
            c = cv.reshape(b, 2, N_GROUPS // 2 * nc, 2 * chunk)
            pe = jnp.broadcast_to(nsa_cmp_pe[j].reshape(2, 2, CMP_STRIDE, 1, HEAD_DIM),
                                  (2, 2, CMP_STRIDE, 2, HEAD_DIM)).reshape(2, 2, 2 * chunk)
            w1 = nsa_cmp_w1[j].reshape(2, 2, CMP_STRIDE, 1, HEAD_DIM, -1)
            w1 = jnp.stack([jnp.concatenate([w1, jnp.zeros_like(w1)], axis=3),
                            jnp.concatenate([jnp.zeros_like(w1), w1], axis=3)], axis=2)
            w1 = w1.reshape(2, 2, 2, 2 * chunk, -1)
            w2pad = jnp.zeros((2, nsa_cmp_w2.shape[2], LANES), F32)
            w2pad = w2pad.at[0, :, :HEAD_DIM].set(nsa_cmp_w2[j, 0]).at[1, :, HEAD_DIM:].set(nsa_cmp_w2[j, 1])
            cmp = _nsa_compress(c, pe, w1.astype(BF16), w2pad.astype(BF16))
            o = _nsa_attn(main, cmp, gates, kpos, cmapt, hconst, hcol, wbias, sbias, tq)
            w_out = nsa_w_out[j]
        else:
            qkv = proj if proj is not None else _sb_proj(h, gm, (sb_w_in[j] * sb_scale).astype(BF16), tm_proj)
            o = _sb_attn(qkv, tri, tq)
            w_out = sb_w_out[j]
        final = i == depth - 1
        next_proj = None
        if not final:
            gn = mix_norm_g[i + 1].reshape(1, d)
            if i % 2 == 0:
                next_proj = (gn, (sb_w_in[(i + 1) // 2] * sb_scale).astype(BF16))
            else:
                next_proj = (gn,) + nsa_proj_weights((i + 1) // 2)
        res = _mix_ffn(h.reshape(b * s, d), o.reshape(b * s, d), w_out.astype(BF16), ffn_norm_g[i].reshape(1, d),
                       ffn_w1[i].astype(BF16), ffn_w3[i].astype(BF16), ffn_w2[i].astype(BF16),
                       final_norm_g.reshape(1, d), tm, tf, final, next_proj, s)
        if final:
            h, proj = res, None
        elif i % 2 == 0:
            h, proj = res[0], res[1].reshape(b, s, -1)
        else:
            h, proj = res[0], (res[1].reshape(b, s, -1), res[2], res[3].reshape(b, s, -1))
        h = h.reshape(b, s, d)
    return h
```

```python
import functools

import numpy as np
import jax
import jax.numpy as jnp
from jax import lax
from jax.experimental import pallas as pl
from jax.experimental.pallas import tpu as pltpu

F32 = jnp.float32
BF16 = jnp.bfloat16

RMS_EPS = 1e-6
NEG = -1e30
BIG = 1e30

N_HEADS = 16
HEAD_DIM = 64
N_GROUPS = 4
HPG = N_HEADS // N_GROUPS
N_BRANCH = 3
CMP_BLOCK = 32
CMP_STRIDE = 16
SLC_BLOCK = 64
N_SLC_BLOCKS = 8
WINDOW = 512

LANES = 128
POS_LANE = 96
LOG2E = 1.4426950408889634
VMEM_LIMIT = 56 * 1024 * 1024
SB_CHAINS = 8
NSA_CHAINS = 4
SB_LAG = SB_CHAINS
NSA_LAG = 2

_NT = (((1,), (1,)), ((), ()))


def _dot(a, b):
    return jnp.dot(a, b, preferred_element_type=F32)


def _dot_nt(a, b):
    return lax.dot_general(a, b, _NT, preferred_element_type=F32)


def _rmsnorm(x, g):
    ms = jnp.mean(x * x, axis=-1, keepdims=True)
    return (x * lax.rsqrt(ms + RMS_EPS)) * g


def _split_bf16(x):
    hi = x.astype(BF16)
    lo = (x - hi.astype(F32)).astype(BF16)
    return hi, lo


def _neg_abs(x):
    return lax.bitcast_convert_type(lax.bitcast_convert_type(x, jnp.uint32) | jnp.uint32(0x80000000), F32)


def _params(*sem):
    return pltpu.CompilerParams(dimension_semantics=sem, vmem_limit_bytes=VMEM_LIMIT)


def _const_spec(shape):
    nd = len(shape)
    return pl.BlockSpec(shape, lambda *_: (0,) * nd)


def _nsa_proj_kernel(x_ref, g_ref, wm_ref, wc_ref, wg_ref, gb_ref, main_ref, cv_ref, gate_ref, cv_tmp_ref):
    a = _rmsnorm(x_ref[...], g_ref[...]).astype(BF16)
    main_ref[...] = _dot(a, wm_ref[...]).astype(main_ref.dtype)
    cv = _dot(a, wc_ref[...])
    n_chunks = cv.shape[0] // CMP_STRIDE
    for kv in range(2):
        for gp in range(N_GROUPS // 2):
            c0 = (kv * N_GROUPS + 2 * gp) * HEAD_DIM
            cv_tmp_ref[kv, gp] = cv[:, c0:c0 + LANES]
            for l in range(CMP_STRIDE):
                cv_ref[kv, gp, :, l * LANES:(l + 1) * LANES] = cv_tmp_ref[kv, gp,
                                                                          pl.ds(l, n_chunks, stride=CMP_STRIDE), :]
    gate_ref[...] = jax.nn.sigmoid(_dot(a, wg_ref[...]) + gb_ref[...])


def _nsa_proj(h, g, w_main, w_cv, w_gate, gate_b, tm):
    b, s, d = h.shape
    nm, ng = w_main.shape[1], w_gate.shape[1]
    return pl.pallas_call(
        _nsa_proj_kernel,
        grid=(b, s // tm),
        in_specs=[
            pl.BlockSpec((None, tm, d), lambda bi, r: (bi, r, 0)),
            _const_spec((1, d)),
            _const_spec(w_main.shape),
            _const_spec(w_cv.shape),
            _const_spec(w_gate.shape),
            _const_spec((1, ng)),
        ],
        out_specs=[
            pl.BlockSpec((None, tm, nm), lambda bi, r: (bi, r, 0)),
            pl.BlockSpec((None, 2, N_GROUPS // 2, tm // CMP_STRIDE, CMP_STRIDE * LANES),
                         lambda bi, r: (bi, 0, 0, r, 0)),
            pl.BlockSpec((None, tm, ng), lambda bi, r: (bi, r, 0)),
        ],
        out_shape=[
            jax.ShapeDtypeStruct((b, s, nm), BF16),
            jax.ShapeDtypeStruct((b, 2, N_GROUPS // 2, s // CMP_STRIDE, CMP_STRIDE * LANES), F32),
            jax.ShapeDtypeStruct((b, s, ng), F32),
        ],
        scratch_shapes=[pltpu.VMEM((2, N_GROUPS // 2, tm, LANES), F32)],
        compiler_params=_params("parallel", "parallel"),
        name="nsa_proj",
    )(h, g, w_main, w_cv, w_gate, gate_b)


def _sb_proj_kernel(x_ref, g_ref, w_ref, o_ref):
    a = _rmsnorm(x_ref[...], g_ref[...]).astype(BF16)
    o_ref[...] = _dot(a, w_ref[...]).astype(o_ref.dtype)


def _sb_proj(h, g, w, tm):
    b, s, d = h.shape
    n = w.shape[1]
    return pl.pallas_call(
        _sb_proj_kernel,
        grid=(b, s // tm),
        in_specs=[
            pl.BlockSpec((None, tm, d), lambda bi, r: (bi, r, 0)),
            _const_spec((1, d)),
            _const_spec(w.shape),
        ],
        out_specs=pl.BlockSpec((None, tm, n), lambda bi, r: (bi, r, 0)),
        out_shape=jax.ShapeDtypeStruct((b, s, n), BF16),
        compiler_params=_params("parallel", "parallel"),
        name="sb_proj",
    )(h, g, w)


def _nsa_compress_kernel(c_ref, pe_ref, w1_ref, w2_ref, o_ref):
    rows = c_ref.shape[1]
    nc = rows // (N_GROUPS // 2)
    out = [jnp.zeros((rows, LANES), F32) for _ in range(2)]
    for kv in range(2):
        c = c_ref[kv]
        top = (c + pe_ref[kv, 0:1, :]).astype(BF16)
        bot = (c + pe_ref[kv, 1:2, :]).astype(BF16)
        for e in range(2):
            a_top = _dot(top, w1_ref[kv, 0, e])
            a_bot = _dot(bot, w1_ref[kv, 1, e])
            hid = jax.nn.gelu(a_top + pltpu.roll(a_bot, rows - 1, 0), approximate=True)
            out[e] = out[e] + _dot(hid.astype(BF16), w2_ref[kv])
    for gp in range(N_GROUPS // 2):
        for e in range(2):
            g = 2 * gp + e
            o_ref[g * nc:(g + 1) * nc, :] = out[e][gp * nc:(gp + 1) * nc]


def _nsa_compress(c, pe, w1, w2pad):
    b, _, rows, width = c.shape
    return pl.pallas_call(
        _nsa_compress_kernel,
        grid=(b,),
        in_specs=[
            pl.BlockSpec((None, 2, rows, width), lambda bi: (bi, 0, 0, 0)),
            _const_spec(pe.shape),
            _const_spec(w1.shape),
            _const_spec(w2pad.shape),
        ],
        out_specs=pl.BlockSpec((None, 2 * rows, LANES), lambda bi: (bi, 0, 0)),
        out_shape=jax.ShapeDtypeStruct((b, 2 * rows, LANES), F32),
        compiler_params=_params("parallel"),
        name="nsa_compress",
    )(c, pe, w1, w2pad)


def _nsa_attn_kernel(*refs, tq, n_slc, n_sel, n_chain):
    (q_ref, kk_ref, vv_ref, cmp_ref, gate_ref, kpos_ref, cmapt_ref, hconst_ref, hcol_ref, wb_ref, sb_ref,
     o_ref, kaug_s_ref, kaug_w_ref, vts_ref, vtw_ref, qsel_ref, qwin_ref, acc_ref) = refs
    chains = range(n_chain)
    i = pl.program_id(2)
    t0 = i * tq
    tk = tq
    n_kt = kaug_s_ref.shape[1] // tk
    gw = HPG * HEAD_DIM

    @pl.when(i == 0)
    def _():
        top = lax.broadcasted_iota(jnp.int32, (LANES, tk), 0) < HEAD_DIM
        for c in chains:
            kk = kk_ref[:, c * LANES:(c + 1) * LANES].astype(F32)
            first = lax.broadcasted_iota(jnp.int32, kk.shape, 1) < HEAD_DIM
            aux = kpos_ref[...].astype(F32)
            kaug_s_ref[c] = jnp.where(first, kk, aux).astype(BF16)
            kaug_w_ref[c] = jnp.where(first, pltpu.roll(kk, HEAD_DIM, 1), aux).astype(BF16)
            for j in range(n_kt):
                vt = vv_ref[j * tk:(j + 1) * tk, c * LANES:(c + 1) * LANES].astype(F32).T
                vts_ref[c, j] = jnp.where(top, vt, 1.0).astype(BF16)
                vtw_ref[c, j] = jnp.where(top, 1.0, vt).astype(BF16)

    q_t = [q_ref[:, c * gw:(c + 1) * gw].astype(F32).T for c in chains]

    def q_operand(c, h, aux_t):
        return jnp.concatenate([q_t[c][h * HEAD_DIM:(h + 1) * HEAD_DIM], aux_t], axis=0).astype(BF16)

    def aux_rows(c, h):
        return jnp.concatenate([hcol_ref[c, h]] * (tq // LANES), axis=1)

    def scores(kref, qref, c, j0, nt):
        kt = kref[c, pl.ds(pl.multiple_of(j0 * tk, tk), nt * tk), :]
        return _dot(kt, qref[c])

    def pv(vt_ref, c, j0, nt, p):
        acc = _dot(vt_ref[c, j0], p[0:tk])
        for u in range(1, nt):
            acc = acc + _dot(vt_ref[c, j0 + u], p[u * tk:(u + 1) * tk])
        return acc

    def heads(bias):
        return jnp.concatenate([bias] * HPG, axis=1)

    for c in chains:
        for h in range(HPG):
            qwin_ref[c, :, h * tq:(h + 1) * tq] = q_operand(c, h, aux_rows(c, h))
    nwt = min(WINDOW // tk + 1, n_kt)
    jw = jnp.clip(i - (nwt - 1), 0, n_kt - nwt)
    wbias = wb_ref[i - jw]

    nc = cmp_ref.shape[0] // n_chain
    ck = [cmp_ref[c * nc:(c + 1) * nc, :] for c in chains]
    first_c = lax.broadcasted_iota(jnp.int32, (nc, LANES), 1) < HEAD_DIM
    sc_all = [_dot(jnp.where(first_c, ck[c], 0.0).astype(BF16), qwin_ref[c]) for c in chains]
    s_w = [scores(kaug_w_ref, qwin_ref, c, jw, nwt) for c in chains]

    def window_head(c, h):
        s_h = s_w[c][:, h * tq:(h + 1) * tq] + wbias
        p_h = jnp.exp2(s_h - jnp.max(s_h, axis=0, keepdims=True)).astype(BF16)
        acc = pv(vtw_ref, c, jw, nwt, p_h)
        return acc[HEAD_DIM:] * (1.0 / acc[0:1])

    dist_c = ((t0 + lax.broadcasted_iota(jnp.int32, (nc, tq), 1))
              - (lax.broadcasted_iota(jnp.int32, (nc, tq), 0) * CMP_STRIDE + (CMP_BLOCK - 1)))
    valid_c = dist_c >= 0
    dist_cf = dist_c.astype(F32)
    psum, p_heads = [], []
    for c in chains:
        ps = jnp.zeros((nc, tq), F32)
        ph = []
        for h in range(HPG):
            slope = hconst_ref[c, h:h + 1, 0:1]
            s_c = jnp.where(valid_c, sc_all[c][:, h * tq:(h + 1) * tq] - slope * dist_cf, NEG)
            m_c = jnp.max(s_c, axis=0, keepdims=True)
            e_c = jnp.where(valid_c, jnp.exp2(s_c - m_c), 0.0)
            den = jnp.sum(e_c, axis=0, keepdims=True)
            p_c = e_c / jnp.maximum(den, 1e-30)
            ps = ps + p_c
            ph.append(p_c.astype(BF16))
        psum.append(ps)
        p_heads.append(ph)
    o_cmp_t = [_dot(ck[c].T.astype(BF16), jnp.concatenate(p_heads[c], axis=1))[HEAD_DIM:] for c in chains]

    cmt = cmapt_ref[...]
    jp = cmt.shape[0]
    imp_t = []
    for c in chains:
        p_hi, p_lo = _split_bf16(psum[c])
        imp_t.append(_dot(cmt, p_hi) + _dot(cmt, p_lo))
    jb = lax.broadcasted_iota(jnp.int32, (jp, tq), 0)
    cur = jnp.right_shift(t0 + lax.broadcasted_iota(jnp.int32, (jp, tq), 1), SLC_BLOCK.bit_length() - 1)
    causal_j = jb <= cur
    forced = (jb == 0) | (jb == cur) | (jb == cur - 1)
    val = [jnp.where(causal_j & forced, BIG, jnp.where(causal_j, imp_t[c], -BIG)) for c in chains]
    rank = [jnp.zeros((jp, tq), jnp.int32) for c in chains]
    o_win_heads = [[] for c in chains]
    for j in range(n_slc):
        if j % (n_slc // HPG) == 0:
            for c in chains:
                o_win_heads[c].append(window_head(c, j // (n_slc // HPG)))
        for c in chains:
            row = val[c][j:j + 1, :]
            beats = (row > val[c]) | ((row == val[c]) & (jb > j))
            rank[c] = rank[c] + beats.astype(jnp.int32)
    o_win_t = [jnp.concatenate(o_win_heads[c], axis=1) for c in chains]

    for c in chains:
        mb_t = jnp.where(causal_j & (rank[c] < n_sel), 0.0, NEG)
        mb_t = jnp.concatenate([mb_t, jnp.zeros((HEAD_DIM - jp, tq), F32)], axis=0)
        for h in range(HPG):
            qsel_ref[c, :, h * tq:(h + 1) * tq] = q_operand(c, h, mb_t + aux_rows(c, h))
    pair = i // 2
    sbias = heads(sb_ref[i - 2 * pair])

    def sel_chunk(j0, carry):
        s, out = [None] * n_chain, [None] * n_chain

        def stage_scores(c):
            s[c] = scores(kaug_s_ref, qsel_ref, c, j0, 2)
            if carry is None:
                s[c] = s[c] + sbias

        def stage_pv(c):
            m_tile = jnp.max(s[c], axis=0, keepdims=True)
            if carry is None:
                out[c] = m_tile
                acc_ref[c] = pv(vts_ref, c, j0, 2, jnp.exp2(s[c] - m_tile).astype(BF16))
            else:
                out[c] = jnp.maximum(carry[c], m_tile)
                p = jnp.exp2(s[c] - out[c]).astype(BF16)
                acc_ref[c] = jnp.exp2(carry[c] - out[c]) * acc_ref[c] + pv(vts_ref, c, j0, 2, p)

        for step in range(n_chain + NSA_LAG):
            if step < n_chain:
                stage_scores(step)
            if 0 <= step - NSA_LAG < n_chain:
                stage_pv(step - NSA_LAG)
        return tuple(out)

    lax.fori_loop(0, pair, lambda jj, carry: sel_chunk(2 * (pair - 1 - jj), carry), sel_chunk(2 * pair, None))

    gt_t = gate_ref[...].T
    for c in chains:
        acc_s = acc_ref[c]
        o_sel_t = acc_s[:HEAD_DIM] * (1.0 / acc_s[HEAD_DIM:HEAD_DIM + 1])
        rows = []
        for h in range(HPG):
            sl = slice(h * tq, (h + 1) * tq)
            g_cmp, g_sel, g_win = (gt_t[br * N_HEADS + c * HPG + h:br * N_HEADS + c * HPG + h + 1]
                                   for br in range(N_BRANCH))
            rows.append(g_cmp * o_cmp_t[c][:, sl] + g_sel * o_sel_t[:, sl] + g_win * o_win_t[c][:, sl])
        o_ref[:, c * gw:(c + 1) * gw] = jnp.concatenate(rows, axis=0).T.astype(o_ref.dtype)


def _nsa_attn(main, cmp, gates, kpos, cmapt, hconst, hcol, wbias, sbias, tq):
    b, s, _ = main.shape
    assert (s // tq) % 2 == 0 and NSA_CHAINS == N_GROUPS
    nch = NSA_CHAINS
    nc = cmp.shape[1] // N_GROUPS
    n_slc = s // SLC_BLOCK
    gw = HPG * HEAD_DIM
    qw, kvw = N_GROUPS * gw, N_GROUPS * LANES
    assert qw % kvw == 0

    kern =functools.partial(_nsa_attn_kernel, tq=tq, n_slc=n_slc, n_sel=min(N_SLC_BLOCKS, n_slc), n_chain=nch)
    return pl.pallas_call(
        kern,
        grid=(b, N_GROUPS // nch, s // tq),
        in_specs=(
            [
                pl.BlockSpec((None, tq, qw), lambda bi, gp, i: (bi, i, 0)),
                pl.BlockSpec((None, s, kvw), lambda bi, gp, i: (bi, 0, qw // kvw)),
                pl.BlockSpec((None, s, kvw), lambda bi, gp, i: (bi, 0, qw // kvw + 1)),
            ]
            + [
                pl.BlockSpec((None, nch * nc, LANES), lambda bi, gp, i: (bi, gp, 0)),
                pl.BlockSpec((None, tq, LANES), lambda bi, gp, i: (bi, i, 0)),
                _const_spec(kpos.shape),
                _const_spec(cmapt.shape),
                pl.BlockSpec((nch, 2 * HPG, LANES), lambda bi, gp, i: (gp, 0, 0)),
                pl.BlockSpec((nch, HPG, HEAD_DIM, LANES), lambda bi, gp, i: (gp, 0, 0, 0)),
                _const_spec(wbias.shape),
                _const_spec(sbias.shape),
            ]),
        out_specs=pl.BlockSpec((None, tq, nch * gw), lambda bi, gp, i: (bi, i, gp)),
        out_shape=jax.ShapeDtypeStruct((b, s, N_GROUPS * gw), BF16),
        scratch_shapes=[
            pltpu.VMEM((nch, s, LANES), BF16),
            pltpu.VMEM((nch, s, LANES), BF16),
            pltpu.VMEM((nch, s // tq, LANES, tq), BF16),
            pltpu.VMEM((nch, s // tq, LANES, tq), BF16),
            pltpu.VMEM((nch, LANES, HPG * tq), BF16),
            pltpu.VMEM((nch, LANES, HPG * tq), BF16),
            pltpu.VMEM((nch, LANES, HPG * tq), F32),
        ],
        compiler_params=_params("parallel", "parallel", "arbitrary"),
        name="nsa_attn",
    )(main, main, main, cmp, gates, kpos, cmapt, hconst, hcol, wbias, sbias)


def _sb_attn_kernel(q_ref, k_ref, v_ref, tri_ref, o_ref, vt_ref, acc_ref, *, tq):
    i = pl.program_id(2)
    tk = tq
    n_kt = k_ref.shape[0] // tk
    n_chain = q_ref.shape[1] // LANES

    @pl.when(i == 0)
    def _():
        for c in range(n_chain):
            for j in range(n_kt):
                vt_ref[c, j] = v_ref[j * tk:(j + 1) * tk, c * LANES:(c + 1) * LANES].astype(F32).T.astype(BF16)

    low = lax.broadcasted_iota(jnp.int32, (tq, LANES), 1) < HEAD_DIM
    before = (lax.broadcasted_iota(jnp.int32, (tk, 2 * tq), 0)
              < (lax.broadcasted_iota(jnp.int32, (tk, 2 * tq), 1) & (tq - 1)))
    tri = tri_ref[...]

    qst = []
    for c in range(n_chain):
        q2 = q_ref[:, c * LANES:(c + 1) * LANES].astype(F32)
        qst.append(jnp.concatenate([jnp.where(low, q2, 0.0), jnp.where(low, 0.0, q2)], axis=0).astype(BF16))

    chains = range(n_chain)

    def tiles(j, diag, carry):
        off = pl.multiple_of(j * tk, tk)
        z, sp, sums = [None] * n_chain, [None] * n_chain, [None] * n_chain

        def scores(c):
            z[c] = _dot_nt(k_ref[pl.ds(off, tk), c * LANES:(c + 1) * LANES], qst[c])
            sp[c] = jnp.maximum(z[c], 0.0) + jnp.log(1.0 + jnp.exp2(_neg_abs(z[c]))) * LOG2E
            if diag:
                sp[c] = jnp.where(before, sp[c], 0.0)

        def suffix(c):
            sums[c] = _dot(tri, sp[c].astype(BF16))

        def weights(c):
            if diag:
                a = jnp.where(before, jnp.exp2(z[c] - sp[c] + sums[c][:tk]), 0.0)
            else:
                a = jnp.exp2(z[c] - sp[c] + sums[c][:tk] + carry[c])
            pv = _dot(vt_ref[c, j], a.astype(BF16))
            if diag:
                acc_ref[c] = pv
            else:
                acc_ref[c] += pv

        for step in range(n_chain + 2 * SB_LAG):
            for d, stage in enumerate((scores, suffix, weights)):
                if 0 <= step - d * SB_LAG < n_chain:
                    stage(step - d * SB_LAG)
        if diag:
            return tuple(sums[c][tk:tk + 1] for c in chains)
        return tuple(carry[c] + sums[c][tk:tk + 1] for c in chains)

    res = lax.fori_loop(0, i, lambda jj, carry: tiles(i - 1 - jj, False, carry), tiles(i, True, None))
    for c in range(n_chain):
        acc = acc_ref[c]
        out_t = jnp.concatenate([acc[:HEAD_DIM, :tq], acc[HEAD_DIM:, tq:]], axis=0)
        o_ref[:, c * LANES:(c + 1) * LANES] = out_t.T.astype(o_ref.dtype)


def _sb_attn(qkv, tri, tq):
    b, s, n3 = qkv.shape
    w = LANES * SB_CHAINS
    nblk = n3 // 3 // w
    return pl.pallas_call(
        functools.partial(_sb_attn_kernel, tq=tq),
        grid=(b, nblk, s // tq),
        in_specs=[
            pl.BlockSpec((None, tq, w), lambda bi, p, i: (bi, i, p)),
            pl.BlockSpec((None, s, w), lambda bi, p, i: (bi, 0, nblk + p)),
            pl.BlockSpec((None, s, w), lambda bi, p, i: (bi, 0, 2 * nblk + p)),
            _const_spec(tri.shape),
        ],
        out_specs=pl.BlockSpec((None, tq, w), lambda bi, p, i: (bi, i, p)),
        out_shape=jax.ShapeDtypeStruct((b, s, nblk * w), BF16),
        scratch_shapes=[pltpu.VMEM((SB_CHAINS, s // tq, LANES, tq), BF16),
                        pltpu.VMEM((SB_CHAINS, LANES, 2 * tq), F32)],
        compiler_params=_params("parallel", "parallel", "arbitrary"),
        name="sb_attn",
    )(qkv, qkv, qkv, tri)


def _mix_ffn_kernel(*refs, final, tf, fuse_next):
    if fuse_next:
        (h_ref, o_ref, wo_ref, g_ref, w1_ref, w3_ref, w2_ref, fg_ref, gn_ref, wn_ref,
         out_ref, qkv_ref, a_ref, acc_ref) = refs
    else:
        h_ref, o_ref, wo_ref, g_ref, w1_ref, w3_ref, w2_ref, fg_ref, out_ref, a_ref, acc_ref = refs
    h1 = h_ref[...] + _dot(o_ref[...], wo_ref[...])
    a_ref[...] = _rmsnorm(h1, g_ref[...]).astype(BF16)
    acc_ref[...] = h1

    for c in range(w1_ref.shape[1] // tf):
        cols = slice(c * tf, (c + 1) * tf)
        a = a_ref[...]
        u = _dot(a, w1_ref[:, cols])
        v = _dot(a, w3_ref[:, cols])
        t = (u * jax.nn.sigmoid(u)) * v
        acc_ref[...] += _dot(t.astype(BF16), w2_ref[cols, :])
    h2 = acc_ref[...]
    out_ref[...] = _rmsnorm(h2, fg_ref[...]) if final else h2
    if fuse_next:
        qkv_ref[...] = _dot(_rmsnorm(h2, gn_ref[...]).astype(BF16), wn_ref[...]).astype(qkv_ref.dtype)


def _mix_ffn(h, o, wo, g, w1, w3, w2, fg, tm, tf, final, next_proj=None, layer=0):
    r, d = h.shape
    single = pl.Buffered(1)

    def wspec(shape):
        nd = len(shape)
        return pl.BlockSpec(shape, lambda *_: (0,) * nd, pipeline_mode=single)

    def lspec(shape):
        return pl.BlockSpec((None,) + tuple(shape[1:]), lambda *_: (layer, 0, 0), pipeline_mode=single)

    in_specs = [
        pl.BlockSpec((tm, d), lambda i: (i, 0)),
        pl.BlockSpec((tm, d), lambda i: (i, 0)),
        wspec(wo.shape),
        _const_spec((1, d)),
        lspec(w1.shape),
        lspec(w3.shape),
        lspec(w2.shape),
        _const_spec((1, d)),
    ]
    out_specs = pl.BlockSpec((tm, d), lambda i: (i, 0))
    out_shape = jax.ShapeDtypeStruct((r, d), F32)
    args = (h, o, wo, g, w1, w3, w2, fg)
    if next_proj is not None:
        gn, wn = next_proj
        in_specs += [_const_spec((1, d)), wspec(wn.shape)]
        out_specs = [out_specs, pl.BlockSpec((tm, wn.shape[1]), lambda i: (i, 0))]
        out_shape = [out_shape, jax.ShapeDtypeStruct((r, wn.shape[1]), BF16)]
        args += (gn, wn)
    return pl.pallas_call(
        functools.partial(_mix_ffn_kernel, final=final, tf=tf, fuse_next=next_proj is not None),
        grid=(r // tm,),
        in_specs=in_specs,
        out_specs=out_specs,
        out_shape=out_shape,
        scratch_shapes=[pltpu.VMEM((tm, d), BF16), pltpu.VMEM((tm, d), F32)],
        compiler_params=_params("parallel"),
        name="mix_ffn",
    )(*args)


def _nsa_layout(d):
    kvw = N_GROUPS * HEAD_DIM
    off = {"q": 0, "kc": d, "vc": d + kvw, "ksl": d + 2 * kvw, "vsl": d + 3 * kvw, "kw": d + 4 * kvw,
           "vw": d + 5 * kvw, "gt": d + 6 * kvw}
    main = [np.arange(d)]
    for pair in (("ksl", "kw"), ("vsl", "vw")):
        for g in range(N_GROUPS):
            for name in pair:
                main.append(off[name] + g * HEAD_DIM + np.arange(HEAD_DIM))
    main = np.concatenate(main)
    cv = off["kc"] + np.arange(2 * kvw)
    scale = np.where(main < d, HEAD_DIM ** -0.5 * LOG2E, 1.0).astype(np.float32)
    return main, scale, cv, off["gt"]


def _bf16_pieces(x):
    x = np.asarray(x, np.float32)
    out = []
    for _ in range(3):
        p = x.astype(BF16).astype(np.float32)
        out.append(p)
        x = (x - p).astype(np.float32)
    return out


def _nsa_tables(s, tk):
    n_slc = s // SLC_BLOCK
    n_cmp = s // CMP_STRIDE - CMP_BLOCK // CMP_STRIDE + 1
    nc = s // CMP_STRIDE
    pos = np.arange(s)
    kpos = np.zeros((s, LANES), np.float32)
    assert n_slc <= POS_LANE - HEAD_DIM
    kpos[pos, HEAD_DIM + pos // SLC_BLOCK] = 1.0
    for c in range(3):
        kpos[:, POS_LANE + c] = (pos // tk) * tk
        kpos[:, POS_LANE + 3 + c] = pos % tk
    cs = np.arange(n_cmp) * CMP_STRIDE
    ss = np.arange(n_slc) * SLC_BLOCK
    ov = np.clip(np.minimum(cs[:, None] + CMP_BLOCK, ss[None] + SLC_BLOCK) - np.maximum(cs[:, None], ss[None]), 0, None)
    jp = -(-n_slc // 16) * 16
    cmapt = np.zeros((jp, nc), np.float32)
    cmapt[:n_slc, :n_cmp] = (ov / CMP_STRIDE).T
    slopes = np.asarray(2.0 ** (-8.0 * (np.arange(N_HEADS) + 1) / N_HEADS), np.float32)
    slopes = (slopes.astype(np.float64) * LOG2E).astype(np.float32)
    pieces = _bf16_pieces(slopes)
    hconst = np.zeros((N_GROUPS, 2 * HPG, LANES), np.float32)
    hcol = np.zeros((N_GROUPS, HPG, HEAD_DIM, LANES), np.float32)
    for g in range(N_GROUPS):
        for hh in range(HPG):
            for c in range(3):
                hcol[g, hh, POS_LANE - HEAD_DIM + c, :] = pieces[c][g * HPG + hh]
                hcol[g, hh, POS_LANE - HEAD_DIM + 3 + c, :] = pieces[c][g * HPG + hh]
            hconst[g, hh, :] = slopes[g * HPG + hh]
    n_kt = s // tk
    nwt = min(WINDOW // tk + 1, n_kt)
    tt = np.arange(tk)[None, :]

    def dist(r, nt):
        return r * tk + tt - np.arange(nt * tk)[:, None]

    wbias = np.stack([np.where((dist(r, nwt) >= 0) & (dist(r, nwt) < WINDOW), 0.0, NEG) for r in range(nwt)])
    sbias = np.stack([np.where(dist(r, 2) >= 0, 0.0, NEG) for r in range(2)])
    return (jnp.asarray(kpos, BF16), jnp.asarray(cmapt, BF16), jnp.asarray(hconst, F32), jnp.asarray(hcol, F32),
            jnp.asarray(wbias, F32), jnp.asarray(sbias, F32))


def _ffn_chunk(d_ff):
    for tf in (512, 256, 128):
        if d_ff % tf == 0:
            return tf
    return d_ff


def kernel(x, mix_norm_g, ffn_norm_g, final_norm_g, nsa_w_in, nsa_gate_b, nsa_cmp_pe, nsa_cmp_w1, nsa_cmp_w2,
           nsa_w_out, sb_w_in, sb_w_out, ffn_w1, ffn_w3, ffn_w2):
    b, s, d = x.shape
    depth = mix_norm_g.shape[0]
    d_ff = ffn_w1.shape[2]
    tm = min(512, s)
    tm_proj = min(1024, s)
    tq = min(256, s)
    tf = _ffn_chunk(d_ff)

    main_idx, main_scale, cv_idx, gt_off = _nsa_layout(d)
    sb_scale = np.where(np.arange(3 * d) < d, HEAD_DIM ** -0.5 * LOG2E, 1.0).astype(np.float32)
    n_gate = N_BRANCH * N_HEADS
    kpos, cmapt, hconst, hcol, wbias, sbias = _nsa_tables(s, tq)
    tri = jnp.asarray(np.concatenate([-np.triu(np.ones((tq, tq), np.float32), 1),
                                      -np.ones((16, tq), np.float32)], axis=0), BF16)
    chunk = CMP_STRIDE * HEAD_DIM
    nc = s // CMP_STRIDE

    ffn_w1b, ffn_w3b, ffn_w2b = ffn_w1.astype(BF16), ffn_w3.astype(BF16), ffn_w2.astype(BF16)
    h, qkv = x, None
    for i in range(depth):
        j = i // 2
        gm = mix_norm_g[i].reshape(1, d)
        if i % 2 == 0:
            w_in = nsa_w_in[j]
            w_main = (w_in[:, main_idx] * main_scale).astype(BF16)
            w_cv = w_in[:, cv_idx].astype(BF16)
            w_gate = jnp.pad(w_in[:, gt_off:gt_off + n_gate], ((0, 0), (0, LANES - n_gate))).astype(BF16)
            gate_b = jnp.pad(nsa_gate_b[j], (0, LANES - n_gate)).reshape(1, LANES)
            main, cv, gates = _nsa_proj(h, gm, w_main, w_cv, w_gate, gate_b, tm_proj)
            c = cv.reshape(b, 2, N_GROUPS // 2 * nc, 2 * chunk)
            pe = jnp.broadcast_to(nsa_cmp_pe[j].reshape(2, 2, CMP_STRIDE, 1, HEAD_DIM),
                                  (2, 2, CMP_STRIDE, 2, HEAD_DIM)).reshape(2, 2, 2 * chunk)
            w1 = nsa_cmp_w1[j].reshape(2, 2, CMP_STRIDE, 1, HEAD_DIM, -1)
            w1 = jnp.stack([jnp.concatenate([w1, jnp.zeros_like(w1)], axis=3),
                            jnp.concatenate([jnp.zeros_like(w1), w1], axis=3)], axis=2)
            w1 = w1.reshape(2, 2, 2, 2 * chunk, -1)
            w2pad = jnp.zeros((2, nsa_cmp_w2.shape[2], LANES), F32)
            w2pad = w2pad.at[0, :, :HEAD_DIM].set(nsa_cmp_w2[j, 0]).at[1, :, HEAD_DIM:].set(nsa_cmp_w2[j, 1])
            cmp = _nsa_compress(c, pe, w1.astype(BF16), w2pad.astype(BF16))
            o = _nsa_attn(main, cmp, gates, kpos, cmapt, hconst, hcol, wbias, sbias, tq)
            w_out = nsa_w_out[j]
        else:
            if qkv is None:
                qkv = _sb_proj(h, gm, (sb_w_in[j] * sb_scale).astype(BF16), tm_proj)
            o = _sb_attn(qkv, tri, tq)
            w_out = sb_w_out[j]
        final = i == depth - 1
        next_proj = None
        if i % 2 == 0 and not final:
            next_proj = (mix_norm_g[i + 1].reshape(1, d), (sb_w_in[(i + 1) // 2] * sb_scale).astype(BF16))
        res = _mix_ffn(h.reshape(b * s, d), o.reshape(b * s, d), w_out.astype(BF16), ffn_norm_g[i].reshape(1, d),
                       ffn_w1b, ffn_w3b, ffn_w2b, final_norm_g.reshape(1, d), tm, tf, final, next_proj, i)
        h, qkv = (res[0], res[1].reshape(b, s, -1)) if next_proj is not None else (res, None)
        h = h.reshape(b, s, d)
    return h
```

```python
import functools

import numpy as np
import jax
import jax.numpy as jnp
from jax import lax
from jax.experimental import pallas as pl
from jax.experimental.pallas import tpu as pltpu

F32 = jnp.float32
BF16 = jnp.bfloat16

RMS_EPS = 1e-6
NEG = -1e30
BIG = 1e30

N_HEADS = 16
HEAD_DIM = 64
N_GROUPS = 4
HPG = N_HEADS // N_GROUPS
N_BRANCH = 3
CMP_BLOCK = 32
CMP_STRIDE = 16
SLC_BLOCK = 64
N_SLC_BLOCKS = 8
WINDOW = 512

LANES = 128
POS_LANE = 96
LOG2E = 1.4426950408889634
VMEM_LIMIT = 56 * 1024 * 1024
SB_CHAINS = 8
NSA_CHAINS = 4
SB_LAG = SB_CHAINS
NSA_LAG = 2

_NT = (((1,), (1,)), ((), ()))


def _dot(a, b):
    return jnp.dot(a, b, preferred_element_type=F32)


def _dot_nt(a, b):
    return lax.dot_general(a, b, _NT, preferred_element_type=F32)


def _rmsnorm(x, g):
    ms = jnp.mean(x * x, axis=-1, keepdims=True)
    return (x * lax.rsqrt(ms + RMS_EPS)) * g


def _split_bf16(x):
    hi = x.astype(BF16)
    lo = (x - hi.astype(F32)).astype(BF16)
    return hi, lo


def _neg_abs(x):
    return lax.bitcast_convert_type(lax.bitcast_convert_type(x, jnp.uint32) | jnp.uint32(0x80000000), F32)


def _params(*sem):
    return pltpu.CompilerParams(dimension_semantics=sem, vmem_limit_bytes=VMEM_LIMIT)


def _const_spec(shape):
    nd = len(shape)
    return pl.BlockSpec(shape, lambda *_: (0,) * nd)


def _nsa_proj_kernel(x_ref, g_ref, wm_ref, wc_ref, wg_ref, gb_ref, main_ref, cv_ref, gate_ref, cv_tmp_ref):
    a = _rmsnorm(x_ref[...], g_ref[...]).astype(BF16)
    main_ref[...] = _dot(a, wm_ref[...]).astype(main_ref.dtype)
    cv = _dot(a, wc_ref[...])
    n_chunks = cv.shape[0] // CMP_STRIDE
    for kv in range(2):
        for gp in range(N_GROUPS // 2):
            c0 = (kv * N_GROUPS + 2 * gp) * HEAD_DIM
            cv_tmp_ref[kv, gp] = cv[:, c0:c0 + LANES]
            for l in range(CMP_STRIDE):
                cv_ref[kv, gp, :, l * LANES:(l + 1) * LANES] = cv_tmp_ref[kv, gp,
                                                                          pl.ds(l, n_chunks, stride=CMP_STRIDE), :]
    gate_ref[...] = jax.nn.sigmoid(_dot(a, wg_ref[...]) + gb_ref[...])


def _nsa_proj(h, g, w_main, w_cv, w_gate, gate_b, tm):
    b, s, d = h.shape
    nm, ng = w_main.shape[1], w_gate.shape[1]
    return pl.pallas_call(
        _nsa_proj_kernel,
        grid=(b, s // tm),
        in_specs=[
            pl.BlockSpec((None, tm, d), lambda bi, r: (bi, r, 0)),
            _const_spec((1, d)),
            _const_spec(w_main.shape),
            _const_spec(w_cv.shape),
            _const_spec(w_gate.shape),
            _const_spec((1, ng)),
        ],
        out_specs=[
            pl.BlockSpec((None, tm, nm), lambda bi, r: (bi, r, 0)),
            pl.BlockSpec((None, 2, N_GROUPS // 2, tm // CMP_STRIDE, CMP_STRIDE * LANES),
                         lambda bi, r: (bi, 0, 0, r, 0)),
            pl.BlockSpec((None, tm, ng), lambda bi, r: (bi, r, 0)),
        ],
        out_shape=[
            jax.ShapeDtypeStruct((b, s, nm), BF16),
            jax.ShapeDtypeStruct((b, 2, N_GROUPS // 2, s // CMP_STRIDE, CMP_STRIDE * LANES), F32),
            jax.ShapeDtypeStruct((b, s, ng), F32),
        ],
        scratch_shapes=[pltpu.VMEM((2, N_GROUPS // 2, tm, LANES), F32)],
        compiler_params=_params("parallel", "parallel"),
        name="nsa_proj",
    )(h, g, w_main, w_cv, w_gate, gate_b)


def _sb_proj_kernel(x_ref, g_ref, w_ref, o_ref):
    a = _rmsnorm(x_ref[...], g_ref[...]).astype(BF16)
    o_ref[...] = _dot(a, w_ref[...]).astype(o_ref.dtype)


def _sb_proj(h, g, w, tm):
    b, s, d = h.shape
    n = w.shape[1]
    return pl.pallas_call(
        _sb_proj_kernel,
        grid=(b, s // tm),
        in_specs=[
            pl.BlockSpec((None, tm, d), lambda bi, r: (bi, r, 0)),
            _const_spec((1, d)),
            _const_spec(w.shape),
        ],
        out_specs=pl.BlockSpec((None, tm, n), lambda bi, r: (bi, r, 0)),
        out_shape=jax.ShapeDtypeStruct((b, s, n), BF16),
        compiler_params=_params("parallel", "parallel"),
        name="sb_proj",
    )(h, g, w)


def _nsa_compress_kernel(c_ref, pe_ref, w1_ref, w2_ref, o_ref):
    rows = c_ref.shape[1]
    nc = rows // (N_GROUPS // 2)
    out = [jnp.zeros((rows, LANES), F32) for _ in range(2)]
    for kv in range(2):
        c = c_ref[kv]
        top = (c + pe_ref[kv, 0:1, :]).astype(BF16)
        bot = (c + pe_ref[kv, 1:2, :]).astype(BF16)
        for e in range(2):
            a_top = _dot(top, w1_ref[kv, 0, e])
            a_bot = _dot(bot, w1_ref[kv, 1, e])
            hid = jax.nn.gelu(a_top + pltpu.roll(a_bot, rows - 1, 0), approximate=True)
            out[e] = out[e] + _dot(hid.astype(BF16), w2_ref[kv])
    for gp in range(N_GROUPS // 2):
        for e in range(2):
            g = 2 * gp + e
            o_ref[g * nc:(g + 1) * nc, :] = out[e][gp * nc:(gp + 1) * nc]


def _nsa_compress(c, pe, w1, w2pad):
    b, _, rows, width = c.shape
    return pl.pallas_call(
        _nsa_compress_kernel,
        grid=(b,),
        in_specs=[
            pl.BlockSpec((None, 2, rows, width), lambda bi: (bi, 0, 0, 0)),
            _const_spec(pe.shape),
            _const_spec(w1.shape),
            _const_spec(w2pad.shape),
        ],
        out_specs=pl.BlockSpec((None, 2 * rows, LANES), lambda bi: (bi, 0, 0)),
        out_shape=jax.ShapeDtypeStruct((b, 2 * rows, LANES), F32),
        compiler_params=_params("parallel"),
        name="nsa_compress",
    )(c, pe, w1, w2pad)


def _nsa_attn_kernel(*refs, tq, n_slc, n_sel, n_chain):
    (q_ref, kk_ref, vv_ref, cmp_ref, gate_ref, kpos_ref, cmapt_ref, hconst_ref, hcol_ref, wb_ref, sb_ref,
     o_ref, kaug_s_ref, kaug_w_ref, vts_ref, vtw_ref, qsel_ref, qwin_ref, acc_ref) = refs
    chains = range(n_chain)
    i = pl.program_id(2)
    t0 = i * tq
    tk = tq
    n_kt = kaug_s_ref.shape[1] // tk
    gw = HPG * HEAD_DIM

    @pl.when(i == 0)
    def _():
        top = lax.broadcasted_iota(jnp.int32, (LANES, tk), 0) < HEAD_DIM
        for c in chains:
            kk = kk_ref[:, c * LANES:(c + 1) * LANES].astype(F32)
            first = lax.broadcasted_iota(jnp.int32, kk.shape, 1) < HEAD_DIM
            aux = kpos_ref[...].astype(F32)
            kaug_s_ref[c] = jnp.where(first, kk, aux).astype(BF16)
            kaug_w_ref[c] = jnp.where(first, pltpu.roll(kk, HEAD_DIM, 1), aux).astype(BF16)
            for j in range(n_kt):
                vt = vv_ref[j * tk:(j + 1) * tk, c * LANES:(c + 1) * LANES].astype(F32).T
                vts_ref[c, j] = jnp.where(top, vt, 1.0).astype(BF16)
                vtw_ref[c, j] = jnp.where(top, 1.0, vt).astype(BF16)

    q_t = [q_ref[:, c * gw:(c + 1) * gw].astype(F32).T for c in chains]

    def q_operand(c, h, aux_t):
        return jnp.concatenate([q_t[c][h * HEAD_DIM:(h + 1) * HEAD_DIM], aux_t], axis=0).astype(BF16)

    def aux_rows(c, h):
        return jnp.concatenate([hcol_ref[c, h]] * (tq // LANES), axis=1)

    def scores(kref, qref, c, j0, nt):
        kt = kref[c, pl.ds(pl.multiple_of(j0 * tk, tk), nt * tk), :]
        return _dot(kt, qref[c])

    def pv(vt_ref, c, j0, nt, p):
        acc = _dot(vt_ref[c, j0], p[0:tk])
        for u in range(1, nt):
            acc = acc + _dot(vt_ref[c, j0 + u], p[u * tk:(u + 1) * tk])
        return acc

    def heads(bias):
        return jnp.concatenate([bias] * HPG, axis=1)

    for c in chains:
        for h in range(HPG):
            qwin_ref[c, :, h * tq:(h + 1) * tq] = q_operand(c, h, aux_rows(c, h))
    nwt = min(WINDOW // tk + 1, n_kt)
    jw = jnp.clip(i - (nwt - 1), 0, n_kt - nwt)
    wbias = wb_ref[i - jw]

    nc = cmp_ref.shape[0] // n_chain
    ck = [cmp_ref[c * nc:(c + 1) * nc, :] for c in chains]
    first_c = lax.broadcasted_iota(jnp.int32, (nc, LANES), 1) < HEAD_DIM
    sc_all = [_dot(jnp.where(first_c, ck[c], 0.0).astype(BF16), qwin_ref[c]) for c in chains]
    s_w = [scores(kaug_w_ref, qwin_ref, c, jw, nwt) for c in chains]

    def window_head(c, h):
        s_h = s_w[c][:, h * tq:(h + 1) * tq] + wbias
        p_h = jnp.exp2(s_h - jnp.max(s_h, axis=0, keepdims=True)).astype(BF16)
        acc = pv(vtw_ref, c, jw, nwt, p_h)
        return acc[HEAD_DIM:] * (1.0 / acc[0:1])

    dist_c = ((t0 + lax.broadcasted_iota(jnp.int32, (nc, tq), 1))
              - (lax.broadcasted_iota(jnp.int32, (nc, tq), 0) * CMP_STRIDE + (CMP_BLOCK - 1)))
    valid_c = dist_c >= 0
    dist_cf = dist_c.astype(F32)
    psum, p_heads = [], []
    for c in chains:
        ps = jnp.zeros((nc, tq), F32)
        ph = []
        for h in range(HPG):
            slope = hconst_ref[c, h:h + 1, 0:1]
            s_c = jnp.where(valid_c, sc_all[c][:, h * tq:(h + 1) * tq] - slope * dist_cf, NEG)
            m_c = jnp.max(s_c, axis=0, keepdims=True)
            e_c = jnp.where(valid_c, jnp.exp2(s_c - m_c), 0.0)
            den = jnp.sum(e_c, axis=0, keepdims=True)
            p_c = e_c / jnp.maximum(den, 1e-30)
            ps = ps + p_c
            ph.append(p_c.astype(BF16))
        psum.append(ps)
        p_heads.append(ph)
    o_cmp_t = [_dot(ck[c].T.astype(BF16), jnp.concatenate(p_heads[c], axis=1))[HEAD_DIM:] for c in chains]

    cmt = cmapt_ref[...]
    jp = cmt.shape[0]
    imp_t = []
    for c in chains:
        p_hi, p_lo = _split_bf16(psum[c])
        imp_t.append(_dot(cmt, p_hi) + _dot(cmt, p_lo))
    jb = lax.broadcasted_iota(jnp.int32, (jp, tq), 0)
    cur = jnp.right_shift(t0 + lax.broadcasted_iota(jnp.int32, (jp, tq), 1), SLC_BLOCK.bit_length() - 1)
    causal_j = jb <= cur
    forced = (jb == 0) | (jb == cur) | (jb == cur - 1)
    val = [jnp.where(causal_j & forced, BIG, jnp.where(causal_j, imp_t[c], -BIG)) for c in chains]
    rank = [jnp.zeros((jp, tq), jnp.int32) for c in chains]
    o_win_heads = [[] for c in chains]
    for j in range(n_slc):
        if j % (n_slc // HPG) == 0:
            for c in chains:
                o_win_heads[c].append(window_head(c, j // (n_slc // HPG)))
        for c in chains:
            row = val[c][j:j + 1, :]
            beats = (row > val[c]) | ((row == val[c]) & (jb > j))
            rank[c] = rank[c] + beats.astype(jnp.int32)
    o_win_t = [jnp.concatenate(o_win_heads[c], axis=1) for c in chains]

    for c in chains:
        mb_t = jnp.where(causal_j & (rank[c] < n_sel), 0.0, NEG)
        mb_t = jnp.concatenate([mb_t, jnp.zeros((HEAD_DIM - jp, tq), F32)], axis=0)
        for h in range(HPG):
            qsel_ref[c, :, h * tq:(h + 1) * tq] = q_operand(c, h, mb_t + aux_rows(c, h))
    pair = i // 2
    sbias = heads(sb_ref[i - 2 * pair])

    def sel_chunk(j0, carry):
        s, out = [None] * n_chain, [None] * n_chain

        def stage_scores(c):
            s[c] = scores(kaug_s_ref, qsel_ref, c, j0, 2)
            if carry is None:
                s[c] = s[c] + sbias

        def stage_pv(c):
            m_tile = jnp.max(s[c], axis=0, keepdims=True)
            if carry is None:
                out[c] = m_tile
                acc_ref[c] = pv(vts_ref, c, j0, 2, jnp.exp2(s[c] - m_tile).astype(BF16))
            else:
                out[c] = jnp.maximum(carry[c], m_tile)
                p = jnp.exp2(s[c] - out[c]).astype(BF16)
                acc_ref[c] = jnp.exp2(carry[c] - out[c]) * acc_ref[c] + pv(vts_ref, c, j0, 2, p)

        for step in range(n_chain + NSA_LAG):
            if step < n_chain:
                stage_scores(step)
            if 0 <= step - NSA_LAG < n_chain:
                stage_pv(step - NSA_LAG)
        return tuple(out)

    lax.fori_loop(0, pair, lambda jj, carry: sel_chunk(2 * (pair - 1 - jj), carry), sel_chunk(2 * pair, None))

    gt_t = gate_ref[...].T
    for c in chains:
        acc_s = acc_ref[c]
        o_sel_t = acc_s[:HEAD_DIM] * (1.0 / acc_s[HEAD_DIM:HEAD_DIM + 1])
        rows = []
        for h in range(HPG):
            sl = slice(h * tq, (h + 1) * tq)
            g_cmp, g_sel, g_win = (gt_t[br * N_HEADS + c * HPG + h:br * N_HEADS + c * HPG + h + 1]
                                   for br in range(N_BRANCH))
            rows.append(g_cmp * o_cmp_t[c][:, sl] + g_sel * o_sel_t[:, sl] + g_win * o_win_t[c][:, sl])
        o_ref[:, c * gw:(c + 1) * gw] = jnp.concatenate(rows, axis=0).T.astype(o_ref.dtype)


def _nsa_attn(main, cmp, gates, kpos, cmapt, hconst, hcol, wbias, sbias, tq):
    b, s, _ = main.shape
    assert (s // tq) % 2 == 0 and NSA_CHAINS == N_GROUPS
    nch = NSA_CHAINS
    nc = cmp.shape[1] // N_GROUPS
    n_slc = s // SLC_BLOCK
    gw = HPG * HEAD_DIM
    qw, kvw = N_GROUPS * gw, N_GROUPS * LANES
    assert qw % kvw == 0

    kern =functools.partial(_nsa_attn_kernel, tq=tq, n_slc=n_slc, n_sel=min(N_SLC_BLOCKS, n_slc), n_chain=nch)
    return pl.pallas_call(
        kern,
        grid=(b, N_GROUPS // nch, s // tq),
        in_specs=(
            [
                pl.BlockSpec((None, tq, qw), lambda bi, gp, i: (bi, i, 0)),
                pl.BlockSpec((None, s, kvw), lambda bi, gp, i: (bi, 0, qw // kvw)),
                pl.BlockSpec((None, s, kvw), lambda bi, gp, i: (bi, 0, qw // kvw + 1)),
            ]
            + [
                pl.BlockSpec((None, nch * nc, LANES), lambda bi, gp, i: (bi, gp, 0)),
                pl.BlockSpec((None, tq, LANES), lambda bi, gp, i: (bi, i, 0)),
                _const_spec(kpos.shape),
                _const_spec(cmapt.shape),
                pl.BlockSpec((nch, 2 * HPG, LANES), lambda bi, gp, i: (gp, 0, 0)),
                pl.BlockSpec((nch, HPG, HEAD_DIM, LANES), lambda bi, gp, i: (gp, 0, 0, 0)),
                _const_spec(wbias.shape),
                _const_spec(sbias.shape),
            ]),
        out_specs=pl.BlockSpec((None, tq, nch * gw), lambda bi, gp, i: (bi, i, gp)),
        out_shape=jax.ShapeDtypeStruct((b, s, N_GROUPS * gw), BF16),
        scratch_shapes=[
            pltpu.VMEM((nch, s, LANES), BF16),
            pltpu.VMEM((nch, s, LANES), BF16),
            pltpu.VMEM((nch, s // tq, LANES, tq), BF16),
            pltpu.VMEM((nch, s // tq, LANES, tq), BF16),
            pltpu.VMEM((nch, LANES, HPG * tq), BF16),
            pltpu.VMEM((nch, LANES, HPG * tq), BF16),
            pltpu.VMEM((nch, LANES, HPG * tq), F32),
        ],
        compiler_params=_params("parallel", "parallel", "arbitrary"),
        name="nsa_attn",
    )(main, main, main, cmp, gates, kpos, cmapt, hconst, hcol, wbias, sbias)


def _sb_attn_kernel(q_ref, k_ref, v_ref, tri_ref, o_ref, vt_ref, acc_ref, *, tq):
    i = pl.program_id(2)
    tk = tq
    n_kt = k_ref.shape[0] // tk
    n_chain = q_ref.shape[1] // LANES

    @pl.when(i == 0)
    def _():
        for c in range(n_chain):
            for j in range(n_kt):
                vt_ref[c, j] = v_ref[j * tk:(j + 1) * tk, c * LANES:(c + 1) * LANES].astype(F32).T.astype(BF16)

    low = lax.broadcasted_iota(jnp.int32, (tq, LANES), 1) < HEAD_DIM
    before = (lax.broadcasted_iota(jnp.int32, (tk, 2 * tq), 0)
              < (lax.broadcasted_iota(jnp.int32, (tk, 2 * tq), 1) & (tq - 1)))
    tri = tri_ref[...]

    qst = []
    for c in range(n_chain):
        q2 = q_ref[:, c * LANES:(c + 1) * LANES].astype(F32)
        qst.append(jnp.concatenate([jnp.where(low, q2, 0.0), jnp.where(low, 0.0, q2)], axis=0).astype(BF16))

    chains = range(n_chain)

    def tiles(j, diag, carry):
        off = pl.multiple_of(j * tk, tk)
        z, sp, sums = [None] * n_chain, [None] * n_chain, [None] * n_chain

        def scores(c):
            z[c] = _dot_nt(k_ref[pl.ds(off, tk), c * LANES:(c + 1) * LANES], qst[c])
            sp[c] = jnp.maximum(z[c], 0.0) + jnp.log(1.0 + jnp.exp2(_neg_abs(z[c]))) * LOG2E
            if diag:
                sp[c] = jnp.where(before, sp[c], 0.0)

        def suffix(c):
            sums[c] = _dot(tri, sp[c].astype(BF16))

        def weights(c):
            if diag:
                a = jnp.where(before, jnp.exp2(z[c] - sp[c] + sums[c][:tk]), 0.0)
            else:
                a = jnp.exp2(z[c] - sp[c] + sums[c][:tk] + carry[c])
            pv = _dot(vt_ref[c, j], a.astype(BF16))
            if diag:
                acc_ref[c] = pv
            else:
                acc_ref[c] += pv

        for step in range(n_chain + 2 * SB_LAG):
            for d, stage in enumerate((scores, suffix, weights)):
                if 0 <= step - d * SB_LAG < n_chain:
                    stage(step - d * SB_LAG)
        if diag:
            return tuple(sums[c][tk:tk + 1] for c in chains)
        return tuple(carry[c] + sums[c][tk:tk + 1] for c in chains)

    res = lax.fori_loop(0, i, lambda jj, carry: tiles(i - 1 - jj, False, carry), tiles(i, True, None))
    for c in range(n_chain):
        acc = acc_ref[c]
        out_t = jnp.concatenate([acc[:HEAD_DIM, :tq], acc[HEAD_DIM:, tq:]], axis=0)
        o_ref[:, c * LANES:(c + 1) * LANES] = out_t.T.astype(o_ref.dtype)


def _sb_attn(qkv, tri, tq):
    b, s, n3 = qkv.shape
    w = LANES * SB_CHAINS
    nblk = n3 // 3 // w
    return pl.pallas_call(
        functools.partial(_sb_attn_kernel, tq=tq),
        grid=(b, nblk, s // tq),
        in_specs=[
            pl.BlockSpec((None, tq, w), lambda bi, p, i: (bi, i, p)),
            pl.BlockSpec((None, s, w), lambda bi, p, i: (bi, 0, nblk + p)),
            pl.BlockSpec((None, s, w), lambda bi, p, i: (bi, 0, 2 * nblk + p)),
            _const_spec(tri.shape),
        ],
        out_specs=pl.BlockSpec((None, tq, w), lambda bi, p, i: (bi, i, p)),
        out_shape=jax.ShapeDtypeStruct((b, s, nblk * w), BF16),
        scratch_shapes=[pltpu.VMEM((SB_CHAINS, s // tq, LANES, tq), BF16),
                        pltpu.VMEM((SB_CHAINS, LANES, 2 * tq), F32)],
        compiler_params=_params("parallel", "parallel", "arbitrary"),
        name="sb_attn",
    )(qkv, qkv, qkv, tri)


def _mix_ffn_kernel(*refs, final, tf, fuse_next):
    if fuse_next:
        (h_ref, o_ref, wo_ref, g_ref, w1_ref, w3_ref, w2_ref, fg_ref, gn_ref, wn_ref,
         out_ref, qkv_ref, a_ref, acc_ref) = refs
    else:
        h_ref, o_ref, wo_ref, g_ref, w1_ref, w3_ref, w2_ref, fg_ref, out_ref, a_ref, acc_ref = refs
    h1 = h_ref[...] + _dot(o_ref[...], wo_ref[...])
    a_ref[...] = _rmsnorm(h1, g_ref[...]).astype(BF16)
    acc_ref[...] = h1

    for c in range(w1_ref.shape[1] // tf):
        cols = slice(c * tf, (c + 1) * tf)
        a = a_ref[...]
        u = _dot(a, w1_ref[:, cols])
        v = _dot(a, w3_ref[:, cols])
        t = (u * jax.nn.sigmoid(u)) * v
        acc_ref[...] += _dot(t.astype(BF16), w2_ref[cols, :])
    h2 = acc_ref[...]
    out_ref[...] = _rmsnorm(h2, fg_ref[...]) if final else h2
    if fuse_next:
        qkv_ref[...] = _dot(_rmsnorm(h2, gn_ref[...]).astype(BF16), wn_ref[...]).astype(qkv_ref.dtype)


def _mix_ffn(h, o, wo, g, w1, w3, w2, fg, tm, tf, final, next_proj=None, layer=0):
    r, d = h.shape
    single = pl.Buffered(1)

    def wspec(shape):
        nd = len(shape)
        return pl.BlockSpec(shape, lambda *_: (0,) * nd, pipeline_mode=single)

    def lspec(shape, layer=layer):
        return pl.BlockSpec((None,) + tuple(shape[1:]), lambda *_: (layer, 0, 0), pipeline_mode=single)

    in_specs = [
        pl.BlockSpec((tm, d), lambda i: (i, 0)),
        pl.BlockSpec((tm, d), lambda i: (i, 0)),
        lspec(wo.shape, layer // 2),
        _const_spec((1, d)),
        lspec(w1.shape),
        lspec(w3.shape),
        lspec(w2.shape),
        _const_spec((1, d)),
    ]
    out_specs = pl.BlockSpec((tm, d), lambda i: (i, 0))
    out_shape = jax.ShapeDtypeStruct((r, d), F32)
    args = (h, o, wo, g, w1, w3, w2, fg)
    if next_proj is not None:
        gn, wn = next_proj
        in_specs += [_const_spec((1, d)), wspec(wn.shape)]
        out_specs = [out_specs, pl.BlockSpec((tm, wn.shape[1]), lambda i: (i, 0))]
        out_shape = [out_shape, jax.ShapeDtypeStruct((r, wn.shape[1]), BF16)]
        args += (gn, wn)
    return pl.pallas_call(
        functools.partial(_mix_ffn_kernel, final=final, tf=tf, fuse_next=next_proj is not None),
        grid=(r // tm,),
        in_specs=in_specs,
        out_specs=out_specs,
        out_shape=out_shape,
        scratch_shapes=[pltpu.VMEM((tm, d), BF16), pltpu.VMEM((tm, d), F32)],
        compiler_params=_params("parallel"),
        name="mix_ffn",
    )(*args)


def _nsa_layout(d):
    kvw = N_GROUPS * HEAD_DIM
    off = {"q": 0, "kc": d, "vc": d + kvw, "ksl": d + 2 * kvw, "vsl": d + 3 * kvw, "kw": d + 4 * kvw,
           "vw": d + 5 * kvw, "gt": d + 6 * kvw}
    main = [np.arange(d)]
    for pair in (("ksl", "kw"), ("vsl", "vw")):
        for g in range(N_GROUPS):
            for name in pair:
                main.append(off[name] + g * HEAD_DIM + np.arange(HEAD_DIM))
    main = np.concatenate(main)
    cv = off["kc"] + np.arange(2 * kvw)
    scale = np.where(main < d, HEAD_DIM ** -0.5 * LOG2E, 1.0).astype(np.float32)
    return main, scale, cv, off["gt"]


def _bf16_pieces(x):
    x = np.asarray(x, np.float32)
    out = []
    for _ in range(3):
        p = x.astype(BF16).astype(np.float32)
        out.append(p)
        x = (x - p).astype(np.float32)
    return out


def _nsa_tables(s, tk):
    n_slc = s // SLC_BLOCK
    n_cmp = s // CMP_STRIDE - CMP_BLOCK // CMP_STRIDE + 1
    nc = s // CMP_STRIDE
    pos = np.arange(s)
    kpos = np.zeros((s, LANES), np.float32)
    assert n_slc <= POS_LANE - HEAD_DIM
    kpos[pos, HEAD_DIM + pos // SLC_BLOCK] = 1.0
    for c in range(3):
        kpos[:, POS_LANE + c] = (pos // tk) * tk
        kpos[:, POS_LANE + 3 + c] = pos % tk
    cs = np.arange(n_cmp) * CMP_STRIDE
    ss = np.arange(n_slc) * SLC_BLOCK
    ov = np.clip(np.minimum(cs[:, None] + CMP_BLOCK, ss[None] + SLC_BLOCK) - np.maximum(cs[:, None], ss[None]), 0, None)
    jp = -(-n_slc // 16) * 16
    cmapt = np.zeros((jp, nc), np.float32)
    cmapt[:n_slc, :n_cmp] = (ov / CMP_STRIDE).T
    slopes = np.asarray(2.0 ** (-8.0 * (np.arange(N_HEADS) + 1) / N_HEADS), np.float32)
    slopes = (slopes.astype(np.float64) * LOG2E).astype(np.float32)
    pieces = _bf16_pieces(slopes)
    hconst = np.zeros((N_GROUPS, 2 * HPG, LANES), np.float32)
    hcol = np.zeros((N_GROUPS, HPG, HEAD_DIM, LANES), np.float32)
    for g in range(N_GROUPS):
        for hh in range(HPG):
            for c in range(3):
                hcol[g, hh, POS_LANE - HEAD_DIM + c, :] = pieces[c][g * HPG + hh]
                hcol[g, hh, POS_LANE - HEAD_DIM + 3 + c, :] = pieces[c][g * HPG + hh]
            hconst[g, hh, :] = slopes[g * HPG + hh]
    n_kt = s // tk
    nwt = min(WINDOW // tk + 1, n_kt)
    tt = np.arange(tk)[None, :]

    def dist(r, nt):
        return r * tk + tt - np.arange(nt * tk)[:, None]

    wbias = np.stack([np.where((dist(r, nwt) >= 0) & (dist(r, nwt) < WINDOW), 0.0, NEG) for r in range(nwt)])
    sbias = np.stack([np.where(dist(r, 2) >= 0, 0.0, NEG) for r in range(2)])
    return (jnp.asarray(kpos, BF16), jnp.asarray(cmapt, BF16), jnp.asarray(hconst, F32), jnp.asarray(hcol, F32),
            jnp.asarray(wbias, F32), jnp.asarray(sbias, F32))


def _ffn_chunk(d_ff):
    for tf in (512, 256, 128):
        if d_ff % tf == 0:
            return tf
    return d_ff


def kernel(x, mix_norm_g, ffn_norm_g, final_norm_g, nsa_w_in, nsa_gate_b, nsa_cmp_pe, nsa_cmp_w1, nsa_cmp_w2,
           nsa_w_out, sb_w_in, sb_w_out, ffn_w1, ffn_w3, ffn_w2):
    b, s, d = x.shape
    depth = mix_norm_g.shape[0]
    d_ff = ffn_w1.shape[2]
    tm = min(512, s)
    tm_proj = min(1024, s)
    tq = min(256, s)
    tf = _ffn_chunk(d_ff)

    main_idx, main_scale, cv_idx, gt_off = _nsa_layout(d)
    sb_scale = np.where(np.arange(3 * d) < d, HEAD_DIM ** -0.5 * LOG2E, 1.0).astype(np.float32)
    n_gate = N_BRANCH * N_HEADS
    kpos, cmapt, hconst, hcol, wbias, sbias = _nsa_tables(s, tq)
    tri = jnp.asarray(np.concatenate([-np.triu(np.ones((tq, tq), np.float32), 1),
                                      -np.ones((16, tq), np.float32)], axis=0), BF16)
    chunk = CMP_STRIDE * HEAD_DIM
    nc = s // CMP_STRIDE

    ffn_w1b, ffn_w3b, ffn_w2b = ffn_w1.astype(BF16), ffn_w3.astype(BF16), ffn_w2.astype(BF16)
    nsa_w_outb, sb_w_outb = nsa_w_out.astype(BF16), sb_w_out.astype(BF16)
    h, qkv = x, None
    for i in range(depth):
        j = i // 2
        gm = mix_norm_g[i].reshape(1, d)
        if i % 2 == 0:
            w_in = nsa_w_in[j]
            w_main = (w_in[:, main_idx] * main_scale).astype(BF16)
            w_cv = w_in[:, cv_idx].astype(BF16)
            w_gate = jnp.pad(w_in[:, gt_off:gt_off + n_gate], ((0, 0), (0, LANES - n_gate))).astype(BF16)
            gate_b = jnp.pad(nsa_gate_b[j], (0, LANES - n_gate)).reshape(1, LANES)
            main, cv, gates = _nsa_proj(h, gm, w_main, w_cv, w_gate, gate_b, tm_proj)
            c = cv.reshape(b, 2, N_GROUPS // 2 * nc, 2 * chunk)
            pe = jnp.broadcast_to(nsa_cmp_pe[j].reshape(2, 2, CMP_STRIDE, 1, HEAD_DIM),
                                  (2, 2, CMP_STRIDE, 2, HEAD_DIM)).reshape(2, 2, 2 * chunk)
            w1 = nsa_cmp_w1[j].reshape(2, 2, CMP_STRIDE, 1, HEAD_DIM, -1)
            w1 = jnp.stack([jnp.concatenate([w1, jnp.zeros_like(w1)], axis=3),
                            jnp.concatenate([jnp.zeros_like(w1), w1], axis=3)], axis=2)
            w1 = w1.reshape(2, 2, 2, 2 * chunk, -1)
            w2pad = jnp.zeros((2, nsa_cmp_w2.shape[2], LANES), F32)
            w2pad = w2pad.at[0, :, :HEAD_DIM].set(nsa_cmp_w2[j, 0]).at[1, :, HEAD_DIM:].set(nsa_cmp_w2[j, 1])
            cmp = _nsa_compress(c, pe, w1.astype(BF16), w2pad.astype(BF16))
            o = _nsa_attn(main, cmp, gates, kpos, cmapt, hconst, hcol, wbias, sbias, tq)
            w_out = nsa_w_outb
        else:
            if qkv is None:
                qkv = _sb_proj(h, gm, (sb_w_in[j] * sb_scale).astype(BF16), tm_proj)
            o = _sb_attn(qkv, tri, tq)
            w_out = sb_w_outb
        final = i == depth - 1
        next_proj = None
        if i % 2 == 0 and not final:
            next_proj = (mix_norm_g[i + 1].reshape(1, d), (sb_w_in[(i + 1) // 2] * sb_scale).astype(BF16))
        res = _mix_ffn(h.reshape(b * s, d), o.reshape(b * s, d), w_out, ffn_norm_g[i].reshape(1, d),
                       ffn_w1b, ffn_w3b, ffn_w2b, final_norm_g.reshape(1, d), tm, tf, final, next_proj, i)
        h, qkv = (res[0], res[1].reshape(b, s, -1)) if next_proj is not None else (res, None)
        h = h.reshape(b, s, d)
    return h
```
